```python
import jax, jax.numpy as jnp
from jax import lax
import numpy as np

D_MODEL = 1024
BATCH = 8
SEQ = 2048
DEPTH = 2
DEC_BATCH = 128
DEC_SEQ = 8
PAST_LEN = 2048
PAGE_SIZE = 128

MIX_WIDTH = D_MODEL
N_MIXERS = 4
GROUP_WIDTH = MIX_WIDTH // N_MIXERS
HEAD_DIM = 64
ATT_HEADS = GROUP_WIDTH // HEAD_DIM
MOBA_BLOCK = 256
MOBA_TOP_K = 3
QUERY_BLOCK = 128
POOL_WINDOWS = (2, 4, 8, 16)
POOL_GROUP = GROUP_WIDTH // len(POOL_WINDOWS)
POOL_HIST = max(POOL_WINDOWS) - 1
CONV_WIDTH = 31
CONV_HIST = CONV_WIDTH - 1
GMLP_CHUNK = 128
GMLP_HEADS = 4
GMLP_HEAD_DIM = GROUP_WIDTH // GMLP_HEADS
IN_COLS = 3 * GROUP_WIDTH + GROUP_WIDTH + 2 * GROUP_WIDTH + 2 * GROUP_WIDTH
FFN_DIM = ((8 * D_MODEL // 3 + 127) // 128) * 128
RMS_EPS = 1e-6
LN_EPS = 1e-5

kernel_name = 'hymba_moba_pool_conv_gmlp_step'


def rmsnorm(x, gain):
    xf = x.astype(jnp.float32)
    y = xf * lax.rsqrt(jnp.mean(xf * xf, axis=-1, keepdims=True) + RMS_EPS)
    return (y * gain.astype(jnp.float32)).astype(x.dtype)


def layernorm(x, gain, bias):
    xf = x.astype(jnp.float32)
    mu = jnp.mean(xf, axis=-1, keepdims=True)
    var = jnp.mean(jnp.square(xf - mu), axis=-1, keepdims=True)
    y = (xf - mu) * lax.rsqrt(var + LN_EPS)
    return (y * gain.astype(jnp.float32) + bias.astype(jnp.float32)).astype(x.dtype)


def swiglu(x, w_gate, w_up, w_down):
    return (jax.nn.silu(x @ w_gate) * (x @ w_up)) @ w_down


def alibi_slopes(n_heads):
    return jnp.exp2(-8.0 * jnp.arange(1, n_heads + 1, dtype=jnp.float32) / n_heads)


def moba_attention(q, k_all, v_all, pos0):
    B, T, H, Dh = q.shape
    L = k_all.shape[1]
    n_blk = -(-L // MOBA_BLOCK)
    pad = n_blk * MOBA_BLOCK - L
    kb = jnp.pad(k_all, ((0, 0), (0, pad), (0, 0), (0, 0))).reshape(B, n_blk, MOBA_BLOCK, H, Dh).transpose(0, 3, 1, 2, 4)
    vb = jnp.pad(v_all, ((0, 0), (0, pad), (0, 0), (0, 0))).reshape(B, n_blk, MOBA_BLOCK, H, Dh).transpose(0, 3, 1, 2, 4)
    k_mean = jnp.mean(kb.astype(jnp.float32), axis=3)
    slopes = alibi_slopes(H)
    scale = Dh ** -0.5
    bi = jnp.arange(B)[:, None, None]
    hi = jnp.arange(H)[None, None, :]
    blk_ids = jnp.arange(n_blk, dtype=jnp.int32)
    offs = jnp.arange(MOBA_BLOCK, dtype=jnp.int32)
    n_slots = MOBA_TOP_K + 1

    def attend(args):
        q_c, pos_c = args
        tc = q_c.shape[1]
        own = pos_c // MOBA_BLOCK
        gate = jnp.einsum('bthd,bhnd->bthn', q_c.astype(jnp.float32), k_mean)
        past = (blk_ids[None, :] < own[:, None])[None, :, None, :]
        gate = jnp.where(past, gate, -jnp.inf)
        if n_blk < MOBA_TOP_K:
            gate = jnp.pad(gate, ((0, 0), (0, 0), (0, 0), (0, MOBA_TOP_K - n_blk)), constant_values=-jnp.inf)
        top_val, top_idx = lax.top_k(gate, MOBA_TOP_K)
        sel_ok = jnp.isfinite(top_val)
        top_idx = jnp.minimum(top_idx, n_blk - 1)
        own_idx = jnp.broadcast_to(own[None, :, None], (B, tc, H))
        slots = [top_idx[..., s] for s in range(MOBA_TOP_K)] + [own_idx]
        scores = []
        for s in range(n_slots):
            idx = slots[s]
            k_sel = kb[bi, hi, idx]
            sc = jnp.einsum('bthd,bthjd->bthj', q_c, k_sel, preferred_element_type=jnp.float32) * scale
            key_pos = idx[..., None] * MOBA_BLOCK + offs
            sc = sc - slopes[:, None] * (pos_c[None, :, None, None] - key_pos).astype(jnp.float32)
            if s < MOBA_TOP_K:
                valid = sel_ok[..., s, None]
            else:
                valid = key_pos <= pos_c[None, :, None, None]
            scores.append(jnp.where(valid, sc, -jnp.inf))
        probs = jax.nn.softmax(jnp.concatenate(scores, axis=-1), axis=-1)
        probs = probs.astype(v_all.dtype).reshape(B, tc, H, n_slots, MOBA_BLOCK)
        out = jnp.einsum('bthj,bthjd->bthd', probs[..., 0, :], vb[bi, hi, slots[0]], preferred_element_type=jnp.float32)
        for s in range(1, n_slots):
            out = out + jnp.einsum('bthj,bthjd->bthd', probs[..., s, :], vb[bi, hi, slots[s]], preferred_element_type=jnp.float32)
        return out.astype(q_c.dtype)

    tc = QUERY_BLOCK if T % QUERY_BLOCK == 0 else T
    nq = T // tc
    pos = pos0 + jnp.arange(T, dtype=jnp.int32)
    qs = q.reshape(B, nq, tc, H, Dh).transpose(1, 0, 2, 3, 4)
    out = lax.map(attend, (qs, pos.reshape(nq, tc)))
    return out.transpose(1, 0, 2, 3, 4).reshape(B, T, H * Dh)


def pool_mixer(u_hist, u_new, pool_w, pool_scale):
    B, T, C = u_new.shape
    n_hist = u_hist.shape[1]
    u_cat = jnp.concatenate([u_hist, u_new], axis=1)
    u = u_cat.astype(jnp.float32)
    csum = jnp.concatenate([jnp.zeros((B, 1, C), jnp.float32), jnp.cumsum(u, axis=1)], axis=1)
    win = jnp.repeat(jnp.array(POOL_WINDOWS, dtype=jnp.int32), C // len(POOL_WINDOWS))
    idx = n_hist + jnp.arange(T, dtype=jnp.int32)
    start = jnp.maximum(idx[:, None] + 1 - win[None, :], 0)
    prev = jnp.take_along_axis(csum, jnp.broadcast_to(start[None], (B, T, C)), axis=1)
    count = jnp.minimum(idx[:, None] + 1, win[None, :]).astype(jnp.float32)
    m = (csum[:, n_hist + 1:] - prev) / count - u[:, n_hist:]
    m = m.astype(u_new.dtype).reshape(B, T, len(POOL_WINDOWS), C // len(POOL_WINDOWS))
    out = jnp.einsum('btgc,gcd->btgd', m, pool_w).reshape(B, T, C) * pool_scale
    return out, u_cat[:, -POOL_HIST:]


def conv_module(a_hist, c_in, conv_w, conv_b, ln_g, ln_b, conv_pw):
    a, g = jnp.split(c_in, 2, axis=-1)
    a = a * jax.nn.sigmoid(g)
    full = jnp.concatenate([a_hist, a], axis=1)
    C = a.shape[-1]
    y = lax.conv_general_dilated(full, conv_w[:, None, :], window_strides=(1,), padding='VALID',
                                 dimension_numbers=('NWC', 'WIO', 'NWC'), feature_group_count=C) + conv_b
    y = jax.nn.silu(layernorm(y, ln_g, ln_b))
    return y @ conv_pw, full[:, -CONV_HIST:]


def spatial_gating(d_in, gate_gain, spatial_w, spatial_b):
    B, T, _ = d_in.shape
    u, v = jnp.split(jax.nn.gelu(d_in), 2, axis=-1)
    v = rmsnorm(v, gate_gain)
    n_chunk = -(-T // GMLP_CHUNK)
    vp = jnp.pad(v, ((0, 0), (0, n_chunk * GMLP_CHUNK - T), (0, 0)))
    vp = vp.reshape(B, n_chunk, GMLP_CHUNK, GMLP_HEADS, GMLP_HEAD_DIM)
    w_causal = spatial_w * jnp.tril(jnp.ones((GMLP_CHUNK, GMLP_CHUNK), spatial_w.dtype))
    s = jnp.einsum('gts,bcsgd->bctgd', w_causal, vp) + spatial_b.T[None, None, :, :, None]
    s = s.reshape(B, n_chunk * GMLP_CHUNK, GROUP_WIDTH)[:, :T]
    return u * s, v


def hybrid_layer(x, pos0, k_past, v_past, pool_hist, conv_hist,
                 ffn1_norm, ffn1_w_gate, ffn1_w_up, ffn1_w_down, mix_norm, w_in, q_norm, k_norm,
                 pool_w, pool_scale, conv_w, conv_b, conv_ln_g, conv_ln_b, conv_pw,
                 gate_norm, spatial_w, spatial_b, w_out, ffn2_norm, ffn2_w_gate, ffn2_w_up, ffn2_w_down):
    B, T, _ = x.shape
    G = GROUP_WIDTH
    x = x + 0.5 * swiglu(rmsnorm(x, ffn1_norm), ffn1_w_gate, ffn1_w_up, ffn1_w_down)
    z = rmsnorm(x, mix_norm) @ w_in
    q, k, v, p_in, c_in, d_in = jnp.split(z, [G, 2 * G, 3 * G, 4 * G, 6 * G], axis=-1)
    q = rmsnorm(q.reshape(B, T, ATT_HEADS, HEAD_DIM), q_norm)
    k = rmsnorm(k.reshape(B, T, ATT_HEADS, HEAD_DIM), k_norm)
    v = v.reshape(B, T, ATT_HEADS, HEAD_DIM)
    att = moba_attention(q, jnp.concatenate([k_past, k], axis=1), jnp.concatenate([v_past, v], axis=1), pos0)
    pool, pool_state = pool_mixer(pool_hist, p_in, pool_w, pool_scale)
    conv, conv_state = conv_module(conv_hist, c_in, conv_w, conv_b, conv_ln_g, conv_ln_b, conv_pw)
    gmlp, gate_v = spatial_gating(d_in, gate_norm, spatial_w, spatial_b)
    x = x + jnp.concatenate([att, pool, conv, gmlp], axis=-1) @ w_out
    x = x + 0.5 * swiglu(rmsnorm(x, ffn2_norm), ffn2_w_gate, ffn2_w_up, ffn2_w_down)
    return x, k, v, pool_state, conv_state, gate_v


def setup_inputs(seed: int = 0) -> dict:
    key = jax.random.key(seed)
    keys = iter(jax.random.split(key, 32))

    def nrm(shape, scale):
        return scale * jax.random.normal(next(keys), shape, jnp.float32)

    n_pages = PAST_LEN // PAGE_SIZE
    n_used = DEC_BATCH * n_pages
    n_pool = (n_used * 5) // 4
    G = GROUP_WIDTH
    page_table = jax.random.permutation(next(keys), n_pool)[:n_used].reshape(DEC_BATCH, n_pages).astype(jnp.int32)
    return {
        'x_prompt': nrm((BATCH, SEQ, D_MODEL), 1.0),
        'x_sample': nrm((DEC_BATCH, DEC_SEQ, D_MODEL), 1.0),
        'cache_k': nrm((DEPTH, n_pool, PAGE_SIZE, ATT_HEADS, HEAD_DIM), 1.0),
        'cache_v': nrm((DEPTH, n_pool, PAGE_SIZE, ATT_HEADS, HEAD_DIM), 1.0),
        'state_pool': nrm((DEPTH, DEC_BATCH, POOL_HIST, G), 1.0),
        'state_conv': nrm((DEPTH, DEC_BATCH, CONV_HIST, G), 0.5),
        'page_table': page_table,
        'ffn1_norm': 1.0 + nrm((DEPTH, D_MODEL), 0.05),
        'ffn1_w_gate': nrm((DEPTH, D_MODEL, FFN_DIM), D_MODEL ** -0.5),
        'ffn1_w_up': nrm((DEPTH, D_MODEL, FFN_DIM), D_MODEL ** -0.5),
        'ffn1_w_down': nrm((DEPTH, FFN_DIM, D_MODEL), FFN_DIM ** -0.5),
        'mix_norm': 1.0 + nrm((DEPTH, D_MODEL), 0.05),
        'w_in': nrm((DEPTH, D_MODEL, IN_COLS), D_MODEL ** -0.5),
        'q_norm': 1.0 + nrm((DEPTH, HEAD_DIM), 0.05),
        'k_norm': 1.0 + nrm((DEPTH, HEAD_DIM), 0.05),
        'pool_w': nrm((DEPTH, len(POOL_WINDOWS), POOL_GROUP, POOL_GROUP), POOL_GROUP ** -0.5),
        'pool_scale': 1.0 + nrm((DEPTH, G), 0.05),
        'conv_w': nrm((DEPTH, CONV_WIDTH, G), CONV_WIDTH ** -0.5),
        'conv_b': nrm((DEPTH, G), 0.02),
        'conv_ln_g': 1.0 + nrm((DEPTH, G), 0.05),
        'conv_ln_b': nrm((DEPTH, G), 0.02),
        'conv_pw': nrm((DEPTH, G, G), G ** -0.5),
        'gate_norm': 1.0 + nrm((DEPTH, G), 0.05),
        'spatial_w': nrm((DEPTH, GMLP_HEADS, GMLP_CHUNK, GMLP_CHUNK), 0.5 * GMLP_CHUNK ** -0.5),
        'spatial_b': 1.0 + nrm((DEPTH, GMLP_HEADS, GMLP_CHUNK), 0.1),
        'w_out': nrm((DEPTH, D_MODEL, D_MODEL), D_MODEL ** -0.5),
        'ffn2_norm': 1.0 + nrm((DEPTH, D_MODEL), 0.05),
        'ffn2_w_gate': nrm((DEPTH, D_MODEL, FFN_DIM), D_MODEL ** -0.5),
        'ffn2_w_up': nrm((DEPTH, D_MODEL, FFN_DIM), D_MODEL ** -0.5),
        'ffn2_w_down': nrm((DEPTH, FFN_DIM, D_MODEL), FFN_DIM ** -0.5),
    }


def reference(x_prompt, x_sample, cache_k, cache_v, state_pool, state_conv, page_table,
              ffn1_norm, ffn1_w_gate, ffn1_w_up, ffn1_w_down, mix_norm, w_in, q_norm, k_norm,
              pool_w, pool_scale, conv_w, conv_b, conv_ln_g, conv_ln_b, conv_pw,
              gate_norm, spatial_w, spatial_b, w_out, ffn2_norm, ffn2_w_gate, ffn2_w_up, ffn2_w_down):
    B = x_prompt.shape[0]
    DB = x_sample.shape[0]
    n_pages = PAST_LEN // PAGE_SIZE
    hp, hs = x_prompt, x_sample
    kp_l, vp_l, ks_l, vs_l, pp_l, ps_l, cp_l, cs_l, gs_l = [], [], [], [], [], [], [], [], []
    for l in range(DEPTH):
        lw = (ffn1_norm[l], ffn1_w_gate[l], ffn1_w_up[l], ffn1_w_down[l], mix_norm[l], w_in[l], q_norm[l], k_norm[l],
              pool_w[l], pool_scale[l], conv_w[l], conv_b[l], conv_ln_g[l], conv_ln_b[l], conv_pw[l],
              gate_norm[l], spatial_w[l], spatial_b[l], w_out[l], ffn2_norm[l], ffn2_w_gate[l], ffn2_w_up[l], ffn2_w_down[l])
        no_kv = jnp.zeros((B, 0, ATT_HEADS, HEAD_DIM), hp.dtype)
        hp, kp, vp, pp, cp, _ = hybrid_layer(hp, 0, no_kv, no_kv,
                                             jnp.zeros((B, 0, GROUP_WIDTH), hp.dtype),
                                             jnp.zeros((B, CONV_HIST, GROUP_WIDTH), hp.dtype), *lw)
        k_past = cache_k[l][page_table].reshape(DB, n_pages * PAGE_SIZE, ATT_HEADS, HEAD_DIM)
        v_past = cache_v[l][page_table].reshape(DB, n_pages * PAGE_SIZE, ATT_HEADS, HEAD_DIM)
        hs, ks, vs, ps, cs, gs = hybrid_layer(hs, PAST_LEN, k_past, v_past, state_pool[l], state_conv[l], *lw)
        kp_l.append(kp); vp_l.append(vp); ks_l.append(ks); vs_l.append(vs)
        pp_l.append(pp); ps_l.append(ps); cp_l.append(cp); cs_l.append(cs); gs_l.append(gs)
    y_prompt, y_sample = hp, hs
    k_prompt, v_prompt = jnp.stack(kp_l), jnp.stack(vp_l)
    k_sample, v_sample = jnp.stack(ks_l), jnp.stack(vs_l)
    pool_prompt, pool_sample = jnp.stack(pp_l), jnp.stack(ps_l)
    conv_prompt, conv_sample = jnp.stack(cp_l), jnp.stack(cs_l)
    gate_v_sample = jnp.stack(gs_l)
    return (y_prompt, y_sample, k_prompt, v_prompt, k_sample, v_sample, pool_prompt, pool_sample, conv_prompt, conv_sample, gate_v_sample)
```

```python
import functools

import jax
import jax.numpy as jnp
from jax import lax
from jax.experimental import pallas as pl
from jax.experimental.pallas import tpu as pltpu

F32 = jnp.float32
BF16 = jnp.bfloat16
NEG_INF = float("-inf")

GROUP = 256
HEAD_DIM = 64
N_HEADS = GROUP // HEAD_DIM
MOBA_BLOCK = 256
MOBA_TOP_K = 3
PAGE = 128
POOL_WINDOWS = (2, 4, 8, 16)
POOL_HIST = 15
CONV_WIDTH = 31
CONV_HIST = 30
CHUNK = 128
RMS_EPS = 1e-6
LN_EPS = 1e-5
VMEM_LIMIT = 48 * 1024 * 1024


def _cparams(sem):
    return pltpu.CompilerParams(dimension_semantics=sem, vmem_limit_bytes=VMEM_LIMIT)


def _rms(x, g):
    ms = jnp.mean(x * x, axis=-1, keepdims=True)
    return x * lax.rsqrt(ms + RMS_EPS) * g


def _dot(a, b):
    return jnp.dot(a, b, preferred_element_type=F32)


def _dot_nt(a, b):
    return lax.dot_general(a, b, (((1,), (1,)), ((), ())), preferred_element_type=F32)


def _split(a):
    hi = a.astype(BF16)
    lo = (a - hi.astype(F32)).astype(BF16)
    return hi, lo


def _dot3_nt(a, b):
    ah, al = _split(a)
    bh, bl = _split(b)
    return _dot_nt(ah, bh) + _dot_nt(ah, bl) + _dot_nt(al, bh)


def _lane_group(shape, width):
    return lax.broadcasted_iota(jnp.int32, shape, len(shape) - 1) // width


def _ffn_kernel(x_ref, g_ref, wg_ref, wu_ref, wd_ref, o_ref, h_scr, acc_scr, *, n_f):
    f = pl.program_id(1)

    @pl.when(f == 0)
    def _():
        h_scr[...] = _rms(x_ref[...], g_ref[...]).astype(BF16)
        acc_scr[...] = jnp.zeros_like(acc_scr)

    h = h_scr[...]
    gate = _dot(h, wg_ref[...])
    up = _dot(h, wu_ref[...])
    act = (gate * jax.nn.sigmoid(gate) * up).astype(BF16)
    acc_scr[...] += _dot(act, wd_ref[...])

    @pl.when(f == n_f - 1)
    def _():
        o_ref[...] = x_ref[...] + 0.5 * acc_scr[...]


def _ffn(x, norm, wg, wu, wd, tm, tf):
    n, d = x.shape
    f = wg.shape[1]
    return pl.pallas_call(
        functools.partial(_ffn_kernel, n_f=f // tf),
        grid=(n // tm, f // tf),
        in_specs=[pl.BlockSpec((tm, d), lambda i, j: (i, 0)),
                  pl.BlockSpec((1, d), lambda i, j: (0, 0)),
                  pl.BlockSpec((d, tf), lambda i, j: (0, j)),
                  pl.BlockSpec((d, tf), lambda i, j: (0, j)),
                  pl.BlockSpec((tf, d), lambda i, j: (j, 0))],
        out_specs=pl.BlockSpec((tm, d), lambda i, j: (i, 0)),
        out_shape=jax.ShapeDtypeStruct((n, d), F32),
        scratch_shapes=[pltpu.VMEM((tm, d), BF16), pltpu.VMEM((tm, d), F32)],
        compiler_params=_cparams(("parallel", "arbitrary")),
        name="ffn",
    )(x, norm, wg, wu, wd)


def _proj_in_kernel(x_ref, g_ref, w_ref, qn_ref, kn_ref, gn_ref, seg_ref,
                    q_o, k_o, v_o, p_o, a_o, u_o, vg_o):
    G = GROUP
    h = _rms(x_ref[...], g_ref[...]).astype(BF16)
    z = _dot(h, w_ref[...])
    seg = seg_ref[...]

    def head_rms(t, gain):
        sq = t * t
        hi, lo = _split(sq)
        ms = _dot(hi, seg) + _dot(lo, seg)
        return t * lax.rsqrt(ms + RMS_EPS) * gain

    q_o[...] = head_rms(z[:, 0:G], qn_ref[...])
    k_o[...] = head_rms(z[:, G:2 * G], kn_ref[...])
    v_o[...] = z[:, 2 * G:3 * G]
    p_o[...] = z[:, 3 * G:4 * G]
    a_o[...] = z[:, 4 * G:5 * G] * jax.nn.sigmoid(z[:, 5 * G:6 * G])
    d = jax.nn.gelu(z[:, 6 * G:8 * G])
    u_o[...] = d[:, 0:G]
    vg_o[...] = _rms(d[:, G:2 * G], gn_ref[...])


def _proj_in(x, norm, w_in, qn, kn, gn, seg, tm):
    n, d = x.shape
    cols = w_in.shape[1]
    row = lambda i: (i, 0)
    fixed = lambda i: (0, 0)
    out = jax.ShapeDtypeStruct((n, GROUP), F32)
    return pl.pallas_call(
        _proj_in_kernel,
        grid=(n // tm,),
        in_specs=[pl.BlockSpec((tm, d), row),
                  pl.BlockSpec((1, d), fixed),
                  pl.BlockSpec((d, cols), fixed),
                  pl.BlockSpec((1, GROUP), fixed),
                  pl.BlockSpec((1, GROUP), fixed),
                  pl.BlockSpec((1, GROUP), fixed),
                  pl.BlockSpec((GROUP, GROUP), fixed)],
        out_specs=[pl.BlockSpec((tm, GROUP), row)] * 7,
        out_shape=[out] * 7,
        compiler_params=_cparams(("parallel",)),
        name="proj_in",
    )(x, norm, w_in, qn, kn, gn, seg)


def _proj_out_kernel(x_ref, a_ref, p_ref, c_ref, g_ref, w_ref, o_ref):
    G = GROUP
    acc = x_ref[...]
    acc += _dot(a_ref[...].astype(BF16), w_ref[0:G, :])
    acc += _dot(p_ref[...].astype(BF16), w_ref[G:2 * G, :])
    acc += _dot(c_ref[...].astype(BF16), w_ref[2 * G:3 * G, :])
    acc += _dot(g_ref[...].astype(BF16), w_ref[3 * G:4 * G, :])
    o_ref[...] = acc


def _proj_out(x, att, pool, conv, gmlp, w_out, tm):
    n, d = x.shape
    row = lambda i: (i, 0)
    return pl.pallas_call(
        _proj_out_kernel,
        grid=(n // tm,),
        in_specs=[pl.BlockSpec((tm, d), row)] + [pl.BlockSpec((tm, GROUP), row)] * 4
                 + [pl.BlockSpec(w_out.shape, lambda i: (0, 0))],
        out_specs=pl.BlockSpec((tm, d), row),
        out_shape=jax.ShapeDtypeStruct((n, d), F32),
        compiler_params=_cparams(("parallel",)),
        name="proj_out",
    )(x, att, pool, conv, gmlp, w_out)


def _select_bias(gate, n_valid, n_blk):
    cols = lax.broadcasted_iota(jnp.int32, gate.shape, 1)
    gm = jnp.where(cols < n_valid, gate, NEG_INF)
    out = []
    for n in range(n_blk):
        gn = gm[:, n:n + 1]
        beats = (gm > gn) | ((gm == gn) & (cols < n))
        rank = jnp.sum(jnp.where(beats, 1.0, 0.0), axis=-1, keepdims=True)
        out.append(jnp.where(rank < MOBA_TOP_K, 0.0, NEG_INF))
    return out


def _attn_prompt_kernel(q_ref, k_ref, v_ref, o_ref, kb_scr, vb_scr, km_scr, bias_scr, *, n_blk):
    i = pl.program_id(1)
    B = MOBA_BLOCK

    @pl.when(i == 0)
    def _():
        km_scr[...] = jnp.zeros_like(km_scr)
        for n in range(n_blk):
            kn = k_ref[0, n * B:(n + 1) * B, :]
            km_scr[n:n + 1, :] = jnp.mean(kn, axis=0, keepdims=True)
            kb_scr[n * B:(n + 1) * B, :] = kn.astype(BF16)
            vb_scr[n * B:(n + 1) * B, :] = v_ref[0, n * B:(n + 1) * B, :].astype(BF16)

    q = q_ref[0]
    kmean = km_scr[...]
    lane_head = _lane_group((B, GROUP), HEAD_DIM)
    rel = (lax.broadcasted_iota(jnp.int32, (B, B), 0) - lax.broadcasted_iota(jnp.int32, (B, B), 1)).astype(F32)
    scale = HEAD_DIM ** -0.5
    i0 = pl.multiple_of(i * B, B)
    k_own = kb_scr[pl.ds(i0, B), :]
    v_own = vb_scr[pl.ds(i0, B), :]
    out = jnp.zeros((B, GROUP), F32)

    for h in range(N_HEADS):
        slope = 2.0 ** (-8.0 * (h + 1) / N_HEADS)
        in_head = lane_head == h
        qh = jnp.where(in_head, q, 0.0)
        qh_b = qh.astype(BF16)
        gate = _dot3_nt(qh, kmean)
        biases = _select_bias(gate, i, n_blk)
        for n in range(n_blk):
            bias_scr[n] = jnp.broadcast_to(biases[n], (B, 128))

        s = _dot_nt(qh_b, k_own) * scale - slope * rel
        s = jnp.where(rel >= 0.0, s, NEG_INF)
        m = jnp.max(s, axis=-1, keepdims=True)
        p = jnp.exp(s - m)
        l = jnp.sum(p, axis=-1, keepdims=True)
        acc = _dot(p.astype(BF16), v_own)

        def body(n, carry):
            m, l, acc = carry
            n0 = pl.multiple_of(n * B, B)
            kn = kb_scr[pl.ds(n0, B), :]
            vn = vb_scr[pl.ds(n0, B), :]
            dist = ((i - n) * B).astype(F32)
            bias = bias_scr[n][:, 0:1]
            s = _dot_nt(qh_b, kn) * scale - slope * (rel + dist) + bias
            m_new = jnp.maximum(m, jnp.max(s, axis=-1, keepdims=True))
            alpha = jnp.exp(m - m_new)
            p = jnp.exp(s - m_new)
            l = alpha * l + jnp.sum(p, axis=-1, keepdims=True)
            acc = alpha * acc + _dot(p.astype(BF16), vn)
            return m_new, l, acc

        m, l, acc = lax.fori_loop(0, i, body, (m, l, acc))
        out = jnp.where(in_head, acc / l, out)

    o_ref[0] = out


def _attn_prompt(q, k, v):
    b, t, g = q.shape
    n_blk = t // MOBA_BLOCK
    return pl.pallas_call(
        functools.partial(_attn_prompt_kernel, n_blk=n_blk),
        grid=(b, n_blk),
        in_specs=[pl.BlockSpec((1, MOBA_BLOCK, g), lambda bi, i: (bi, i, 0)),
                  pl.BlockSpec((1, t, g), lambda bi, i: (bi, 0, 0)),
                  pl.BlockSpec((1, t, g), lambda bi, i: (bi, 0, 0))],
        out_specs=pl.BlockSpec((1, MOBA_BLOCK, g), lambda bi, i: (bi, i, 0)),
        out_shape=jax.ShapeDtypeStruct((b, t, g), F32),
        scratch_shapes=[pltpu.VMEM((t, g), BF16), pltpu.VMEM((t, g), BF16),
                        pltpu.VMEM((128, g), F32), pltpu.VMEM((n_blk, MOBA_BLOCK, 128), F32)],
        compiler_params=_cparams(("parallel", "arbitrary")),
        name="attn_prompt",
    )(q, k, v)


def _attn_sample_kernel(pt_ref, q_ref, kn_ref, vn_ref, *rest, n_pages, t_new):
    k_pages = rest[:n_pages]
    v_pages = rest[n_pages:2 * n_pages]
    o_ref = rest[2 * n_pages]
    T = t_new
    R = N_HEADS * T
    n_blk = n_pages * PAGE // MOBA_BLOCK
    past = n_pages * PAGE
    scale = HEAD_DIM ** -0.5

    q = q_ref[0]
    q_rows = jnp.concatenate([q] * N_HEADS, axis=0)
    row_head = lax.broadcasted_iota(jnp.int32, (R, GROUP), 0) // T
    qbd = jnp.where(_lane_group((R, GROUP), HEAD_DIM) == row_head, q_rows, 0.0)
    qbd_b = qbd.astype(BF16)

    sums = [jnp.sum(kp[0], axis=0, keepdims=True) for kp in k_pages]
    per_blk = MOBA_BLOCK // PAGE
    means = [sum(sums[n * per_blk:(n + 1) * per_blk]) * (1.0 / MOBA_BLOCK) for n in range(n_blk)]
    km_rows = lax.broadcasted_iota(jnp.int32, (128, GROUP), 0)
    kmean = jnp.zeros((128, GROUP), F32)
    for n in range(n_blk):
        kmean = jnp.where(km_rows == n, means[n], kmean)
    gate = _dot3_nt(qbd, kmean)
    biases = _select_bias(gate, n_blk, n_blk)

    r1 = lax.broadcasted_iota(jnp.int32, (R, 1), 0)
    slope = jnp.zeros((R, 1), F32)
    for h in range(N_HEADS):
        slope = jnp.where(r1 // T == h, 2.0 ** (-8.0 * (h + 1) / N_HEADS), slope)
    t_row = r1 % T
    pos_q = (past + t_row).astype(F32)
    col = lax.broadcasted_iota(jnp.int32, (R, PAGE), 1)

    scores = []
    for pg in range(n_pages):
        kp = k_pages[pg][0].astype(BF16)
        key_pos = (col + pg * PAGE).astype(F32)
        s = _dot_nt(qbd_b, kp) * scale - slope * (pos_q - key_pos) + biases[pg // per_blk]
        scores.append(s)
    pad = jnp.zeros((PAGE - T, GROUP), F32)
    k_new = jnp.concatenate([kn_ref[0], pad], axis=0).astype(BF16)
    v_new = jnp.concatenate([vn_ref[0], pad], axis=0).astype(BF16)
    s_new = _dot_nt(qbd_b, k_new) * scale - slope * (t_row - col).astype(F32)
    s_new = jnp.where(col <= t_row, s_new, NEG_INF)
    scores.append(s_new)

    m = functools.reduce(jnp.maximum, [jnp.max(s, axis=-1, keepdims=True) for s in scores])
    l = jnp.zeros((R, 1), F32)
    acc = jnp.zeros((R, GROUP), F32)
    for pg in range(n_pages + 1):
        p = jnp.exp(scores[pg] - m)
        l = l + jnp.sum(p, axis=-1, keepdims=True)
        vp = v_new if pg == n_pages else v_pages[pg][0].astype(BF16)
        acc = acc + _dot(p.astype(BF16), vp)
    acc = acc / l

    lane_head = _lane_group((T, GROUP), HEAD_DIM)
    out = jnp.zeros((T, GROUP), F32)
    for h in range(N_HEADS):
        out = jnp.where(lane_head == h, acc[h * T:(h + 1) * T, :], out)
    o_ref[0] = out


def _attn_sample(q, k_new, v_new, cache_k, cache_v, page_table, page_base):
    db, t, g = q.shape
    n_pages = page_table.shape[1]
    tok = pl.BlockSpec((1, t, g), lambda b, pt: (b, 0, 0))

    def page_spec(pg):
        return pl.BlockSpec((1, PAGE, g), lambda b, pt: (page_base + pt[b, pg], 0, 0))

    pages = [page_spec(pg) for pg in range(n_pages)]
    return pl.pallas_call(
        functools.partial(_attn_sample_kernel, n_pages=n_pages, t_new=t),
        grid_spec=pltpu.PrefetchScalarGridSpec(
            num_scalar_prefetch=1,
            grid=(db,),
            in_specs=[tok, tok, tok] + pages + pages,
            out_specs=pl.BlockSpec((1, t, g), lambda b, pt: (b, 0, 0)),
        ),
        out_shape=jax.ShapeDtypeStruct((db, t, g), F32),
        compiler_params=_cparams(("arbitrary",)),
        name="attn_sample",
    )(page_table, q, k_new, v_new, *([cache_k] * n_pages), *([cache_v] * n_pages))


def _layernorm_silu(y, g, b):
    mu = jnp.mean(y, axis=-1, keepdims=True)
    yc = y - mu
    var = jnp.mean(yc * yc, axis=-1, keepdims=True)
    yn = yc * lax.rsqrt(var + LN_EPS) * g + b
    return yn * jax.nn.sigmoid(yn)


def _pool_window_select(sums, shape):
    grp = _lane_group(shape, GROUP // len(POOL_WINDOWS))
    sel = sums[POOL_WINDOWS[-1]]
    for gi in range(len(POOL_WINDOWS) - 2, -1, -1):
        sel = jnp.where(grp == gi, sums[POOL_WINDOWS[gi]], sel)
    win = jnp.left_shift(jnp.int32(POOL_WINDOWS[0]), grp)
    return sel, win


def _mix_prompt_kernel(pp_ref, p_ref, ap_ref, a_ref, u_ref, vg_ref, poolw_ref, pscale_ref,
                       convw_ref, convb_ref, lng_ref, lnb_ref, convpw_ref, sw_ref, sb_ref,
                       pool_o, conv_o, gmlp_o, p_scr, a_scr):
    c = pl.program_id(1)
    C = CHUNK
    has_prev = c > 0
    p_cur = p_ref[0]
    p_scr[0:C, :] = jnp.where(has_prev, pp_ref[0], 0.0)
    p_scr[C:2 * C, :] = p_cur
    a_scr[0:C, :] = jnp.where(has_prev, ap_ref[0], 0.0)
    a_scr[C:2 * C, :] = a_ref[0]

    acc = jnp.zeros((C, GROUP), F32)
    sums = {}
    for i in range(POOL_WINDOWS[-1]):
        acc = acc + p_scr[C - i:2 * C - i, :]
        if i + 1 in POOL_WINDOWS:
            sums[i + 1] = acc
    sel, win = _pool_window_select(sums, (C, GROUP))
    t_glob = c * C + lax.broadcasted_iota(jnp.int32, (C, GROUP), 0)
    count = jnp.minimum(t_glob + 1, win).astype(F32)
    m = sel / count - p_cur
    pool_o[0] = _dot(m.astype(BF16), poolw_ref[...]) * pscale_ref[...]

    y = jnp.zeros((C, GROUP), F32)
    base = C - CONV_HIST
    for j in range(CONV_WIDTH):
        y = y + a_scr[base + j:base + j + C, :] * convw_ref[j:j + 1, :]
    y = _layernorm_silu(y + convb_ref[...], lng_ref[...], lnb_ref[...])
    conv_o[0] = _dot(y.astype(BF16), convpw_ref[...])

    vgb = vg_ref[0].astype(BF16)
    tri = lax.broadcasted_iota(jnp.int32, (C, C), 0) >= lax.broadcasted_iota(jnp.int32, (C, C), 1)
    grp = _lane_group((C, GROUP), HEAD_DIM)
    s = sb_ref[...]
    for g in range(N_HEADS):
        wc = jnp.where(tri, sw_ref[g], 0.0).astype(BF16)
        s = s + jnp.where(grp == g, _dot(wc, vgb), 0.0)
    gmlp_o[0] = u_ref[0] * s


def _mix_prompt(p, a, u, vg, poolw, pscale, convw, convb, lng, lnb, convpw, sw, sb):
    b, t, g = p.shape
    cur = pl.BlockSpec((1, CHUNK, g), lambda bi, c: (bi, c, 0))
    prev = pl.BlockSpec((1, CHUNK, g), lambda bi, c: (bi, jnp.maximum(c - 1, 0), 0))

    def full(arr):
        nd = arr.ndim
        return pl.BlockSpec(arr.shape, lambda bi, c: (0,) * nd)

    out = jax.ShapeDtypeStruct((b, t, g), F32)
    weights = (poolw, pscale, convw, convb, lng, lnb, convpw, sw, sb)
    return pl.pallas_call(
        _mix_prompt_kernel,
        grid=(b, t // CHUNK),
        in_specs=[prev, cur, prev, cur, cur, cur] + [full(w) for w in weights],
        out_specs=[cur] * 3,
        out_shape=[out] * 3,
        scratch_shapes=[pltpu.VMEM((2 * CHUNK, g), F32), pltpu.VMEM((2 * CHUNK, g), F32)],
        compiler_params=_cparams(("parallel", "arbitrary")),
        name="mix_prompt",
    )(p, p, a, a, u, vg, *weights)


def _mix_sample_kernel(hp_ref, p_ref, hc_ref, a_ref, u_ref, vg_ref, poolw_ref, pscale_ref,
                       convw_ref, convb_ref, lng_ref, lnb_ref, convpw_ref, sw_ref, sb_ref,
                       pool_o, conv_o, gmlp_o, *, t_new):
    T = t_new
    nb = p_ref.shape[1]
    shape = (nb, GROUP)

    def pool_row(k):
        return hp_ref[k] if k < POOL_HIST else p_ref[k - POOL_HIST]

    def conv_row(k):
        return hc_ref[k] if k < CONV_HIST else a_ref[k - CONV_HIST]

    for t in range(T):
        acc = jnp.zeros(shape, F32)
        sums = {}
        for i in range(POOL_WINDOWS[-1]):
            k = POOL_HIST + t - i
            if k >= 0:
                acc = acc + pool_row(k)
            if i + 1 in POOL_WINDOWS:
                sums[i + 1] = acc
        sel, win = _pool_window_select(sums, shape)
        count = jnp.minimum(POOL_HIST + t + 1, win).astype(F32)
        m = sel / count - p_ref[t]
        pool_o[t] = _dot(m.astype(BF16), poolw_ref[...]) * pscale_ref[...]

        y = jnp.zeros(shape, F32)
        for j in range(CONV_WIDTH):
            y = y + conv_row(t + j) * convw_ref[j:j + 1, :]
        y = _layernorm_silu(y + convb_ref[...], lng_ref[...], lnb_ref[...])
        conv_o[t] = _dot(y.astype(BF16), convpw_ref[...])

        s = jnp.broadcast_to(sb_ref[t:t + 1, :], shape)
        for j in range(t + 1):
            s = s + sw_ref[t, j:j + 1, :] * vg_ref[j]
        gmlp_o[t] = u_ref[t] * s


def _mix_sample(hp, p, hc, a, u, vg, poolw, pscale, convw, convb, lng, lnb, convpw, sw, sb):
    t, db, g = p.shape

    def full(arr):
        nd = arr.ndim
        return pl.BlockSpec(arr.shape, lambda i: (0,) * nd)

    args = (hp, p, hc, a, u, vg, poolw, pscale, convw, convb, lng, lnb, convpw, sw, sb)
    out = jax.ShapeDtypeStruct((t, db, g), F32)
    return pl.pallas_call(
        functools.partial(_mix_sample_kernel, t_new=t),
        grid=(1,),
        in_specs=[full(x) for x in args],
        out_specs=[full(out)] * 3,
        out_shape=[out] * 3,
        compiler_params=_cparams(("arbitrary",)),
        name="mix_sample",
    )(*args)


def _row_tile(n, cap):
    t = min(n, cap)
    while n % t:
        t //= 2
    return t


def _ffn_tile(f):
    for tf in (512, 256, 128):
        if f % tf == 0:
            return tf
    return f


def _layer_weights(l, ffn1_norm, ffn1_w_gate, ffn1_w_up, ffn1_w_down, mix_norm, w_in, q_norm, k_norm,
                   pool_w, pool_scale, conv_w, conv_b, conv_ln_g, conv_ln_b, conv_pw,
                   gate_norm, spatial_w, spatial_b, w_out, ffn2_norm, ffn2_w_gate, ffn2_w_up, ffn2_w_down):
    row = lambda v: v[l].reshape(1, -1)
    head_id = jnp.arange(GROUP) // HEAD_DIM
    seg = jnp.where(head_id[:, None] == head_id[None, :], 1.0 / HEAD_DIM, 0.0).astype(BF16)
    n_win = len(POOL_WINDOWS)
    pg = GROUP // n_win
    pool_bd = jnp.zeros((GROUP, GROUP), F32)
    for gi in range(n_win):
        pool_bd = pool_bd.at[gi * pg:(gi + 1) * pg, gi * pg:(gi + 1) * pg].set(pool_w[l, gi])
    sb_lanes = jnp.repeat(spatial_b[l].T, HEAD_DIM, axis=1)
    return dict(
        ffn1=(row(ffn1_norm), ffn1_w_gate[l].astype(BF16), ffn1_w_up[l].astype(BF16), ffn1_w_down[l].astype(BF16)),
        ffn2=(row(ffn2_norm), ffn2_w_gate[l].astype(BF16), ffn2_w_up[l].astype(BF16), ffn2_w_down[l].astype(BF16)),
        proj_in=(row(mix_norm), w_in[l].astype(BF16), jnp.tile(q_norm[l], N_HEADS).reshape(1, -1),
                 jnp.tile(k_norm[l], N_HEADS).reshape(1, -1), row(gate_norm), seg),
        w_out=w_out[l].astype(BF16),
        mix=(pool_bd.astype(BF16), row(pool_scale), conv_w[l], row(conv_b), row(conv_ln_g), row(conv_ln_b),
             conv_pw[l].astype(BF16)),
        spatial_w=spatial_w[l],
        spatial_b_lanes=sb_lanes,
    )


def _token_stage_in(x, w):
    n = x.shape[0]
    x1 = _ffn(x, *w["ffn1"], tm=_row_tile(n, 1024), tf=_ffn_tile(w["ffn1"][1].shape[1]))
    return x1, _proj_in(x1, *w["proj_in"], tm=_row_tile(n, 512))


def _token_stage_out(x1, att, pool, conv, gmlp, w):
    n = x1.shape[0]
    x2 = _proj_out(x1, att, pool, conv, gmlp, w["w_out"], tm=_row_tile(n, 1024))
    return _ffn(x2, *w["ffn2"], tm=_row_tile(n, 1024), tf=_ffn_tile(w["ffn2"][1].shape[1]))


def _prompt_layer(x, w):
    b, t, d = x.shape
    x1, (q, k, v, p, a, u, vg) = _token_stage_in(x.reshape(b * t, d), w)
    r3 = lambda arr: arr.reshape(b, t, GROUP)
    q, k, v, p, a, u, vg = map(r3, (q, k, v, p, a, u, vg))
    att = _attn_prompt(q, k, v)
    pool, conv, gmlp = _mix_prompt(p, a, u, vg, *w["mix"], w["spatial_w"], w["spatial_b_lanes"])
    flat = lambda arr: arr.reshape(b * t, GROUP)
    y = _token_stage_out(x1, flat(att), flat(pool), flat(conv), flat(gmlp), w)
    return y.reshape(b, t, d), k, v, p[:, -POOL_HIST:], a[:, -CONV_HIST:]


def _sample_layer(x, cache_k, cache_v, page_table, page_base, hist_pool, hist_conv, w):
    db, t, d = x.shape
    x1, (q, k, v, p, a, u, vg) = _token_stage_in(x.reshape(db * t, d), w)
    r3 = lambda arr: arr.reshape(db, t, GROUP)
    q, k, v, p, a, u, vg = map(r3, (q, k, v, p, a, u, vg))
    att = _attn_sample(q, k, v, cache_k, cache_v, page_table, page_base)
    tm = lambda arr: jnp.swapaxes(arr, 0, 1)
    sw_lanes = jnp.repeat(jnp.transpose(w["spatial_w"][:, :t, :t], (1, 2, 0)), HEAD_DIM, axis=2)
    pool, conv, gmlp = _mix_sample(tm(hist_pool), tm(p), tm(hist_conv), tm(a), tm(u), tm(vg),
                                   *w["mix"], sw_lanes, w["spatial_b_lanes"][:t])
    flat = lambda arr: tm(arr).reshape(db * t, GROUP)
    y = _token_stage_out(x1, att.reshape(db * t, GROUP), flat(pool), flat(conv), flat(gmlp), w)
    pool_state = jnp.concatenate([hist_pool, p], axis=1)[:, -POOL_HIST:]
    conv_state = jnp.concatenate([hist_conv, a], axis=1)[:, -CONV_HIST:]
    return y.reshape(db, t, d), k, v, pool_state, conv_state, vg


def kernel(x_prompt, x_sample, cache_k, cache_v, state_pool, state_conv, page_table, ffn1_norm, ffn1_w_gate, ffn1_w_up, ffn1_w_down, mix_norm, w_in, q_norm, k_norm, pool_w, pool_scale, conv_w, conv_b, conv_ln_g, conv_ln_b, conv_pw, gate_norm, spatial_w, spatial_b, w_out, ffn2_norm, ffn2_w_gate, ffn2_w_up, ffn2_w_down):
    depth, n_pool = cache_k.shape[0], cache_k.shape[1]
    b, t = x_prompt.shape[0], x_prompt.shape[1]
    db, ts = x_sample.shape[0], x_sample.shape[1]
    ck = cache_k.reshape(depth * n_pool, PAGE, GROUP)
    cv = cache_v.reshape(depth * n_pool, PAGE, GROUP)
    hp, hs = x_prompt, x_sample
    outs = [[] for _ in range(9)]
    for l in range(depth):
        w = _layer_weights(l, ffn1_norm, ffn1_w_gate, ffn1_w_up, ffn1_w_down, mix_norm, w_in, q_norm, k_norm,
                           pool_w, pool_scale, conv_w, conv_b, conv_ln_g, conv_ln_b, conv_pw,
                           gate_norm, spatial_w, spatial_b, w_out, ffn2_norm, ffn2_w_gate, ffn2_w_up, ffn2_w_down)
        hp, kp, vp, pp, cp = _prompt_layer(hp, w)
        hs, ks, vs, ps, cs, gs = _sample_layer(hs, ck, cv, page_table, l * n_pool, state_pool[l], state_conv[l], w)
        heads = lambda arr, n, s: arr.reshape(n, s, N_HEADS, HEAD_DIM)
        for lst, val in zip(outs, (heads(kp, b, t), heads(vp, b, t), heads(ks, db, ts), heads(vs, db, ts),
                                   pp, ps, cp, cs, gs)):
            lst.append(val)
    return (hp, hs) + tuple(jnp.stack(lst) for lst in outs)
```

```python
import functools

import jax
import jax.numpy as jnp
from jax import lax
from jax.experimental import pallas as pl
from jax.experimental.pallas import tpu as pltpu

F32 = jnp.float32
BF16 = jnp.bfloat16
NEG_INF = float("-inf")

GROUP = 256
HEAD_DIM = 64
N_HEADS = GROUP // HEAD_DIM
MOBA_BLOCK = 256
MOBA_TOP_K = 3
PAGE = 128
POOL_WINDOWS = (2, 4, 8, 16)
POOL_HIST = 15
CONV_WIDTH = 31
CONV_HIST = 30
CHUNK = 128
RMS_EPS = 1e-6
LN_EPS = 1e-5
VMEM_LIMIT = 48 * 1024 * 1024


def _cparams(sem):
    return pltpu.CompilerParams(dimension_semantics=sem, vmem_limit_bytes=VMEM_LIMIT)


def _rms(x, g):
    ms = jnp.mean(x * x, axis=-1, keepdims=True)
    return x * lax.rsqrt(ms + RMS_EPS) * g


def _dot(a, b):
    return jnp.dot(a, b, preferred_element_type=F32)


def _dot_nt(a, b):
    return lax.dot_general(a, b, (((1,), (1,)), ((), ())), preferred_element_type=F32)


def _split(a):
    hi = a.astype(BF16)
    lo = (a - hi.astype(F32)).astype(BF16)
    return hi, lo


def _dot3_nt(a, b):
    ah, al = _split(a)
    bh, bl = _split(b)
    return _dot_nt(ah, bh) + _dot_nt(ah, bl) + _dot_nt(al, bh)


def _lane_group(shape, width):
    return lax.broadcasted_iota(jnp.int32, shape, len(shape) - 1) // width


def _ffn_kernel(x_ref, g_ref, wg_ref, wu_ref, wd_ref, o_ref, h_scr, acc_scr, *, n_f):
    f = pl.program_id(1)

    @pl.when(f == 0)
    def _():
        h_scr[...] = _rms(x_ref[...], g_ref[...]).astype(BF16)
        acc_scr[...] = jnp.zeros_like(acc_scr)

    h = h_scr[...]
    gate = _dot(h, wg_ref[...])
    up = _dot(h, wu_ref[...])
    act = (gate * jax.nn.sigmoid(gate) * up).astype(BF16)
    acc_scr[...] += _dot(act, wd_ref[...])

    @pl.when(f == n_f - 1)
    def _():
        o_ref[...] = x_ref[...] + 0.5 * acc_scr[...]


def _ffn(x, norm, wg, wu, wd, tm, tf):
    n, d = x.shape
    f = wg.shape[1]
    return pl.pallas_call(
        functools.partial(_ffn_kernel, n_f=f // tf),
        grid=(n // tm, f // tf),
        in_specs=[pl.BlockSpec((tm, d), lambda i, j: (i, 0)),
                  pl.BlockSpec((1, d), lambda i, j: (0, 0)),
                  pl.BlockSpec((d, tf), lambda i, j: (0, j)),
                  pl.BlockSpec((d, tf), lambda i, j: (0, j)),
                  pl.BlockSpec((tf, d), lambda i, j: (j, 0))],
        out_specs=pl.BlockSpec((tm, d), lambda i, j: (i, 0)),
        out_shape=jax.ShapeDtypeStruct((n, d), F32),
        scratch_shapes=[pltpu.VMEM((tm, d), BF16), pltpu.VMEM((tm, d), F32)],
        compiler_params=_cparams(("parallel", "arbitrary")),
        name="ffn",
    )(x, norm, wg, wu, wd)


def _proj_in_kernel(x_ref, g_ref, w_ref, qn_ref, kn_ref, gn_ref, seg_ref,
                    q_o, k_o, v_o, p_o, a_o, u_o, vg_o):
    G = GROUP
    h = _rms(x_ref[...], g_ref[...]).astype(BF16)
    z = _dot(h, w_ref[...])
    seg = seg_ref[...]

    def head_rms(t, gain):
        sq = t * t
        hi, lo = _split(sq)
        ms = _dot(hi, seg) + _dot(lo, seg)
        return t * lax.rsqrt(ms + RMS_EPS) * gain

    q_o[...] = head_rms(z[:, 0:G], qn_ref[...])
    k_o[...] = head_rms(z[:, G:2 * G], kn_ref[...])
    v_o[...] = z[:, 2 * G:3 * G]
    p_o[...] = z[:, 3 * G:4 * G]
    a_o[...] = z[:, 4 * G:5 * G] * jax.nn.sigmoid(z[:, 5 * G:6 * G])
    d = jax.nn.gelu(z[:, 6 * G:8 * G])
    u_o[...] = d[:, 0:G]
    vg_o[...] = _rms(d[:, G:2 * G], gn_ref[...])


def _proj_in(x, norm, w_in, qn, kn, gn, seg, tm):
    n, d = x.shape
    cols = w_in.shape[1]
    row = lambda i: (i, 0)
    fixed = lambda i: (0, 0)
    out = jax.ShapeDtypeStruct((n, GROUP), F32)
    return pl.pallas_call(
        _proj_in_kernel,
        grid=(n // tm,),
        in_specs=[pl.BlockSpec((tm, d), row),
                  pl.BlockSpec((1, d), fixed),
                  pl.BlockSpec((d, cols), fixed),
                  pl.BlockSpec((1, GROUP), fixed),
                  pl.BlockSpec((1, GROUP), fixed),
                  pl.BlockSpec((1, GROUP), fixed),
                  pl.BlockSpec((GROUP, GROUP), fixed)],
        out_specs=[pl.BlockSpec((tm, GROUP), row)] * 7,
        out_shape=[out] * 7,
        compiler_params=_cparams(("parallel",)),
        name="proj_in",
    )(x, norm, w_in, qn, kn, gn, seg)


def _proj_out_kernel(x_ref, a_ref, p_ref, c_ref, g_ref, w_ref, o_ref):
    G = GROUP
    acc = x_ref[...]
    acc += _dot(a_ref[...].astype(BF16), w_ref[0:G, :])
    acc += _dot(p_ref[...].astype(BF16), w_ref[G:2 * G, :])
    acc += _dot(c_ref[...].astype(BF16), w_ref[2 * G:3 * G, :])
    acc += _dot(g_ref[...].astype(BF16), w_ref[3 * G:4 * G, :])
    o_ref[...] = acc


def _proj_out(x, att, pool, conv, gmlp, w_out, tm):
    n, d = x.shape
    row = lambda i: (i, 0)
    return pl.pallas_call(
        _proj_out_kernel,
        grid=(n // tm,),
        in_specs=[pl.BlockSpec((tm, d), row)] + [pl.BlockSpec((tm, GROUP), row)] * 4
                 + [pl.BlockSpec(w_out.shape, lambda i: (0, 0))],
        out_specs=pl.BlockSpec((tm, d), row),
        out_shape=jax.ShapeDtypeStruct((n, d), F32),
        compiler_params=_cparams(("parallel",)),
        name="proj_out",
    )(x, att, pool, conv, gmlp, w_out)


GATE_ROWS = 8
VT_ROWS = HEAD_DIM + 16


def _select_bias_rows(gate, n_valid):
    rows = lax.broadcasted_iota(jnp.int32, gate.shape, 0)
    gm = jnp.where(rows < n_valid, gate, NEG_INF)
    bias = jnp.full(gate.shape, NEG_INF, F32)
    for n in range(gate.shape[0]):
        gn = gm[n:n + 1, :]
        beats = (gm > gn) | ((gm == gn) & (rows < n))
        rank = jnp.sum(jnp.where(beats, 1.0, 0.0), axis=0, keepdims=True)
        bias = jnp.where((rows == n) & (rank < MOBA_TOP_K), 0.0, bias)
    return bias


def _attn_prompt_kernel(q_ref, k_ref, v_ref, o_ref, kb_scr, vt_scr, kbd_scr, bias_scr, *, n_blk):
    i = pl.program_id(1)
    B = MOBA_BLOCK
    lane_head = _lane_group((1, GROUP), HEAD_DIM)

    @pl.when(i == 0)
    def _():
        kbd_scr[...] = jnp.zeros_like(kbd_scr)
        ones = jnp.ones((VT_ROWS - HEAD_DIM, B), BF16)
        for n in range(n_blk):
            kn = k_ref[0, n * B:(n + 1) * B, :]
            kb_scr[n] = kn.astype(BF16)
            mean = jnp.mean(kn, axis=0, keepdims=True)
            vt = v_ref[0, n * B:(n + 1) * B, :].T
            for h in range(N_HEADS):
                r = h * GATE_ROWS + n
                kbd_scr[r:r + 1, :] = jnp.where(lane_head == h, mean, 0.0)
                vt_scr[n, h, 0:HEAD_DIM, :] = vt[h * HEAD_DIM:(h + 1) * HEAD_DIM, :].astype(BF16)
                vt_scr[n, h, HEAD_DIM:VT_ROWS, :] = ones

    q = q_ref[0]
    gate = _dot3_nt(kbd_scr[...], q)
    for h in range(N_HEADS):
        rows = slice(h * GATE_ROWS, (h + 1) * GATE_ROWS)
        bias_scr[rows, :] = _select_bias_rows(gate[rows, :], i)

    neg_rel = (lax.broadcasted_iota(jnp.int32, (B, B), 0) - lax.broadcasted_iota(jnp.int32, (B, B), 1)).astype(F32)
    slopes = [2.0 ** (-8.0 * (h + 1) / N_HEADS) for h in range(N_HEADS)]
    q_scaled = q * (HEAD_DIM ** -0.5)
    in_head = _lane_group((B, GROUP), HEAD_DIM)
    qh = [jnp.where(in_head == h, q_scaled, 0.0).astype(BF16) for h in range(N_HEADS)]

    k_own = kb_scr[i]
    ms, accs = [], []
    qk_own = [_dot_nt(k_own, qh[h]) for h in range(N_HEADS)]
    for h in range(N_HEADS):
        s = qk_own[h] + slopes[h] * neg_rel
        s = jnp.where(neg_rel <= 0.0, s, NEG_INF)
        m = jnp.max(s, axis=0, keepdims=True)
        p = jnp.exp(s - m)
        ms.append(m)
        accs.append(_dot(vt_scr[i, h], p.astype(BF16)))

    def body(n, carry):
        ms, accs = carry
        kn = kb_scr[n]
        dist = ((i - n) * B).astype(F32)
        new_ms, new_accs = [], []
        qk = [_dot_nt(kn, qh[h]) for h in range(N_HEADS)]
        for h in range(N_HEADS):
            bias = bias_scr[pl.ds(h * GATE_ROWS + n, 1), :] - slopes[h] * dist
            s = qk[h] + slopes[h] * neg_rel + bias
            m_new = jnp.maximum(ms[h], jnp.max(s, axis=0, keepdims=True))
            alpha = jnp.exp(ms[h] - m_new)
            p = jnp.exp(s - m_new)
            new_ms.append(m_new)
            new_accs.append(alpha * accs[h] + _dot(vt_scr[n, h], p.astype(BF16)))
        return tuple(new_ms), tuple(new_accs)

    ms, accs = lax.fori_loop(0, i, body, (tuple(ms), tuple(accs)))
    out_t = jnp.concatenate([a[0:HEAD_DIM, :] / a[HEAD_DIM:HEAD_DIM + 1, :] for a in accs], axis=0)
    o_ref[0] = out_t.T


def _attn_prompt(q, k, v):
    b, t, g = q.shape
    n_blk = t // MOBA_BLOCK
    assert t % MOBA_BLOCK == 0 and n_blk <= GATE_ROWS
    return pl.pallas_call(
        functools.partial(_attn_prompt_kernel, n_blk=n_blk),
        grid=(b, n_blk),
        in_specs=[pl.BlockSpec((1, MOBA_BLOCK, g), lambda bi, i: (bi, i, 0)),
                  pl.BlockSpec((1, t, g), lambda bi, i: (bi, 0, 0)),
                  pl.BlockSpec((1, t, g), lambda bi, i: (bi, 0, 0))],
        out_specs=pl.BlockSpec((1, MOBA_BLOCK, g), lambda bi, i: (bi, i, 0)),
        out_shape=jax.ShapeDtypeStruct((b, t, g), F32),
        scratch_shapes=[pltpu.VMEM((n_blk, MOBA_BLOCK, g), BF16),
                        pltpu.VMEM((n_blk, N_HEADS, VT_ROWS, MOBA_BLOCK), BF16),
                        pltpu.VMEM((N_HEADS * GATE_ROWS, g), F32),
                        pltpu.VMEM((N_HEADS * GATE_ROWS, MOBA_BLOCK), F32)],
        compiler_params=_cparams(("parallel", "arbitrary")),
        name="attn_prompt",
    )(q, k, v)


def _attn_sample_kernel(pt_ref, q_ref, kn_ref, vn_ref, *rest, n_pages, t_new):
    k_pages = rest[:n_pages]
    v_pages = rest[n_pages:2 * n_pages]
    o_ref = rest[2 * n_pages]
    T = t_new
    R = N_HEADS * T
    n_blk = n_pages * PAGE // MOBA_BLOCK
    past = n_pages * PAGE
    scale = HEAD_DIM ** -0.5

    q = q_ref[0] * scale
    q_rows = jnp.concatenate([q] * N_HEADS, axis=0)
    row_head = lax.broadcasted_iota(jnp.int32, (R, GROUP), 0) // T
    qbd = jnp.where(_lane_group((R, GROUP), HEAD_DIM) == row_head, q_rows, 0.0)
    q_hi, q_lo = _split(qbd)

    raw = []
    for pg in range(n_pages):
        k_hi, k_lo = _split(k_pages[pg][0])
        raw.append(_dot(q_hi, k_hi) + _dot(q_hi, k_lo) + _dot(q_lo, k_hi))
    per_blk = MOBA_BLOCK // PAGE
    gates = [jnp.sum(sum(raw[n * per_blk:(n + 1) * per_blk]), axis=-1, keepdims=True) for n in range(n_blk)]
    biases = []
    for n in range(n_blk):
        rank = jnp.zeros((R, 1), F32)
        for mth in range(n_blk):
            if mth != n:
                beats = (gates[mth] >= gates[n]) if mth < n else (gates[mth] > gates[n])
                rank = rank + jnp.where(beats, 1.0, 0.0)
        biases.append(jnp.where(rank < MOBA_TOP_K, 0.0, NEG_INF))

    r1 = lax.broadcasted_iota(jnp.int32, (R, 1), 0)
    slope = jnp.zeros((R, 1), F32)
    for h in range(N_HEADS):
        slope = jnp.where(r1 // T == h, 2.0 ** (-8.0 * (h + 1) / N_HEADS), slope)
    t_row = r1 % T
    pos_q = (past + t_row).astype(F32)
    col = lax.broadcasted_iota(jnp.int32, (R, PAGE), 1)

    scores = []
    for pg in range(n_pages):
        key_pos = (col + pg * PAGE).astype(F32)
        scores.append(raw[pg] - slope * (pos_q - key_pos) + biases[pg // per_blk])
    pad = jnp.zeros((PAGE - T, GROUP), F32)
    k_new = jnp.concatenate([kn_ref[0], pad], axis=0).astype(BF16)
    v_new = jnp.concatenate([vn_ref[0], pad], axis=0).astype(BF16)
    s_new = _dot_nt(q_hi, k_new) - slope * (t_row - col).astype(F32)
    s_new = jnp.where(col <= t_row, s_new, NEG_INF)
    scores.append(s_new)

    m = functools.reduce(jnp.maximum, [jnp.max(s, axis=-1, keepdims=True) for s in scores])
    l = jnp.zeros((R, 1), F32)
    acc = jnp.zeros((R, GROUP), F32)
    for pg in range(n_pages + 1):
        p = jnp.exp(scores[pg] - m)
        l = l + jnp.sum(p, axis=-1, keepdims=True)
        if pg == n_pages:
            acc = acc + _dot(p.astype(BF16), v_new)
        else:
            acc = acc + _dot_nt(p.astype(BF16), v_pages[pg][0].astype(BF16))
    acc = acc / l

    lane_head = _lane_group((T, GROUP), HEAD_DIM)
    out = jnp.zeros((T, GROUP), F32)
    for h in range(N_HEADS):
        out = jnp.where(lane_head == h, acc[h * T:(h + 1) * T, :], out)
    o_ref[0] = out


def _attn_sample(q, k_new, v_new, cache_k, cache_v, page_table, page_base):
    db, t, g = q.shape
    n_pages = page_table.shape[1]
    tok = pl.BlockSpec((1, t, g), lambda b, pt: (b, 0, 0))

    def page_spec(pg):
        return pl.BlockSpec((1, g, PAGE), lambda b, pt: (page_base + pt[b, pg], 0, 0))

    pages = [page_spec(pg) for pg in range(n_pages)]
    return pl.pallas_call(
        functools.partial(_attn_sample_kernel, n_pages=n_pages, t_new=t),
        grid_spec=pltpu.PrefetchScalarGridSpec(
            num_scalar_prefetch=1,
            grid=(db,),
            in_specs=[tok, tok, tok] + pages + pages,
            out_specs=pl.BlockSpec((1, t, g), lambda b, pt: (b, 0, 0)),
        ),
        out_shape=jax.ShapeDtypeStruct((db, t, g), F32),
        compiler_params=_cparams(("arbitrary",)),
        name="attn_sample",
    )(page_table, q, k_new, v_new, *([cache_k] * n_pages), *([cache_v] * n_pages))


def _layernorm_silu(y, g, b):
    mu = jnp.mean(y, axis=-1, keepdims=True)
    yc = y - mu
    var = jnp.mean(yc * yc, axis=-1, keepdims=True)
    yn = yc * lax.rsqrt(var + LN_EPS) * g + b
    return yn * jax.nn.sigmoid(yn)


def _pool_window_select(sums, shape):
    grp = _lane_group(shape, GROUP // len(POOL_WINDOWS))
    sel = sums[POOL_WINDOWS[-1]]
    for gi in range(len(POOL_WINDOWS) - 2, -1, -1):
        sel = jnp.where(grp == gi, sums[POOL_WINDOWS[gi]], sel)
    win = jnp.left_shift(jnp.int32(POOL_WINDOWS[0]), grp)
    return sel, win


def _mix_prompt_kernel(pp_ref, p_ref, ap_ref, a_ref, u_ref, vg_ref, poolw_ref, pscale_ref,
                       convw_ref, convb_ref, lng_ref, lnb_ref, convpw_ref, sw_ref, sb_ref,
                       pool_o, conv_o, gmlp_o, p_scr, a_scr):
    c = pl.program_id(1)
    C = CHUNK
    has_prev = c > 0
    p_cur = p_ref[0]
    p_scr[0:C, :] = jnp.where(has_prev, pp_ref[0], 0.0)
    p_scr[C:2 * C, :] = p_cur
    a_scr[0:C, :] = jnp.where(has_prev, ap_ref[0], 0.0)
    a_scr[C:2 * C, :] = a_ref[0]

    w2 = p_scr[C - 24:2 * C, :] + p_scr[C - 25:2 * C - 1, :]
    w4 = w2[8:C + 24] + w2[6:C + 22]
    w8 = w4[8:C + 16] + w4[4:C + 12]
    w16 = w8[8:C + 8] + w8[0:C]
    sums = {2: w2[24:C + 24], 4: w4[16:C + 16], 8: w8[8:C + 8], 16: w16}
    sel, win = _pool_window_select(sums, (C, GROUP))
    t_glob = c * C + lax.broadcasted_iota(jnp.int32, (C, GROUP), 0)
    count = jnp.minimum(t_glob + 1, win).astype(F32)
    m = sel / count - p_cur
    pool_o[0] = _dot(m.astype(BF16), poolw_ref[...]) * pscale_ref[...]

    base = C - CONV_HIST
    y = jnp.zeros((C, GROUP), F32)
    for r in range(8):
        taps = [j for j in range(CONV_WIDTH) if (base + j) % 8 == r]
        rows = C + (8 if r else 0)
        z = jnp.zeros((rows, GROUP), F32)
        for j in taps:
            start = base + j - r
            z = z + a_scr[start:start + rows, :] * convw_ref[j:j + 1, :]
        y = y + z[r:r + C]
    y = _layernorm_silu(y + convb_ref[...], lng_ref[...], lnb_ref[...])
    conv_o[0] = _dot(y.astype(BF16), convpw_ref[...])

    vgb = vg_ref[0].astype(BF16)
    tri = lax.broadcasted_iota(jnp.int32, (C, C), 0) >= lax.broadcasted_iota(jnp.int32, (C, C), 1)
    grp = _lane_group((C, GROUP), HEAD_DIM)
    s = sb_ref[...]
    for g in range(N_HEADS):
        wc = jnp.where(tri, sw_ref[g], 0.0).astype(BF16)
        s = s + jnp.where(grp == g, _dot(wc, vgb), 0.0)
    gmlp_o[0] = u_ref[0] * s


def _mix_prompt(p, a, u, vg, poolw, pscale, convw, convb, lng, lnb, convpw, sw, sb):
    b, t, g = p.shape
    cur = pl.BlockSpec((1, CHUNK, g), lambda bi, c: (bi, c, 0))
    prev = pl.BlockSpec((1, CHUNK, g), lambda bi, c: (bi, jnp.maximum(c - 1, 0), 0))

    def full(arr):
        nd = arr.ndim
        return pl.BlockSpec(arr.shape, lambda bi, c: (0,) * nd)

    out = jax.ShapeDtypeStruct((b, t, g), F32)
    weights = (poolw, pscale, convw, convb, lng, lnb, convpw, sw, sb)
    return pl.pallas_call(
        _mix_prompt_kernel,
        grid=(b, t // CHUNK),
        in_specs=[prev, cur, prev, cur, cur, cur] + [full(w) for w in weights],
        out_specs=[cur] * 3,
        out_shape=[out] * 3,
        scratch_shapes=[pltpu.VMEM((2 * CHUNK, g), F32), pltpu.VMEM((2 * CHUNK, g), F32)],
        compiler_params=_cparams(("parallel", "arbitrary")),
        name="mix_prompt",
    )(p, p, a, a, u, vg, *weights)


def _mix_sample_kernel(hp_ref, p_ref, hc_ref, a_ref, u_ref, vg_ref, poolw_ref, pscale_ref,
                       convw_ref, convb_ref, lng_ref, lnb_ref, convpw_ref, sw_ref, sb_ref,
                       pool_o, conv_o, gmlp_o, *, t_new):
    T = t_new
    nb = p_ref.shape[1]
    shape = (nb, GROUP)

    def pool_row(k):
        return hp_ref[k] if k < POOL_HIST else p_ref[k - POOL_HIST]

    def conv_row(k):
        return hc_ref[k] if k < CONV_HIST else a_ref[k - CONV_HIST]

    for t in range(T):
        acc = jnp.zeros(shape, F32)
        sums = {}
        for i in range(POOL_WINDOWS[-1]):
            k = POOL_HIST + t - i
            if k >= 0:
                acc = acc + pool_row(k)
            if i + 1 in POOL_WINDOWS:
                sums[i + 1] = acc
        sel, win = _pool_window_select(sums, shape)
        count = jnp.minimum(POOL_HIST + t + 1, win).astype(F32)
        m = sel / count - p_ref[t]
        pool_o[t] = _dot(m.astype(BF16), poolw_ref[...]) * pscale_ref[...]

        y = jnp.zeros(shape, F32)
        for j in range(CONV_WIDTH):
            y = y + conv_row(t + j) * convw_ref[j:j + 1, :]
        y = _layernorm_silu(y + convb_ref[...], lng_ref[...], lnb_ref[...])
        conv_o[t] = _dot(y.astype(BF16), convpw_ref[...])

        s = jnp.broadcast_to(sb_ref[t:t + 1, :], shape)
        for j in range(t + 1):
            s = s + sw_ref[t, j:j + 1, :] * vg_ref[j]
        gmlp_o[t] = u_ref[t] * s


def _mix_sample(hp, p, hc, a, u, vg, poolw, pscale, convw, convb, lng, lnb, convpw, sw, sb):
    t, db, g = p.shape

    def full(arr):
        nd = arr.ndim
        return pl.BlockSpec(arr.shape, lambda i: (0,) * nd)

    args = (hp, p, hc, a, u, vg, poolw, pscale, convw, convb, lng, lnb, convpw, sw, sb)
    out = jax.ShapeDtypeStruct((t, db, g), F32)
    return pl.pallas_call(
        functools.partial(_mix_sample_kernel, t_new=t),
        grid=(1,),
        in_specs=[full(x) for x in args],
        out_specs=[full(out)] * 3,
        out_shape=[out] * 3,
        compiler_params=_cparams(("arbitrary",)),
        name="mix_sample",
    )(*args)


def _row_tile(n, cap):
    t = min(n, cap)
    while n % t:
        t //= 2
    return t


def _ffn_tile(f):
    for tf in (512, 256, 128):
        if f % tf == 0:
            return tf
    return f


def _layer_weights(l, ffn1_norm, ffn1_w_gate, ffn1_w_up, ffn1_w_down, mix_norm, w_in, q_norm, k_norm,
                   pool_w, pool_scale, conv_w, conv_b, conv_ln_g, conv_ln_b, conv_pw,
                   gate_norm, spatial_w, spatial_b, w_out, ffn2_norm, ffn2_w_gate, ffn2_w_up, ffn2_w_down):
    row = lambda v: v[l].reshape(1, -1)
    head_id = jnp.arange(GROUP) // HEAD_DIM
    seg = jnp.where(head_id[:, None] == head_id[None, :], 1.0 / HEAD_DIM, 0.0).astype(BF16)
    n_win = len(POOL_WINDOWS)
    pg = GROUP // n_win
    pool_bd = jnp.zeros((GROUP, GROUP), F32)
    for gi in range(n_win):
        pool_bd = pool_bd.at[gi * pg:(gi + 1) * pg, gi * pg:(gi + 1) * pg].set(pool_w[l, gi])
    sb_lanes = jnp.repeat(spatial_b[l].T, HEAD_DIM, axis=1)
    return dict(
        ffn1=(row(ffn1_norm), ffn1_w_gate[l].astype(BF16), ffn1_w_up[l].astype(BF16), ffn1_w_down[l].astype(BF16)),
        ffn2=(row(ffn2_norm), ffn2_w_gate[l].astype(BF16), ffn2_w_up[l].astype(BF16), ffn2_w_down[l].astype(BF16)),
        proj_in=(row(mix_norm), w_in[l].astype(BF16), jnp.tile(q_norm[l], N_HEADS).reshape(1, -1),
                 jnp.tile(k_norm[l], N_HEADS).reshape(1, -1), row(gate_norm), seg),
        w_out=w_out[l].astype(BF16),
        mix=(pool_bd.astype(BF16), row(pool_scale), conv_w[l], row(conv_b), row(conv_ln_g), row(conv_ln_b),
             conv_pw[l].astype(BF16)),
        spatial_w=spatial_w[l],
        spatial_b_lanes=sb_lanes,
    )


def _token_stage_in(x, w):
    n = x.shape[0]
    x1 = _ffn(x, *w["ffn1"], tm=_row_tile(n, 1024), tf=_ffn_tile(w["ffn1"][1].shape[1]))
    return x1, _proj_in(x1, *w["proj_in"], tm=_row_tile(n, 512))


def _token_stage_out(x1, att, pool, conv, gmlp, w):
    n = x1.shape[0]
    x2 = _proj_out(x1, att, pool, conv, gmlp, w["w_out"], tm=_row_tile(n, 1024))
    return _ffn(x2, *w["ffn2"], tm=_row_tile(n, 1024), tf=_ffn_tile(w["ffn2"][1].shape[1]))


def _prompt_layer(x, w):
    b, t, d = x.shape
    x1, (q, k, v, p, a, u, vg) = _token_stage_in(x.reshape(b * t, d), w)
    r3 = lambda arr: arr.reshape(b, t, GROUP)
    q, k, v, p, a, u, vg = map(r3, (q, k, v, p, a, u, vg))
    att = _attn_prompt(q, k, v)
    pool, conv, gmlp = _mix_prompt(p, a, u, vg, *w["mix"], w["spatial_w"], w["spatial_b_lanes"])
    flat = lambda arr: arr.reshape(b * t, GROUP)
    y = _token_stage_out(x1, flat(att), flat(pool), flat(conv), flat(gmlp), w)
    return y.reshape(b, t, d), k, v, p[:, -POOL_HIST:], a[:, -CONV_HIST:]


def _sample_layer(x, cache_k, cache_v, page_table, page_base, hist_pool, hist_conv, w):
    db, t, d = x.shape
    x1, (q, k, v, p, a, u, vg) = _token_stage_in(x.reshape(db * t, d), w)
    r3 = lambda arr: arr.reshape(db, t, GROUP)
    q, k, v, p, a, u, vg = map(r3, (q, k, v, p, a, u, vg))
    att = _attn_sample(q, k, v, cache_k, cache_v, page_table, page_base)
    tm = lambda arr: jnp.swapaxes(arr, 0, 1)
    sw_lanes = jnp.repeat(jnp.transpose(w["spatial_w"][:, :t, :t], (1, 2, 0)), HEAD_DIM, axis=2)
    hp_t, p_t, hc_t, a_t = tm(hist_pool), tm(p), tm(hist_conv), tm(a)
    pool, conv, gmlp = _mix_sample(hp_t, p_t, hc_t, a_t, tm(u), tm(vg),
                                   *w["mix"], sw_lanes, w["spatial_b_lanes"][:t])
    flat = lambda arr: tm(arr).reshape(db * t, GROUP)
    y = _token_stage_out(x1, att.reshape(db * t, GROUP), flat(pool), flat(conv), flat(gmlp), w)
    pool_state = tm(jnp.concatenate([hp_t, p_t], axis=0)[-POOL_HIST:])
    conv_state = tm(jnp.concatenate([hc_t, a_t], axis=0)[-CONV_HIST:])
    return y.reshape(db, t, d), k, v, pool_state, conv_state, vg


def kernel(x_prompt, x_sample, cache_k, cache_v, state_pool, state_conv, page_table, ffn1_norm, ffn1_w_gate, ffn1_w_up, ffn1_w_down, mix_norm, w_in, q_norm, k_norm, pool_w, pool_scale, conv_w, conv_b, conv_ln_g, conv_ln_b, conv_pw, gate_norm, spatial_w, spatial_b, w_out, ffn2_norm, ffn2_w_gate, ffn2_w_up, ffn2_w_down):
    depth, n_pool = cache_k.shape[0], cache_k.shape[1]
    b, t = x_prompt.shape[0], x_prompt.shape[1]
    db, ts = x_sample.shape[0], x_sample.shape[1]
    ck = jnp.transpose(cache_k, (0, 1, 3, 4, 2)).reshape(depth * n_pool, GROUP, PAGE)
    cv = jnp.transpose(cache_v, (0, 1, 3, 4, 2)).reshape(depth * n_pool, GROUP, PAGE)
    hp, hs = x_prompt, x_sample
    outs = [[] for _ in range(9)]
    for l in range(depth):
        w = _layer_weights(l, ffn1_norm, ffn1_w_gate, ffn1_w_up, ffn1_w_down, mix_norm, w_in, q_norm, k_norm,
                           pool_w, pool_scale, conv_w, conv_b, conv_ln_g, conv_ln_b, conv_pw,
                           gate_norm, spatial_w, spatial_b, w_out, ffn2_norm, ffn2_w_gate, ffn2_w_up, ffn2_w_down)
        hp, kp, vp, pp, cp = _prompt_layer(hp, w)
        hs, ks, vs, ps, cs, gs = _sample_layer(hs, ck, cv, page_table, l * n_pool, state_pool[l], state_conv[l], w)
        heads = lambda arr, n, s: arr.reshape(n, s, N_HEADS, HEAD_DIM)
        for lst, val in zip(outs, (heads(kp, b, t), heads(vp, b, t), heads(ks, db, ts), heads(vs, db, ts),
                                   pp, ps, cp, cs, gs)):
            lst.append(val)
    return (hp, hs) + tuple(jnp.stack(lst) for lst in outs)
```

```python
import functools

import jax
import jax.numpy as jnp
from jax import lax
from jax.experimental import pallas as pl
from jax.experimental.pallas import tpu as pltpu

F32 = jnp.float32
BF16 = jnp.bfloat16
NEG_INF = float("-inf")

GROUP = 256
HEAD_DIM = 64
N_HEADS = GROUP // HEAD_DIM
MOBA_BLOCK = 256
MOBA_TOP_K = 3
PAGE = 128
POOL_WINDOWS = (2, 4, 8, 16)
POOL_HIST = 15
CONV_WIDTH = 31
CONV_HIST = 30
CHUNK = 128
RMS_EPS = 1e-6
LN_EPS = 1e-5
VMEM_LIMIT = 56 * 1024 * 1024


def _cparams(sem):
    return pltpu.CompilerParams(dimension_semantics=sem, vmem_limit_bytes=VMEM_LIMIT)


def _rms(x, g):
    ms = jnp.mean(x * x, axis=-1, keepdims=True)
    return x * lax.rsqrt(ms + RMS_EPS) * g


def _dot(a, b):
    return jnp.dot(a, b, preferred_element_type=F32)


def _dot_nt(a, b):
    return lax.dot_general(a, b, (((1,), (1,)), ((), ())), preferred_element_type=F32)


def _split(a):
    hi = a.astype(BF16)
    lo = (a - hi.astype(F32)).astype(BF16)
    return hi, lo


def _dot3_nt(a, b):
    ah, al = _split(a)
    bh, bl = _split(b)
    return _dot_nt(ah, bh) + _dot_nt(ah, bl) + _dot_nt(al, bh)


def _lane_group(shape, width):
    return lax.broadcasted_iota(jnp.int32, shape, len(shape) - 1) // width


def _ffn_kernel(*refs, n_f, n_mix):
    x_ref = refs[0]
    mix_refs = refs[1:1 + n_mix]
    rest = refs[1 + n_mix:]
    if n_mix:
        wo_ref, rest = rest[0], rest[1:]
    g_ref, wg_ref, wu_ref, wd_ref, o_ref, h_scr, acc_scr = rest
    f = pl.program_id(1)

    @pl.when(f == 0)
    def _():
        x = x_ref[...]
        for idx, m_ref in enumerate(mix_refs):
            x = x + _dot(m_ref[...].astype(BF16), wo_ref[idx * GROUP:(idx + 1) * GROUP, :])
        o_ref[...] = x
        h_scr[...] = _rms(x, g_ref[...]).astype(BF16)
        acc_scr[...] = jnp.zeros_like(acc_scr)

    h = h_scr[...]
    gate = _dot(h, wg_ref[...].astype(BF16))
    up = _dot(h, wu_ref[...].astype(BF16))
    act = (gate * jax.nn.sigmoid(gate) * up).astype(BF16)
    acc_scr[...] += _dot(act, wd_ref[...].astype(BF16))

    @pl.when(f == n_f - 1)
    def _():
        o_ref[...] = o_ref[...] + 0.5 * acc_scr[...]


def _ffn(x, mixes, w_out, norm, wg, wu, wd, layer, tm, tf):
    n, d = x.shape
    f = wg.shape[2]
    n_mix = len(mixes)
    row = lambda i, j: (i, 0)
    in_specs = [pl.BlockSpec((tm, d), row)] + [pl.BlockSpec((tm, GROUP), row)] * n_mix
    args = [x, *mixes]
    if n_mix:
        in_specs.append(pl.BlockSpec(w_out.shape, lambda i, j: (0, 0)))
        args.append(w_out)
    in_specs += [pl.BlockSpec((1, d), lambda i, j: (0, 0)),
                 pl.BlockSpec((None, d, tf), lambda i, j: (layer, 0, j)),
                 pl.BlockSpec((None, d, tf), lambda i, j: (layer, 0, j)),
                 pl.BlockSpec((None, tf, d), lambda i, j: (layer, j, 0))]
    args += [norm, wg, wu, wd]
    return pl.pallas_call(
        functools.partial(_ffn_kernel, n_f=f // tf, n_mix=n_mix),
        grid=(n // tm, f // tf),
        in_specs=in_specs,
        out_specs=pl.BlockSpec((tm, d), row),
        out_shape=jax.ShapeDtypeStruct((n, d), F32),
        scratch_shapes=[pltpu.VMEM((tm, d), BF16), pltpu.VMEM((tm, d), F32)],
        compiler_params=_cparams(("parallel", "arbitrary")),
        name="ffn_out" if n_mix else "ffn",
    )(*args)


def _proj_in_kernel(x_ref, g_ref, w_ref, qn_ref, kn_ref, gn_ref, seg_ref,
                    q_o, k_o, v_o, p_o, a_o, u_o, vg_o):
    G = GROUP
    h = _rms(x_ref[...], g_ref[...]).astype(BF16)
    z = _dot(h, w_ref[...])
    seg = seg_ref[...]

    def head_rms(t, gain):
        sq = t * t
        hi, lo = _split(sq)
        ms = _dot(hi, seg) + _dot(lo, seg)
        return t * lax.rsqrt(ms + RMS_EPS) * gain

    q_o[...] = head_rms(z[:, 0:G], qn_ref[...])
    k_o[...] = head_rms(z[:, G:2 * G], kn_ref[...])
    v_o[...] = z[:, 2 * G:3 * G]
    p_o[...] = z[:, 3 * G:4 * G]
    a_o[...] = z[:, 4 * G:5 * G] * jax.nn.sigmoid(z[:, 5 * G:6 * G])
    d = jax.nn.gelu(z[:, 6 * G:8 * G])
    u_o[...] = d[:, 0:G]
    vg_o[...] = _rms(d[:, G:2 * G], gn_ref[...])


def _proj_in(x, norm, w_in, qn, kn, gn, seg, tm):
    n, d = x.shape
    cols = w_in.shape[1]
    row = lambda i: (i, 0)
    fixed = lambda i: (0, 0)
    out = jax.ShapeDtypeStruct((n, GROUP), F32)
    return pl.pallas_call(
        _proj_in_kernel,
        grid=(n // tm,),
        in_specs=[pl.BlockSpec((tm, d), row),
                  pl.BlockSpec((1, d), fixed),
                  pl.BlockSpec((d, cols), fixed),
                  pl.BlockSpec((1, GROUP), fixed),
                  pl.BlockSpec((1, GROUP), fixed),
                  pl.BlockSpec((1, GROUP), fixed),
                  pl.BlockSpec((GROUP, GROUP), fixed)],
        out_specs=[pl.BlockSpec((tm, GROUP), row)] * 7,
        out_shape=[out] * 7,
        compiler_params=_cparams(("parallel",)),
        name="proj_in",
    )(x, norm, w_in, qn, kn, gn, seg)


GATE_ROWS = 8
VT_ROWS = HEAD_DIM + 16


def _select_bias_rows(gate, n_valid):
    rows = lax.broadcasted_iota(jnp.int32, gate.shape, 0)
    gm = jnp.where(rows < n_valid, gate, NEG_INF)
    bias = jnp.where(rows == n_valid, 0.0, NEG_INF)
    for n in range(gate.shape[0]):
        gn = gm[n:n + 1, :]
        beats = (gm > gn) | ((gm == gn) & (rows < n))
        rank = jnp.sum(jnp.where(beats, 1.0, 0.0), axis=0, keepdims=True)
        bias = jnp.where((rows == n) & (rows < n_valid) & (rank < MOBA_TOP_K), 0.0, bias)
    return bias


def _attn_prompt_kernel(q_ref, k_ref, v_ref, o_ref, kb_scr, vt_scr, kbd_scr, bias_scr, *, n_blk):
    i = pl.program_id(1)
    B = MOBA_BLOCK
    lane_head = _lane_group((1, GROUP), HEAD_DIM)

    @pl.when(i == 0)
    def _():
        kbd_scr[...] = jnp.zeros_like(kbd_scr)
        ones = jnp.ones((VT_ROWS - HEAD_DIM, B), BF16)
        for n in range(n_blk):
            kn = k_ref[0, n * B:(n + 1) * B, :]
            kb_scr[n] = kn.astype(BF16)
            mean = jnp.mean(kn, axis=0, keepdims=True)
            vt = v_ref[0, n * B:(n + 1) * B, :].T
            for h in range(N_HEADS):
                r = h * GATE_ROWS + n
                kbd_scr[r:r + 1, :] = jnp.where(lane_head == h, mean, 0.0)
                vt_scr[n, h, 0:HEAD_DIM, :] = vt[h * HEAD_DIM:(h + 1) * HEAD_DIM, :].astype(BF16)
                vt_scr[n, h, HEAD_DIM:VT_ROWS, :] = ones

    q = q_ref[0]
    gate = _dot3_nt(kbd_scr[...], q)
    for h in range(N_HEADS):
        rows = slice(h * GATE_ROWS, (h + 1) * GATE_ROWS)
        bias_scr[rows, :] = _select_bias_rows(gate[rows, :], i)

    neg_rel = (lax.broadcasted_iota(jnp.int32, (B, B), 0) - lax.broadcasted_iota(jnp.int32, (B, B), 1)).astype(F32)
    slopes = [2.0 ** (-8.0 * (h + 1) / N_HEADS) for h in range(N_HEADS)]
    q_scaled = q * (HEAD_DIM ** -0.5)
    in_head = _lane_group((B, GROUP), HEAD_DIM)
    qh = [jnp.where(in_head == h, q_scaled, 0.0).astype(BF16) for h in range(N_HEADS)]

    alibi = [slopes[h] * neg_rel for h in range(N_HEADS)]
    alibi_causal = [jnp.where(neg_rel <= 0.0, a, NEG_INF) for a in alibi]

    def scores(blk):
        kn = kb_scr[blk]
        return tuple(_dot_nt(kn, qh[h]) for h in range(N_HEADS))

    def absorb(blk, qk, ms, accs):
        is_own = blk == i
        dist = ((i - blk) * B).astype(F32)
        new_ms, new_accs = [], []
        for h in range(N_HEADS):
            bias = bias_scr[pl.ds(h * GATE_ROWS + blk, 1), :] - slopes[h] * dist
            s = qk[h] + jnp.where(is_own, alibi_causal[h], alibi[h]) + bias
            m_new = jnp.maximum(ms[h], jnp.max(s, axis=0, keepdims=True))
            alpha = jnp.exp(ms[h] - m_new)
            p = jnp.exp(s - m_new)
            new_ms.append(m_new)
            new_accs.append(alpha * accs[h] + _dot(vt_scr[blk, h], p.astype(BF16)))
        return tuple(new_ms), tuple(new_accs)

    def body(j, carry):
        qk, ms, accs = carry
        qk_next = scores(j)
        ms, accs = absorb(jnp.where(j == 0, i, j - 1), qk, ms, accs)
        return qk_next, ms, accs

    init = (scores(i), (jnp.full((1, B), NEG_INF, F32),) * N_HEADS, (jnp.zeros((VT_ROWS, B), F32),) * N_HEADS)
    qk, ms, accs = lax.fori_loop(0, i, body, init)
    ms, accs = absorb(jnp.where(i == 0, i, i - 1), qk, ms, accs)
    out_t = jnp.concatenate([a[0:HEAD_DIM, :] / a[HEAD_DIM:HEAD_DIM + 1, :] for a in accs], axis=0)
    o_ref[0] = out_t.T


def _attn_prompt(q, k, v):
    b, t, g = q.shape
    n_blk = t // MOBA_BLOCK
    assert t % MOBA_BLOCK == 0 and n_blk <= GATE_ROWS
    return pl.pallas_call(
        functools.partial(_attn_prompt_kernel, n_blk=n_blk),
        grid=(b, n_blk),
        in_specs=[pl.BlockSpec((1, MOBA_BLOCK, g), lambda bi, i: (bi, i, 0)),
                  pl.BlockSpec((1, t, g), lambda bi, i: (bi, 0, 0)),
                  pl.BlockSpec((1, t, g), lambda bi, i: (bi, 0, 0))],
        out_specs=pl.BlockSpec((1, MOBA_BLOCK, g), lambda bi, i: (bi, i, 0)),
        out_shape=jax.ShapeDtypeStruct((b, t, g), F32),
        scratch_shapes=[pltpu.VMEM((n_blk, MOBA_BLOCK, g), BF16),
                        pltpu.VMEM((n_blk, N_HEADS, VT_ROWS, MOBA_BLOCK), BF16),
                        pltpu.VMEM((N_HEADS * GATE_ROWS, g), F32),
                        pltpu.VMEM((N_HEADS * GATE_ROWS, MOBA_BLOCK), F32)],
        compiler_params=_cparams(("parallel", "arbitrary")),
        name="attn_prompt",
    )(q, k, v)


def _attn_sample_kernel(pt_ref, q_ref, kn_ref, vn_ref, *rest, n_pages, t_new):
    k_pages = rest[:n_pages]
    v_pages = rest[n_pages:2 * n_pages]
    o_ref = rest[2 * n_pages]
    T = t_new
    R = N_HEADS * T
    n_blk = n_pages * PAGE // MOBA_BLOCK
    past = n_pages * PAGE
    scale = HEAD_DIM ** -0.5

    q = q_ref[0] * scale
    q_rows = jnp.concatenate([q] * N_HEADS, axis=0)
    row_head = lax.broadcasted_iota(jnp.int32, (R, GROUP), 0) // T
    qbd = jnp.where(_lane_group((R, GROUP), HEAD_DIM) == row_head, q_rows, 0.0)
    q_hi, q_lo = _split(qbd)

    raw = []
    for pg in range(n_pages):
        k_hi, k_lo = _split(k_pages[pg][0])
        raw.append(_dot(q_hi, k_hi) + _dot(q_hi, k_lo) + _dot(q_lo, k_hi))
    per_blk = MOBA_BLOCK // PAGE
    gates = [jnp.sum(sum(raw[n * per_blk:(n + 1) * per_blk]), axis=-1, keepdims=True) for n in range(n_blk)]
    biases = []
    for n in range(n_blk):
        rank = jnp.zeros((R, 1), F32)
        for mth in range(n_blk):
            if mth != n:
                beats = (gates[mth] >= gates[n]) if mth < n else (gates[mth] > gates[n])
                rank = rank + jnp.where(beats, 1.0, 0.0)
        biases.append(jnp.where(rank < MOBA_TOP_K, 0.0, NEG_INF))

    r1 = lax.broadcasted_iota(jnp.int32, (R, 1), 0)
    slope = jnp.zeros((R, 1), F32)
    for h in range(N_HEADS):
        slope = jnp.where(r1 // T == h, 2.0 ** (-8.0 * (h + 1) / N_HEADS), slope)
    t_row = r1 % T
    pos_q = (past + t_row).astype(F32)
    col = lax.broadcasted_iota(jnp.int32, (R, PAGE), 1)

    scores = []
    for pg in range(n_pages):
        key_pos = (col + pg * PAGE).astype(F32)
        scores.append(raw[pg] - slope * (pos_q - key_pos) + biases[pg // per_blk])
    pad = jnp.zeros((PAGE - T, GROUP), F32)
    k_new = jnp.concatenate([kn_ref[0], pad], axis=0).astype(BF16)
    v_new = jnp.concatenate([vn_ref[0], pad], axis=0).astype(BF16)
    s_new = _dot_nt(q_hi, k_new) - slope * (t_row - col).astype(F32)
    s_new = jnp.where(col <= t_row, s_new, NEG_INF)
    scores.append(s_new)

    m = functools.reduce(jnp.maximum, [jnp.max(s, axis=-1, keepdims=True) for s in scores])
    l = jnp.zeros((R, 1), F32)
    acc = jnp.zeros((R, GROUP), F32)
    for pg in range(n_pages + 1):
        p = jnp.exp(scores[pg] - m)
        l = l + jnp.sum(p, axis=-1, keepdims=True)
        if pg == n_pages:
            acc = acc + _dot(p.astype(BF16), v_new)
        else:
            acc = acc + _dot_nt(p.astype(BF16), v_pages[pg][0].astype(BF16))
    acc = acc / l

    lane_head = _lane_group((T, GROUP), HEAD_DIM)
    out = jnp.zeros((T, GROUP), F32)
    for h in range(N_HEADS):
        out = jnp.where(lane_head == h, acc[h * T:(h + 1) * T, :], out)
    o_ref[0] = out


def _attn_sample(q, k_new, v_new, cache_k, cache_v, page_table, page_base):
    db, t, g = q.shape
    n_pages = page_table.shape[1]
    tok = pl.BlockSpec((1, t, g), lambda b, pt: (b, 0, 0))

    def page_spec(pg):
        return pl.BlockSpec((1, g, PAGE), lambda b, pt: (page_base + pt[b, pg], 0, 0))

    pages = [page_spec(pg) for pg in range(n_pages)]
    return pl.pallas_call(
        functools.partial(_attn_sample_kernel, n_pages=n_pages, t_new=t),
        grid_spec=pltpu.PrefetchScalarGridSpec(
            num_scalar_prefetch=1,
            grid=(db,),
            in_specs=[tok, tok, tok] + pages + pages,
            out_specs=pl.BlockSpec((1, t, g), lambda b, pt: (b, 0, 0)),
        ),
        out_shape=jax.ShapeDtypeStruct((db, t, g), F32),
        compiler_params=_cparams(("arbitrary",)),
        name="attn_sample",
    )(page_table, q, k_new, v_new, *([cache_k] * n_pages), *([cache_v] * n_pages))


def _layernorm_silu(y, g, b):
    mu = jnp.mean(y, axis=-1, keepdims=True)
    yc = y - mu
    var = jnp.mean(yc * yc, axis=-1, keepdims=True)
    yn = yc * lax.rsqrt(var + LN_EPS) * g + b
    return yn * jax.nn.sigmoid(yn)


def _pool_window_select(sums, shape):
    grp = _lane_group(shape, GROUP // len(POOL_WINDOWS))
    sel = sums[POOL_WINDOWS[-1]]
    for gi in range(len(POOL_WINDOWS) - 2, -1, -1):
        sel = jnp.where(grp == gi, sums[POOL_WINDOWS[gi]], sel)
    win = jnp.left_shift(jnp.int32(POOL_WINDOWS[0]), grp)
    return sel, win


MIX_ROWS = 256
HALO = 32


def _mix_prompt_kernel(pp_ref, p_ref, ap_ref, a_ref, u_ref, vg_ref, poolw_ref, pscale_ref,
                       convw_ref, convb_ref, lng_ref, lnb_ref, convpw_ref, sw_ref, sb_ref,
                       pool_o, conv_o, gmlp_o, p_scr, a_scr):
    c = pl.program_id(1)
    C = CHUNK
    has_prev = c > 0
    p_scr[0:HALO, :] = jnp.where(has_prev, pp_ref[0], 0.0)
    p_scr[HALO:HALO + MIX_ROWS, :] = p_ref[0]
    a_scr[0:HALO, :] = jnp.where(has_prev, ap_ref[0], 0.0)
    a_scr[HALO:HALO + MIX_ROWS, :] = a_ref[0]
    tri = lax.broadcasted_iota(jnp.int32, (C, C), 0) >= lax.broadcasted_iota(jnp.int32, (C, C), 1)
    w_causal = [jnp.where(tri, sw_ref[g], 0.0).astype(BF16) for g in range(N_HEADS)]
    grp = _lane_group((C, GROUP), HEAD_DIM)

    for sub in range(MIX_ROWS // C):
        s0 = HALO + sub * C
        rows = slice(sub * C, (sub + 1) * C)

        w2 = p_scr[s0 - 24:s0 + C, :] + p_scr[s0 - 25:s0 + C - 1, :]
        w4 = w2[8:C + 24] + w2[6:C + 22]
        w8 = w4[8:C + 16] + w4[4:C + 12]
        w16 = w8[8:C + 8] + w8[0:C]
        sums = {2: w2[24:C + 24], 4: w4[16:C + 16], 8: w8[8:C + 8], 16: w16}
        sel, win = _pool_window_select(sums, (C, GROUP))
        t_glob = c * MIX_ROWS + sub * C + lax.broadcasted_iota(jnp.int32, (C, GROUP), 0)
        count = jnp.minimum(t_glob + 1, win).astype(F32)
        m = sel / count - p_scr[s0:s0 + C, :]
        pool_o[0, rows, :] = _dot(m.astype(BF16), poolw_ref[...]) * pscale_ref[...]

        base = s0 - CONV_HIST
        y = jnp.zeros((C, GROUP), F32)
        for r in range(8):
            taps = [j for j in range(CONV_WIDTH) if (base + j) % 8 == r]
            n_rows = C + (8 if r else 0)
            z = jnp.zeros((n_rows, GROUP), F32)
            for j in taps:
                start = base + j - r
                z = z + a_scr[start:start + n_rows, :] * convw_ref[j:j + 1, :]
            y = y + z[r:r + C]
        y = _layernorm_silu(y + convb_ref[...], lng_ref[...], lnb_ref[...])
        conv_o[0, rows, :] = _dot(y.astype(BF16), convpw_ref[...])

        vgb = vg_ref[0, rows, :].astype(BF16)
        s = sb_ref[...]
        for g in range(N_HEADS):
            s = s + jnp.where(grp == g, _dot(w_causal[g], vgb), 0.0)
        gmlp_o[0, rows, :] = u_ref[0, rows, :] * s


def _mix_prompt(p, a, u, vg, poolw, pscale, convw, convb, lng, lnb, convpw, sw, sb):
    b, t, g = p.shape
    assert POOL_WINDOWS == (2, 4, 8, 16) and t % MIX_ROWS == 0 and MIX_ROWS % CHUNK == 0 and HALO >= CONV_HIST
    per = MIX_ROWS // HALO
    cur = pl.BlockSpec((1, MIX_ROWS, g), lambda bi, c: (bi, c, 0))
    prev = pl.BlockSpec((1, HALO, g), lambda bi, c: (bi, jnp.maximum(c * per - 1, 0), 0))

    def full(arr):
        nd = arr.ndim
        return pl.BlockSpec(arr.shape, lambda bi, c: (0,) * nd)

    out = jax.ShapeDtypeStruct((b, t, g), F32)
    weights = (poolw, pscale, convw, convb, lng, lnb, convpw, sw, sb)
    scratch = pltpu.VMEM((HALO + MIX_ROWS, g), F32)
    return pl.pallas_call(
        _mix_prompt_kernel,
        grid=(b, t // MIX_ROWS),
        in_specs=[prev, cur, prev, cur, cur, cur] + [full(w) for w in weights],
        out_specs=[cur] * 3,
        out_shape=[out] * 3,
        scratch_shapes=[scratch, scratch],
        compiler_params=_cparams(("parallel", "arbitrary")),
        name="mix_prompt",
    )(p, p, a, a, u, vg, *weights)


def _mix_sample_kernel(hp_ref, p_ref, hc_ref, a_ref, u_ref, vg_ref, poolw_ref, pscale_ref,
                       convw_ref, convb_ref, lng_ref, lnb_ref, convpw_ref, sw_ref, sb_ref,
                       pool_o, conv_o, gmlp_o, *, t_new):
    T = t_new
    nb = p_ref.shape[1]
    shape = (nb, GROUP)

    def pool_row(k):
        return hp_ref[k] if k < POOL_HIST else p_ref[k - POOL_HIST]

    def conv_row(k):
        return hc_ref[k] if k < CONV_HIST else a_ref[k - CONV_HIST]

    for t in range(T):
        acc = jnp.zeros(shape, F32)
        sums = {}
        for i in range(POOL_WINDOWS[-1]):
            k = POOL_HIST + t - i
            if k >= 0:
                acc = acc + pool_row(k)
            if i + 1 in POOL_WINDOWS:
                sums[i + 1] = acc
        sel, win = _pool_window_select(sums, shape)
        count = jnp.minimum(POOL_HIST + t + 1, win).astype(F32)
        m = sel / count - p_ref[t]
        pool_o[t] = _dot(m.astype(BF16), poolw_ref[...]) * pscale_ref[...]

        y = jnp.zeros(shape, F32)
        for j in range(CONV_WIDTH):
            y = y + conv_row(t + j) * convw_ref[j:j + 1, :]
        y = _layernorm_silu(y + convb_ref[...], lng_ref[...], lnb_ref[...])
        conv_o[t] = _dot(y.astype(BF16), convpw_ref[...])

        s = jnp.broadcast_to(sb_ref[t:t + 1, :], shape)
        for j in range(t + 1):
            s = s + sw_ref[t, j:j + 1, :] * vg_ref[j]
        gmlp_o[t] = u_ref[t] * s


def _mix_sample(hp, p, hc, a, u, vg, poolw, pscale, convw, convb, lng, lnb, convpw, sw, sb):
    t, db, g = p.shape

    def full(arr):
        nd = arr.ndim
        return pl.BlockSpec(arr.shape, lambda i: (0,) * nd)

    args = (hp, p, hc, a, u, vg, poolw, pscale, convw, convb, lng, lnb, convpw, sw, sb)
    out = jax.ShapeDtypeStruct((t, db, g), F32)
    return pl.pallas_call(
        functools.partial(_mix_sample_kernel, t_new=t),
        grid=(1,),
        in_specs=[full(x) for x in args],
        out_specs=[full(out)] * 3,
        out_shape=[out] * 3,
        compiler_params=_cparams(("arbitrary",)),
        name="mix_sample",
    )(*args)


def _row_tile(n, cap):
    t = min(n, cap)
    while n % t:
        t //= 2
    return t


def _ffn_tile(f):
    for tf in (512, 256, 128):
        if f % tf == 0:
            return tf
    return f


def _layer_weights(l, ffn1_norm, ffn1_w_gate, ffn1_w_up, ffn1_w_down, mix_norm, w_in, q_norm, k_norm,
                   pool_w, pool_scale, conv_w, conv_b, conv_ln_g, conv_ln_b, conv_pw,
                   gate_norm, spatial_w, spatial_b, w_out, ffn2_norm, ffn2_w_gate, ffn2_w_up, ffn2_w_down):
    row = lambda v: v[l].reshape(1, -1)
    head_id = jnp.arange(GROUP) // HEAD_DIM
    seg = jnp.where(head_id[:, None] == head_id[None, :], 1.0 / HEAD_DIM, 0.0).astype(BF16)
    n_win = len(POOL_WINDOWS)
    pg = GROUP // n_win
    pool_bd = jnp.zeros((GROUP, GROUP), F32)
    for gi in range(n_win):
        pool_bd = pool_bd.at[gi * pg:(gi + 1) * pg, gi * pg:(gi + 1) * pg].set(pool_w[l, gi])
    sb_lanes = jnp.repeat(spatial_b[l].T, HEAD_DIM, axis=1)
    return dict(
        layer=l,
        ffn1=(row(ffn1_norm), ffn1_w_gate, ffn1_w_up, ffn1_w_down),
        ffn2=(row(ffn2_norm), ffn2_w_gate, ffn2_w_up, ffn2_w_down),
        proj_in=(row(mix_norm), w_in[l].astype(BF16), jnp.tile(q_norm[l], N_HEADS).reshape(1, -1),
                 jnp.tile(k_norm[l], N_HEADS).reshape(1, -1), row(gate_norm), seg),
        w_out=w_out[l].astype(BF16),
        mix=(pool_bd.astype(BF16), row(pool_scale), conv_w[l], row(conv_b), row(conv_ln_g), row(conv_ln_b),
             conv_pw[l].astype(BF16)),
        spatial_w=spatial_w[l],
        spatial_b_lanes=sb_lanes,
    )


def _token_stage_in(x, w):
    n = x.shape[0]
    x1 = _ffn(x, (), None, *w["ffn1"], layer=w["layer"], tm=_row_tile(n, 1024),
              tf=_ffn_tile(w["ffn1"][1].shape[2]))
    return x1, _proj_in(x1, *w["proj_in"], tm=_row_tile(n, 512))


def _token_stage_out(x1, att, pool, conv, gmlp, w):
    n = x1.shape[0]
    return _ffn(x1, (att, pool, conv, gmlp), w["w_out"], *w["ffn2"], layer=w["layer"], tm=_row_tile(n, 1024),
                tf=_ffn_tile(w["ffn2"][1].shape[2]))


def _prompt_layer(x, w):
    b, t, d = x.shape
    x1, (q, k, v, p, a, u, vg) = _token_stage_in(x.reshape(b * t, d), w)
    r3 = lambda arr: arr.reshape(b, t, GROUP)
    q, k, v, p, a, u, vg = map(r3, (q, k, v, p, a, u, vg))
    att = _attn_prompt(q, k, v)
    pool, conv, gmlp = _mix_prompt(p, a, u, vg, *w["mix"], w["spatial_w"], w["spatial_b_lanes"])
    flat = lambda arr: arr.reshape(b * t, GROUP)
    y = _token_stage_out(x1, flat(att), flat(pool), flat(conv), flat(gmlp), w)
    return y.reshape(b, t, d), k, v, p[:, -POOL_HIST:], a[:, -CONV_HIST:]


def _sample_layer(x, cache_k, cache_v, page_table, page_base, hist_pool, hist_conv, w):
    db, t, d = x.shape
    x1, (q, k, v, p, a, u, vg) = _token_stage_in(x.reshape(db * t, d), w)
    r3 = lambda arr: arr.reshape(db, t, GROUP)
    q, k, v, p, a, u, vg = map(r3, (q, k, v, p, a, u, vg))
    att = _attn_sample(q, k, v, cache_k, cache_v, page_table, page_base)
    tm = lambda arr: jnp.swapaxes(arr, 0, 1)
    sw_lanes = jnp.repeat(jnp.transpose(w["spatial_w"][:, :t, :t], (1, 2, 0)), HEAD_DIM, axis=2)
    hp_t, p_t, hc_t, a_t = tm(hist_pool), tm(p), tm(hist_conv), tm(a)
    pool, conv, gmlp = _mix_sample(hp_t, p_t, hc_t, a_t, tm(u), tm(vg),
                                   *w["mix"], sw_lanes, w["spatial_b_lanes"][:t])
    flat = lambda arr: tm(arr).reshape(db * t, GROUP)
    y = _token_stage_out(x1, att.reshape(db * t, GROUP), flat(pool), flat(conv), flat(gmlp), w)
    pool_state = tm(jnp.concatenate([hp_t, p_t], axis=0)[-POOL_HIST:])
    conv_state = tm(jnp.concatenate([hc_t, a_t], axis=0)[-CONV_HIST:])
    return y.reshape(db, t, d), k, v, pool_state, conv_state, vg


def kernel(x_prompt, x_sample, cache_k, cache_v, state_pool, state_conv, page_table, ffn1_norm, ffn1_w_gate, ffn1_w_up, ffn1_w_down, mix_norm, w_in, q_norm, k_norm, pool_w, pool_scale, conv_w, conv_b, conv_ln_g, conv_ln_b, conv_pw, gate_norm, spatial_w, spatial_b, w_out, ffn2_norm, ffn2_w_gate, ffn2_w_up, ffn2_w_down):
    depth, n_pool = cache_k.shape[0], cache_k.shape[1]
    b, t = x_prompt.shape[0], x_prompt.shape[1]
    db, ts = x_sample.shape[0], x_sample.shape[1]
    ck = jnp.transpose(cache_k, (0, 1, 3, 4, 2)).reshape(depth * n_pool, GROUP, PAGE)
    cv = jnp.transpose(cache_v, (0, 1, 3, 4, 2)).reshape(depth * n_pool, GROUP, PAGE)
    hp, hs = x_prompt, x_sample
    outs = [[] for _ in range(9)]
    for l in range(depth):
        w = _layer_weights(l, ffn1_norm, ffn1_w_gate, ffn1_w_up, ffn1_w_down, mix_norm, w_in, q_norm, k_norm,
                           pool_w, pool_scale, conv_w, conv_b, conv_ln_g, conv_ln_b, conv_pw,
                           gate_norm, spatial_w, spatial_b, w_out, ffn2_norm, ffn2_w_gate, ffn2_w_up, ffn2_w_down)
        hp, kp, vp, pp, cp = _prompt_layer(hp, w)
        hs, ks, vs, ps, cs, gs = _sample_layer(hs, ck, cv, page_table, l * n_pool, state_pool[l], state_conv[l], w)
        heads = lambda arr, n, s: arr.reshape(n, s, N_HEADS, HEAD_DIM)
        for lst, val in zip(outs, (heads(kp, b, t), heads(vp, b, t), heads(ks, db, ts), heads(vs, db, ts),
                                   pp, ps, cp, cs, gs)):
            lst.append(val)
    return (hp, hs) + tuple(jnp.stack(lst) for lst in outs)
```

```python
import functools

import jax
import jax.numpy as jnp
from jax import lax
from jax.experimental import pallas as pl
from jax.experimental.pallas import tpu as pltpu

F32 = jnp.float32
BF16 = jnp.bfloat16
NEG_INF = float("-inf")

GROUP = 256
HEAD_DIM = 64
N_HEADS = GROUP // HEAD_DIM
MOBA_BLOCK = 256
MOBA_TOP_K = 3
PAGE = 128
POOL_WINDOWS = (2, 4, 8, 16)
POOL_HIST = 15
CONV_WIDTH = 31
CONV_HIST = 30
CHUNK = 128
RMS_EPS = 1e-6
LN_EPS = 1e-5
VMEM_LIMIT = 56 * 1024 * 1024


def _cparams(sem):
    return pltpu.CompilerParams(dimension_semantics=sem, vmem_limit_bytes=VMEM_LIMIT)


def _rms(x, g):
    ms = jnp.mean(x * x, axis=-1, keepdims=True)
    return x * lax.rsqrt(ms + RMS_EPS) * g


def _dot(a, b):
    return jnp.dot(a, b, preferred_element_type=F32)


def _dot_nt(a, b):
    return lax.dot_general(a, b, (((1,), (1,)), ((), ())), preferred_element_type=F32)


def _split(a):
    hi = a.astype(BF16)
    lo = (a - hi.astype(F32)).astype(BF16)
    return hi, lo


def _dot3_nt(a, b):
    ah, al = _split(a)
    bh, bl = _split(b)
    return _dot_nt(ah, bh) + _dot_nt(ah, bl) + _dot_nt(al, bh)


def _lane_group(shape, width):
    return lax.broadcasted_iota(jnp.int32, shape, len(shape) - 1) // width


def _ffn_kernel(*refs, n_f, n_mix):
    x_ref = refs[0]
    mix_refs = refs[1:1 + n_mix]
    rest = refs[1 + n_mix:]
    if n_mix:
        wo_ref, rest = rest[0], rest[1:]
    g_ref, wg_ref, wu_ref, wd_ref, o_ref, h_scr, acc_scr = rest
    f = pl.program_id(1)

    @pl.when(f == 0)
    def _():
        x = x_ref[...]
        for idx, m_ref in enumerate(mix_refs):
            x = x + _dot(m_ref[...].astype(BF16), wo_ref[idx * GROUP:(idx + 1) * GROUP, :])
        o_ref[...] = x
        h_scr[...] = _rms(x, g_ref[...]).astype(BF16)
        acc_scr[...] = jnp.zeros_like(acc_scr)

    h = h_scr[...]
    gate = _dot(h, wg_ref[...].astype(BF16))
    up = _dot(h, wu_ref[...].astype(BF16))
    act = (gate * jax.nn.sigmoid(gate) * up).astype(BF16)
    acc_scr[...] += _dot(act, wd_ref[...].astype(BF16))

    @pl.when(f == n_f - 1)
    def _():
        o_ref[...] = o_ref[...] + 0.5 * acc_scr[...]


def _ffn(x, mixes, w_out, norm, wg, wu, wd, layer, tm, tf):
    n, d = x.shape
    f = wg.shape[2]
    n_mix = len(mixes)
    row = lambda i, j: (i, 0)
    in_specs = [pl.BlockSpec((tm, d), row)] + [pl.BlockSpec((tm, GROUP), row)] * n_mix
    args = [x, *mixes]
    if n_mix:
        in_specs.append(pl.BlockSpec(w_out.shape, lambda i, j: (0, 0)))
        args.append(w_out)
    in_specs += [pl.BlockSpec((1, d), lambda i, j: (0, 0)),
                 pl.BlockSpec((None, d, tf), lambda i, j: (layer, 0, j)),
                 pl.BlockSpec((None, d, tf), lambda i, j: (layer, 0, j)),
                 pl.BlockSpec((None, tf, d), lambda i, j: (layer, j, 0))]
    args += [norm, wg, wu, wd]
    return pl.pallas_call(
        functools.partial(_ffn_kernel, n_f=f // tf, n_mix=n_mix),
        grid=(n // tm, f // tf),
        in_specs=in_specs,
        out_specs=pl.BlockSpec((tm, d), row),
        out_shape=jax.ShapeDtypeStruct((n, d), F32),
        scratch_shapes=[pltpu.VMEM((tm, d), BF16), pltpu.VMEM((tm, d), F32)],
        compiler_params=_cparams(("parallel", "arbitrary")),
        name="ffn_out" if n_mix else "ffn",
    )(*args)


def _proj_in_kernel(*refs, kv_transposed, n_alias):
    x_ref, g_ref, w_ref, qn_ref, kn_ref, gn_ref, seg_ref = refs[:7]
    q_o, k_o, v_o, p_o, a_o, u_o, vg_o = refs[7 + n_alias:]
    G = GROUP
    h = _rms(x_ref[...], g_ref[...]).astype(BF16)
    z = _dot(h, w_ref[...])
    seg = seg_ref[...]

    def head_rms(t, gain):
        sq = t * t
        hi, lo = _split(sq)
        ms = _dot(hi, seg) + _dot(lo, seg)
        return t * lax.rsqrt(ms + RMS_EPS) * gain

    q_o[...] = head_rms(z[:, 0:G], qn_ref[...])
    k = head_rms(z[:, G:2 * G], kn_ref[...])
    v = z[:, 2 * G:3 * G]
    if not kv_transposed:
        k_o[...] = k
        v_o[...] = v
    elif len(k_o.shape) == 2:
        k_o[...] = k.T
        v_o[...] = v.T
    else:
        k_o[0] = k.T
        v_o[0] = v.T
        for slot in range(1, k_o.shape[0]):
            k_o[slot] = jnp.zeros(k_o.shape[1:], F32)
            v_o[slot] = jnp.zeros(v_o.shape[1:], F32)
    p_o[...] = z[:, 3 * G:4 * G]
    a_o[...] = z[:, 4 * G:5 * G] * jax.nn.sigmoid(z[:, 5 * G:6 * G])
    d = jax.nn.gelu(z[:, 6 * G:8 * G])
    u_o[...] = d[:, 0:G]
    vg_o[...] = _rms(d[:, G:2 * G], gn_ref[...])


def _proj_in(x, norm, w_in, qn, kn, gn, seg, tm, kv_stack=None):
    n, d = x.shape
    cols = w_in.shape[1]
    row = lambda i: (i, 0)
    fixed = lambda i: (0, 0)
    out = jax.ShapeDtypeStruct((n, GROUP), F32)
    in_specs = [pl.BlockSpec((tm, d), row),
                pl.BlockSpec((1, d), fixed),
                pl.BlockSpec((d, cols), fixed),
                pl.BlockSpec((1, GROUP), fixed),
                pl.BlockSpec((1, GROUP), fixed),
                pl.BlockSpec((1, GROUP), fixed),
                pl.BlockSpec((GROUP, GROUP), fixed)]
    args = [x, norm, w_in, qn, kn, gn, seg]
    out_specs = [pl.BlockSpec((tm, GROUP), row)] * 7
    out_shape = [out] * 7
    aliases = {}
    if kv_stack is not None:
        layer, depth, b, t, prev = kv_stack
        per_b = t // tm
        if prev is None:
            assert layer == 0
            kv_spec = pl.BlockSpec((depth, None, GROUP, tm), lambda i: (0, i // per_b, 0, i % per_b))
        else:
            kv_spec = pl.BlockSpec((None, None, GROUP, tm), lambda i: (layer, i // per_b, 0, i % per_b))
        out_specs[1:3] = [kv_spec, kv_spec]
        out_shape[1:3] = [jax.ShapeDtypeStruct((depth, b, GROUP, t), F32)] * 2
        if prev is not None:
            in_specs += [pl.BlockSpec(memory_space=pl.ANY)] * 2
            aliases = {len(args): 1, len(args) + 1: 2}
            args += list(prev)
    return pl.pallas_call(
        functools.partial(_proj_in_kernel, kv_transposed=kv_stack is not None, n_alias=len(aliases)),
        grid=(n // tm,),
        in_specs=in_specs,
        out_specs=out_specs,
        out_shape=out_shape,
        input_output_aliases=aliases,
        compiler_params=_cparams(("parallel",)),
        name="proj_in",
    )(*args)


GATE_ROWS = 8
VT_ROWS = HEAD_DIM + 16


def _select_bias_rows(gate, n_valid):
    rows = lax.broadcasted_iota(jnp.int32, gate.shape, 0)
    gm = jnp.where(rows < n_valid, gate, NEG_INF)
    bias = jnp.where(rows == n_valid, 0.0, NEG_INF)
    for n in range(gate.shape[0]):
        gn = gm[n:n + 1, :]
        beats = (gm > gn) | ((gm == gn) & (rows < n))
        rank = jnp.sum(jnp.where(beats, 1.0, 0.0), axis=0, keepdims=True)
        bias = jnp.where((rows == n) & (rows < n_valid) & (rank < MOBA_TOP_K), 0.0, bias)
    return bias


def _attn_prompt_kernel(q_ref, k_ref, v_ref, o_ref, kb_scr, vt_scr, kbd_scr, bias_scr, *, n_blk):
    i = pl.program_id(1)
    B = MOBA_BLOCK
    lane_head = _lane_group((1, GROUP), HEAD_DIM)

    @pl.when(i == 0)
    def _():
        kbd_scr[...] = jnp.zeros_like(kbd_scr)
        ones = jnp.ones((VT_ROWS - HEAD_DIM, B), BF16)
        for n in range(n_blk):
            kn = k_ref[:, n * B:(n + 1) * B].T
            kb_scr[n] = kn.astype(BF16)
            mean = jnp.mean(kn, axis=0, keepdims=True)
            vt = v_ref[:, n * B:(n + 1) * B]
            for h in range(N_HEADS):
                r = h * GATE_ROWS + n
                kbd_scr[r:r + 1, :] = jnp.where(lane_head == h, mean, 0.0)
                vt_scr[n, h, 0:HEAD_DIM, :] = vt[h * HEAD_DIM:(h + 1) * HEAD_DIM, :].astype(BF16)
                vt_scr[n, h, HEAD_DIM:VT_ROWS, :] = ones

    q = q_ref[0]
    gate = _dot3_nt(kbd_scr[...], q)
    for h in range(N_HEADS):
        rows = slice(h * GATE_ROWS, (h + 1) * GATE_ROWS)
        bias_scr[rows, :] = _select_bias_rows(gate[rows, :], i)

    neg_rel = (lax.broadcasted_iota(jnp.int32, (B, B), 0) - lax.broadcasted_iota(jnp.int32, (B, B), 1)).astype(F32)
    slopes = [2.0 ** (-8.0 * (h + 1) / N_HEADS) for h in range(N_HEADS)]
    q_scaled = q * (HEAD_DIM ** -0.5)
    in_head = _lane_group((B, GROUP), HEAD_DIM)
    qh = [jnp.where(in_head == h, q_scaled, 0.0).astype(BF16) for h in range(N_HEADS)]

    alibi = [slopes[h] * neg_rel for h in range(N_HEADS)]
    alibi_causal = [jnp.where(neg_rel <= 0.0, a, NEG_INF) for a in alibi]

    def scores(blk):
        kn = kb_scr[blk]
        return tuple(_dot_nt(kn, qh[h]) for h in range(N_HEADS))

    def absorb(blk, qk, ms, accs):
        is_own = blk == i
        dist = ((i - blk) * B).astype(F32)
        new_ms, new_accs = [], []
        for h in range(N_HEADS):
            bias = bias_scr[pl.ds(h * GATE_ROWS + blk, 1), :] - slopes[h] * dist
            s = qk[h] + jnp.where(is_own, alibi_causal[h], alibi[h]) + bias
            m_new = jnp.maximum(ms[h], jnp.max(s, axis=0, keepdims=True))
            alpha = jnp.exp(ms[h] - m_new)
            p = jnp.exp(s - m_new)
            new_ms.append(m_new)
            new_accs.append(alpha * accs[h] + _dot(vt_scr[blk, h], p.astype(BF16)))
        return tuple(new_ms), tuple(new_accs)

    def body(j, carry):
        qk, ms, accs = carry
        qk_next = scores(j)
        ms, accs = absorb(jnp.where(j == 0, i, j - 1), qk, ms, accs)
        return qk_next, ms, accs

    init = (scores(i), (jnp.full((1, B), NEG_INF, F32),) * N_HEADS, (jnp.zeros((VT_ROWS, B), F32),) * N_HEADS)
    qk, ms, accs = lax.fori_loop(0, i, body, init)
    ms, accs = absorb(jnp.where(i == 0, i, i - 1), qk, ms, accs)
    out_t = jnp.concatenate([a[0:HEAD_DIM, :] / a[HEAD_DIM:HEAD_DIM + 1, :] for a in accs], axis=0)
    o_ref[0] = out_t.T


def _attn_prompt(q, kt, vt, layer):
    b, t, g = q.shape
    n_blk = t // MOBA_BLOCK
    assert t % MOBA_BLOCK == 0 and n_blk <= GATE_ROWS
    kv_spec = pl.BlockSpec((None, None, g, t), lambda bi, i: (layer, bi, 0, 0))
    return pl.pallas_call(
        functools.partial(_attn_prompt_kernel, n_blk=n_blk),
        grid=(b, n_blk),
        in_specs=[pl.BlockSpec((1, MOBA_BLOCK, g), lambda bi, i: (bi, i, 0)), kv_spec, kv_spec],
        out_specs=pl.BlockSpec((1, MOBA_BLOCK, g), lambda bi, i: (bi, i, 0)),
        out_shape=jax.ShapeDtypeStruct((b, t, g), F32),
        scratch_shapes=[pltpu.VMEM((n_blk, MOBA_BLOCK, g), BF16),
                        pltpu.VMEM((n_blk, N_HEADS, VT_ROWS, MOBA_BLOCK), BF16),
                        pltpu.VMEM((N_HEADS * GATE_ROWS, g), F32),
                        pltpu.VMEM((N_HEADS * GATE_ROWS, MOBA_BLOCK), F32)],
        compiler_params=_cparams(("parallel", "arbitrary")),
        name="attn_prompt",
    )(q, kt, vt)


def _attn_sample_kernel(pt_ref, q_ref, kn_ref, vn_ref, ck_hbm, cv_hbm, o_ref, k_buf, v_buf, sems,
                        *, n_pages, t_new, page_base):
    b = pl.program_id(0)

    def page_copies(seq, slot):
        out = []
        for pg in range(n_pages):
            page = page_base + pt_ref[seq, pg]
            out.append(pltpu.make_async_copy(ck_hbm.at[page], k_buf.at[slot, pg], sems.at[0, slot]))
            out.append(pltpu.make_async_copy(cv_hbm.at[page], v_buf.at[slot, pg], sems.at[1, slot]))
        return out

    @pl.when(b == 0)
    def _():
        for cp in page_copies(0, 0):
            cp.start()

    @pl.when(b + 1 < pl.num_programs(0))
    def _():
        for cp in page_copies(b + 1, (b + 1) % 2):
            cp.start()

    slot = b % 2
    for cp in page_copies(b, slot):
        cp.wait()
    k_pages = [k_buf.at[slot, pg] for pg in range(n_pages)]
    v_pages = [v_buf.at[slot, pg] for pg in range(n_pages)]
    T = t_new
    R = N_HEADS * T
    n_blk = n_pages * PAGE // MOBA_BLOCK
    past = n_pages * PAGE
    scale = HEAD_DIM ** -0.5

    q = q_ref[0] * scale
    q_rows = jnp.concatenate([q] * N_HEADS, axis=0)
    row_head = lax.broadcasted_iota(jnp.int32, (R, GROUP), 0) // T
    qbd = jnp.where(_lane_group((R, GROUP), HEAD_DIM) == row_head, q_rows, 0.0)
    q_hi, q_lo = _split(qbd)

    raw = []
    for pg in range(n_pages):
        k_hi, k_lo = _split(k_pages[pg][...])
        raw.append(_dot(q_hi, k_hi) + _dot(q_hi, k_lo) + _dot(q_lo, k_hi))
    per_blk = MOBA_BLOCK // PAGE
    gates = [jnp.sum(sum(raw[n * per_blk:(n + 1) * per_blk]), axis=-1, keepdims=True) for n in range(n_blk)]
    biases = []
    for n in range(n_blk):
        rank = jnp.zeros((R, 1), F32)
        for mth in range(n_blk):
            if mth != n:
                beats = (gates[mth] >= gates[n]) if mth < n else (gates[mth] > gates[n])
                rank = rank + jnp.where(beats, 1.0, 0.0)
        biases.append(jnp.where(rank < MOBA_TOP_K, 0.0, NEG_INF))

    r1 = lax.broadcasted_iota(jnp.int32, (R, 1), 0)
    slope = jnp.zeros((R, 1), F32)
    for h in range(N_HEADS):
        slope = jnp.where(r1 // T == h, 2.0 ** (-8.0 * (h + 1) / N_HEADS), slope)
    t_row = r1 % T
    pos_q = (past + t_row).astype(F32)
    col = lax.broadcasted_iota(jnp.int32, (R, PAGE), 1)

    scores = []
    for pg in range(n_pages):
        key_pos = (col + pg * PAGE).astype(F32)
        scores.append(raw[pg] - slope * (pos_q - key_pos) + biases[pg // per_blk])
    pad = jnp.zeros((PAGE - T, GROUP), F32)
    k_new = jnp.concatenate([kn_ref[0], pad], axis=0).astype(BF16)
    v_new = jnp.concatenate([vn_ref[0], pad], axis=0).astype(BF16)
    s_new = _dot_nt(q_hi, k_new) - slope * (t_row - col).astype(F32)
    s_new = jnp.where(col <= t_row, s_new, NEG_INF)
    scores.append(s_new)

    m = functools.reduce(jnp.maximum, [jnp.max(s, axis=-1, keepdims=True) for s in scores])
    l = jnp.zeros((R, 1), F32)
    acc = jnp.zeros((R, GROUP), F32)
    for pg in range(n_pages + 1):
        p = jnp.exp(scores[pg] - m)
        l = l + jnp.sum(p, axis=-1, keepdims=True)
        if pg == n_pages:
            acc = acc + _dot(p.astype(BF16), v_new)
        else:
            acc = acc + _dot_nt(p.astype(BF16), v_pages[pg][...].astype(BF16))
    acc = acc / l

    lane_head = _lane_group((T, GROUP), HEAD_DIM)
    out = jnp.zeros((T, GROUP), F32)
    for h in range(N_HEADS):
        out = jnp.where(lane_head == h, acc[h * T:(h + 1) * T, :], out)
    o_ref[0] = out


def _attn_sample(q, k_new, v_new, cache_k, cache_v, page_table, page_base):
    db, t, g = q.shape
    n_pages = page_table.shape[1]
    tok = pl.BlockSpec((1, t, g), lambda b, pt: (b, 0, 0))
    hbm = pl.BlockSpec(memory_space=pl.ANY)
    page_buf = pltpu.VMEM((2, n_pages, g, PAGE), F32)
    return pl.pallas_call(
        functools.partial(_attn_sample_kernel, n_pages=n_pages, t_new=t, page_base=page_base),
        grid_spec=pltpu.PrefetchScalarGridSpec(
            num_scalar_prefetch=1,
            grid=(db,),
            in_specs=[tok, tok, tok, hbm, hbm],
            out_specs=pl.BlockSpec((1, t, g), lambda b, pt: (b, 0, 0)),
            scratch_shapes=[page_buf, page_buf, pltpu.SemaphoreType.DMA((2, 2))],
        ),
        out_shape=jax.ShapeDtypeStruct((db, t, g), F32),
        compiler_params=_cparams(("arbitrary",)),
        name="attn_sample",
    )(page_table, q, k_new, v_new, cache_k, cache_v)


def _layernorm_silu(y, g, b):
    mu = jnp.mean(y, axis=-1, keepdims=True)
    yc = y - mu
    var = jnp.mean(yc * yc, axis=-1, keepdims=True)
    yn = yc * lax.rsqrt(var + LN_EPS) * g + b
    return yn * jax.nn.sigmoid(yn)


def _pool_window_select(sums, shape):
    grp = _lane_group(shape, GROUP // len(POOL_WINDOWS))
    sel = sums[POOL_WINDOWS[-1]]
    for gi in range(len(POOL_WINDOWS) - 2, -1, -1):
        sel = jnp.where(grp == gi, sums[POOL_WINDOWS[gi]], sel)
    win = jnp.left_shift(jnp.int32(POOL_WINDOWS[0]), grp)
    return sel, win


MIX_ROWS = 256
HALO = 32


def _mix_prompt_kernel(pp_ref, p_ref, ap_ref, a_ref, u_ref, vg_ref, poolw_ref, pscale_ref,
                       convw_ref, convb_ref, lng_ref, lnb_ref, convpw_ref, sw_ref, sb_ref,
                       pool_o, conv_o, gmlp_o, p_scr, a_scr):
    c = pl.program_id(1)
    C = CHUNK
    has_prev = c > 0
    p_scr[0:HALO, :] = jnp.where(has_prev, pp_ref[0], 0.0)
    p_scr[HALO:HALO + MIX_ROWS, :] = p_ref[0]
    a_scr[0:HALO, :] = jnp.where(has_prev, ap_ref[0], 0.0)
    a_scr[HALO:HALO + MIX_ROWS, :] = a_ref[0]
    tri = lax.broadcasted_iota(jnp.int32, (C, C), 0) >= lax.broadcasted_iota(jnp.int32, (C, C), 1)
    w_causal = [jnp.where(tri, sw_ref[g], 0.0).astype(BF16) for g in range(N_HEADS)]
    grp = _lane_group((C, GROUP), HEAD_DIM)

    for sub in range(MIX_ROWS // C):
        s0 = HALO + sub * C
        rows = slice(sub * C, (sub + 1) * C)

        w2 = p_scr[s0 - 24:s0 + C, :] + p_scr[s0 - 25:s0 + C - 1, :]
        w4 = w2[8:C + 24] + w2[6:C + 22]
        w8 = w4[8:C + 16] + w4[4:C + 12]
        w16 = w8[8:C + 8] + w8[0:C]
        sums = {2: w2[24:C + 24], 4: w4[16:C + 16], 8: w8[8:C + 8], 16: w16}
        sel, win = _pool_window_select(sums, (C, GROUP))
        t_glob = c * MIX_ROWS + sub * C + lax.broadcasted_iota(jnp.int32, (C, GROUP), 0)
        count = jnp.minimum(t_glob + 1, win).astype(F32)
        m = sel / count - p_scr[s0:s0 + C, :]
        pool_o[0, rows, :] = _dot(m.astype(BF16), poolw_ref[...]) * pscale_ref[...]

        base = s0 - CONV_HIST
        y = jnp.zeros((C, GROUP), F32)
        for r in range(8):
            taps = [j for j in range(CONV_WIDTH) if (base + j) % 8 == r]
            n_rows = C + (8 if r else 0)
            z = jnp.zeros((n_rows, GROUP), F32)
            for j in taps:
                start = base + j - r
                z = z + a_scr[start:start + n_rows, :] * convw_ref[j:j + 1, :]
            y = y + z[r:r + C]
        y = _layernorm_silu(y + convb_ref[...], lng_ref[...], lnb_ref[...])
        conv_o[0, rows, :] = _dot(y.astype(BF16), convpw_ref[...])

        vgb = vg_ref[0, rows, :].astype(BF16)
        s = sb_ref[...]
        for g in range(N_HEADS):
            s = s + jnp.where(grp == g, _dot(w_causal[g], vgb), 0.0)
        gmlp_o[0, rows, :] = u_ref[0, rows, :] * s


def _mix_prompt(p, a, u, vg, poolw, pscale, convw, convb, lng, lnb, convpw, sw, sb):
    b, t, g = p.shape
    assert POOL_WINDOWS == (2, 4, 8, 16) and t % MIX_ROWS == 0 and MIX_ROWS % CHUNK == 0 and HALO >= CONV_HIST
    per = MIX_ROWS // HALO
    cur = pl.BlockSpec((1, MIX_ROWS, g), lambda bi, c: (bi, c, 0))
    prev = pl.BlockSpec((1, HALO, g), lambda bi, c: (bi, jnp.maximum(c * per - 1, 0), 0))

    def full(arr):
        nd = arr.ndim
        return pl.BlockSpec(arr.shape, lambda bi, c: (0,) * nd)

    out = jax.ShapeDtypeStruct((b, t, g), F32)
    weights = (poolw, pscale, convw, convb, lng, lnb, convpw, sw, sb)
    scratch = pltpu.VMEM((HALO + MIX_ROWS, g), F32)
    return pl.pallas_call(
        _mix_prompt_kernel,
        grid=(b, t // MIX_ROWS),
        in_specs=[prev, cur, prev, cur, cur, cur] + [full(w) for w in weights],
        out_specs=[cur] * 3,
        out_shape=[out] * 3,
        scratch_shapes=[scratch, scratch],
        compiler_params=_cparams(("parallel", "arbitrary")),
        name="mix_prompt",
    )(p, p, a, a, u, vg, *weights)


def _mix_sample_kernel(hp_ref, p_ref, hc_ref, a_ref, u_ref, vg_ref, poolw_ref, pscale_ref,
                       convw_ref, convb_ref, lng_ref, lnb_ref, convpw_ref, sw_ref, sb_ref,
                       pool_o, conv_o, gmlp_o, *, t_new):
    T = t_new
    nb = p_ref.shape[1]
    shape = (nb, GROUP)

    def pool_row(k):
        return hp_ref[k] if k < POOL_HIST else p_ref[k - POOL_HIST]

    def conv_row(k):
        return hc_ref[k] if k < CONV_HIST else a_ref[k - CONV_HIST]

    for t in range(T):
        acc = jnp.zeros(shape, F32)
        sums = {}
        for i in range(POOL_WINDOWS[-1]):
            k = POOL_HIST + t - i
            if k >= 0:
                acc = acc + pool_row(k)
            if i + 1 in POOL_WINDOWS:
                sums[i + 1] = acc
        sel, win = _pool_window_select(sums, shape)
        count = jnp.minimum(POOL_HIST + t + 1, win).astype(F32)
        m = sel / count - p_ref[t]
        pool_o[t] = _dot(m.astype(BF16), poolw_ref[...]) * pscale_ref[...]

        y = jnp.zeros(shape, F32)
        for j in range(CONV_WIDTH):
            y = y + conv_row(t + j) * convw_ref[j:j + 1, :]
        y = _layernorm_silu(y + convb_ref[...], lng_ref[...], lnb_ref[...])
        conv_o[t] = _dot(y.astype(BF16), convpw_ref[...])

        s = jnp.broadcast_to(sb_ref[t:t + 1, :], shape)
        for j in range(t + 1):
            s = s + sw_ref[t, j:j + 1, :] * vg_ref[j]
        gmlp_o[t] = u_ref[t] * s


def _mix_sample(hp, p, hc, a, u, vg, poolw, pscale, convw, convb, lng, lnb, convpw, sw, sb):
    t, db, g = p.shape

    def full(arr):
        nd = arr.ndim
        return pl.BlockSpec(arr.shape, lambda i: (0,) * nd)

    args = (hp, p, hc, a, u, vg, poolw, pscale, convw, convb, lng, lnb, convpw, sw, sb)
    out = jax.ShapeDtypeStruct((t, db, g), F32)
    return pl.pallas_call(
        functools.partial(_mix_sample_kernel, t_new=t),
        grid=(1,),
        in_specs=[full(x) for x in args],
        out_specs=[full(out)] * 3,
        out_shape=[out] * 3,
        compiler_params=_cparams(("arbitrary",)),
        name="mix_sample",
    )(*args)


def _row_tile(n, cap):
    t = min(n, cap)
    while n % t:
        t //= 2
    return t


def _ffn_tile(f):
    for tf in (512, 256, 128):
        if f % tf == 0:
            return tf
    return f


def _layer_weights(l, ffn1_norm, ffn1_w_gate, ffn1_w_up, ffn1_w_down, mix_norm, w_in, q_norm, k_norm,
                   pool_w, pool_scale, conv_w, conv_b, conv_ln_g, conv_ln_b, conv_pw,
                   gate_norm, spatial_w, spatial_b, w_out, ffn2_norm, ffn2_w_gate, ffn2_w_up, ffn2_w_down):
    row = lambda v: v[l].reshape(1, -1)
    head_id = jnp.arange(GROUP) // HEAD_DIM
    seg = jnp.where(head_id[:, None] == head_id[None, :], 1.0 / HEAD_DIM, 0.0).astype(BF16)
    n_win = len(POOL_WINDOWS)
    pg = GROUP // n_win
    pool_bd = jnp.zeros((GROUP, GROUP), F32)
    for gi in range(n_win):
        pool_bd = pool_bd.at[gi * pg:(gi + 1) * pg, gi * pg:(gi + 1) * pg].set(pool_w[l, gi])
    sb_lanes = jnp.repeat(spatial_b[l].T, HEAD_DIM, axis=1)
    return dict(
        layer=l,
        ffn1=(row(ffn1_norm), ffn1_w_gate, ffn1_w_up, ffn1_w_down),
        ffn2=(row(ffn2_norm), ffn2_w_gate, ffn2_w_up, ffn2_w_down),
        proj_in=(row(mix_norm), w_in[l].astype(BF16), jnp.tile(q_norm[l], N_HEADS).reshape(1, -1),
                 jnp.tile(k_norm[l], N_HEADS).reshape(1, -1), row(gate_norm), seg),
        w_out=w_out[l].astype(BF16),
        mix=(pool_bd.astype(BF16), row(pool_scale), conv_w[l], row(conv_b), row(conv_ln_g), row(conv_ln_b),
             conv_pw[l].astype(BF16)),
        spatial_w=spatial_w[l],
        spatial_b_lanes=sb_lanes,
    )


def _token_stage_in(x, w, kv_stack=None):
    n = x.shape[0]
    x1 = _ffn(x, (), None, *w["ffn1"], layer=w["layer"], tm=_row_tile(n, 1024 if w["layer"] == 0 else 512),
              tf=_ffn_tile(w["ffn1"][1].shape[2]))
    tm = _row_tile(n if kv_stack is None else kv_stack[3], 512)
    return x1, _proj_in(x1, *w["proj_in"], tm=tm, kv_stack=kv_stack)


def _token_stage_out(x1, att, pool, conv, gmlp, w):
    n = x1.shape[0]
    return _ffn(x1, (att, pool, conv, gmlp), w["w_out"], *w["ffn2"], layer=w["layer"], tm=_row_tile(n, 1024),
                tf=_ffn_tile(w["ffn2"][1].shape[2]))


def _prompt_layer(x, w, depth, kv_prev):
    b, t, d = x.shape
    x1, (q, kt, vt, p, a, u, vg) = _token_stage_in(x.reshape(b * t, d), w, (w["layer"], depth, b, t, kv_prev))
    r3 = lambda arr: arr.reshape(b, t, GROUP)
    q, p, a, u, vg = map(r3, (q, p, a, u, vg))
    att = _attn_prompt(q, kt, vt, w["layer"])
    pool, conv, gmlp = _mix_prompt(p, a, u, vg, *w["mix"], w["spatial_w"], w["spatial_b_lanes"])
    flat = lambda arr: arr.reshape(b * t, GROUP)
    y = _token_stage_out(x1, flat(att), flat(pool), flat(conv), flat(gmlp), w)
    return y.reshape(b, t, d), (kt, vt), p[:, -POOL_HIST:], a[:, -CONV_HIST:]


def _sample_layer(x, cache_k, cache_v, page_table, page_base, hist_pool, hist_conv, w):
    db, t, d = x.shape
    x1, (q, k, v, p, a, u, vg) = _token_stage_in(x.reshape(db * t, d), w)
    r3 = lambda arr: arr.reshape(db, t, GROUP)
    q, k, v, p, a, u, vg = map(r3, (q, k, v, p, a, u, vg))
    att = _attn_sample(q, k, v, cache_k, cache_v, page_table, page_base)
    tm = lambda arr: jnp.swapaxes(arr, 0, 1)
    sw_lanes = jnp.repeat(jnp.transpose(w["spatial_w"][:, :t, :t], (1, 2, 0)), HEAD_DIM, axis=2)
    hp_t, p_t, hc_t, a_t = tm(hist_pool), tm(p), tm(hist_conv), tm(a)
    pool, conv, gmlp = _mix_sample(hp_t, p_t, hc_t, a_t, tm(u), tm(vg),
                                   *w["mix"], sw_lanes, w["spatial_b_lanes"][:t])
    flat = lambda arr: tm(arr).reshape(db * t, GROUP)
    y = _token_stage_out(x1, att.reshape(db * t, GROUP), flat(pool), flat(conv), flat(gmlp), w)
    pool_state = tm(jnp.concatenate([hp_t, p_t], axis=0)[-POOL_HIST:])
    conv_state = tm(jnp.concatenate([hc_t, a_t], axis=0)[-CONV_HIST:])
    return y.reshape(db, t, d), k, v, pool_state, conv_state, vg


def kernel(x_prompt, x_sample, cache_k, cache_v, state_pool, state_conv, page_table, ffn1_norm, ffn1_w_gate, ffn1_w_up, ffn1_w_down, mix_norm, w_in, q_norm, k_norm, pool_w, pool_scale, conv_w, conv_b, conv_ln_g, conv_ln_b, conv_pw, gate_norm, spatial_w, spatial_b, w_out, ffn2_norm, ffn2_w_gate, ffn2_w_up, ffn2_w_down):
    depth, n_pool = cache_k.shape[0], cache_k.shape[1]
    b, t = x_prompt.shape[0], x_prompt.shape[1]
    db, ts = x_sample.shape[0], x_sample.shape[1]
    ck = jnp.transpose(cache_k, (0, 1, 3, 4, 2)).reshape(depth * n_pool, GROUP, PAGE)
    cv = jnp.transpose(cache_v, (0, 1, 3, 4, 2)).reshape(depth * n_pool, GROUP, PAGE)
    hp, hs = x_prompt, x_sample
    outs = [[] for _ in range(7)]
    kv_prompt = None
    for l in range(depth):
        w = _layer_weights(l, ffn1_norm, ffn1_w_gate, ffn1_w_up, ffn1_w_down, mix_norm, w_in, q_norm, k_norm,
                           pool_w, pool_scale, conv_w, conv_b, conv_ln_g, conv_ln_b, conv_pw,
                           gate_norm, spatial_w, spatial_b, w_out, ffn2_norm, ffn2_w_gate, ffn2_w_up, ffn2_w_down)
        hp, kv_prompt, pp, cp = _prompt_layer(hp, w, depth, kv_prompt)
        hs, ks, vs, ps, cs, gs = _sample_layer(hs, ck, cv, page_table, l * n_pool, state_pool[l], state_conv[l], w)
        heads = lambda arr: arr.reshape(db, ts, N_HEADS, HEAD_DIM)
        for lst, val in zip(outs, (heads(ks), heads(vs), pp, ps, cp, cs, gs)):
            lst.append(val)
    k_prompt, v_prompt = (jnp.transpose(s.reshape(depth, b, N_HEADS, HEAD_DIM, t), (0, 1, 4, 2, 3)) for s in kv_prompt)
    return (hp, hs, k_prompt, v_prompt) + tuple(jnp.stack(lst) for lst in outs)
```

```python
import functools

import jax
import jax.numpy as jnp
from jax import lax
from jax.experimental import pallas as pl
from jax.experimental.pallas import tpu as pltpu

F32 = jnp.float32
BF16 = jnp.bfloat16
NEG_INF = float("-inf")

GROUP = 256
HEAD_DIM = 64
N_HEADS = GROUP // HEAD_DIM
MOBA_BLOCK = 256
MOBA_TOP_K = 3
PAGE = 128
POOL_WINDOWS = (2, 4, 8, 16)
POOL_HIST = 15
CONV_WIDTH = 31
CONV_HIST = 30
CHUNK = 128
FFN_ROWS = 512
RMS_EPS = 1e-6
LN_EPS = 1e-5
VMEM_LIMIT = 56 * 1024 * 1024


def _cparams(sem):
    return pltpu.CompilerParams(dimension_semantics=sem, vmem_limit_bytes=VMEM_LIMIT)


def _rms(x, g):
    ms = jnp.mean(x * x, axis=-1, keepdims=True)
    return x * lax.rsqrt(ms + RMS_EPS) * g


def _dot(a, b):
    return jnp.dot(a, b, preferred_element_type=F32)


def _dot_nt(a, b):
    return lax.dot_general(a, b, (((1,), (1,)), ((), ())), preferred_element_type=F32)


def _split(a):
    hi = a.astype(BF16)
    lo = (a - hi.astype(F32)).astype(BF16)
    return hi, lo


def _dot3_nt(a, b):
    ah, al = _split(a)
    bh, bl = _split(b)
    return _dot_nt(ah, bh) + _dot_nt(ah, bl) + _dot_nt(al, bh)


def _lane_group(shape, width):
    return lax.broadcasted_iota(jnp.int32, shape, len(shape) - 1) // width


def _ffn_kernel(*refs, n_f, tf, n_mix, layer):
    x_ref = refs[0]
    mix_refs = refs[1:1 + n_mix]
    rest = refs[1 + n_mix:]
    if n_mix:
        wo_ref, rest = rest[0], rest[1:]
    g_ref, wg_hbm, wu_hbm, wd_hbm, o_ref, wg_b, wu_b, wd_b, gu_stage, d_stage, sems, acc_scr = rest
    first = pl.program_id(0) == 0

    def chunk_copies(c):
        slot = c % 2
        cols = pl.ds(c * tf, tf)
        return (pltpu.make_async_copy(wg_hbm.at[layer, :, cols], gu_stage.at[slot, 0], sems.at[slot, 0]),
                pltpu.make_async_copy(wu_hbm.at[layer, :, cols], gu_stage.at[slot, 1], sems.at[slot, 1]),
                pltpu.make_async_copy(wd_hbm.at[layer, cols, :], d_stage.at[slot], sems.at[slot, 2]))

    @pl.when(first)
    def _():
        for cp in chunk_copies(0):
            cp.start()

    x = x_ref[...]
    for idx, m_ref in enumerate(mix_refs):
        x = x + _dot(m_ref[...].astype(BF16), wo_ref[idx * GROUP:(idx + 1) * GROUP, :])
    h = _rms(x, g_ref[...]).astype(BF16)

    def land_chunk(c):
        cols = slice(c * tf, (c + 1) * tf)
        if c + 1 < n_f:
            for cp in chunk_copies(c + 1):
                cp.start()
        for cp in chunk_copies(c):
            cp.wait()
        wg_b[:, cols] = gu_stage[c % 2, 0].astype(BF16)
        wu_b[:, cols] = gu_stage[c % 2, 1].astype(BF16)
        wd_b[cols, :] = d_stage[c % 2].astype(BF16)

    def swiglu(fetch):
        for c in range(n_f):
            cols = slice(c * tf, (c + 1) * tf)
            fetch(c)
            gate = _dot(h, wg_b[:, cols])
            up = _dot(h, wu_b[:, cols])
            act = (gate * jax.nn.sigmoid(gate) * up).astype(BF16)
            part = _dot(act, wd_b[cols, :])
            if c == 0:
                acc_scr[...] = part
            else:
                acc_scr[...] += part
        o_ref[...] = x + 0.5 * acc_scr[...]

    @pl.when(first)
    def _():
        swiglu(land_chunk)

    @pl.when(jnp.logical_not(first))
    def _():
        swiglu(lambda c: None)


def _ffn(x, mixes, w_out, norm, wg, wu, wd, layer, tm, tf):
    n, d = x.shape
    f = wg.shape[2]
    n_mix = len(mixes)
    row = lambda i: (i, 0)
    hbm = pl.BlockSpec(memory_space=pl.ANY)
    in_specs = [pl.BlockSpec((tm, d), row)] + [pl.BlockSpec((tm, GROUP), row)] * n_mix
    args = [x, *mixes]
    if n_mix:
        in_specs.append(pl.BlockSpec(w_out.shape, lambda i: (0, 0)))
        args.append(w_out)
    in_specs += [pl.BlockSpec((1, d), lambda i: (0, 0)), hbm, hbm, hbm]
    args += [norm, wg, wu, wd]
    return pl.pallas_call(
        functools.partial(_ffn_kernel, n_f=f // tf, tf=tf, n_mix=n_mix, layer=layer),
        grid=(n // tm,),
        in_specs=in_specs,
        out_specs=pl.BlockSpec((tm, d), row),
        out_shape=jax.ShapeDtypeStruct((n, d), F32),
        scratch_shapes=[pltpu.VMEM((d, f), BF16), pltpu.VMEM((d, f), BF16), pltpu.VMEM((f, d), BF16),
                        pltpu.VMEM((2, 2, d, tf), F32), pltpu.VMEM((2, tf, d), F32),
                        pltpu.SemaphoreType.DMA((2, 3)), pltpu.VMEM((tm, d), F32)],
        compiler_params=_cparams(("arbitrary",)),
        name="ffn_out" if n_mix else "ffn",
    )(*args)


def _proj_in_kernel(*refs, kv_transposed, n_alias):
    x_ref, g_ref, w_ref, qn_ref, kn_ref, gn_ref, seg_ref = refs[:7]
    q_o, k_o, v_o, p_o, a_o, u_o, vg_o = refs[7 + n_alias:]
    G = GROUP
    h = _rms(x_ref[...], g_ref[...]).astype(BF16)
    z = _dot(h, w_ref[...])
    seg = seg_ref[...]

    def head_rms(t, gain):
        sq = t * t
        hi, lo = _split(sq)
        ms = _dot(hi, seg) + _dot(lo, seg)
        return t * lax.rsqrt(ms + RMS_EPS) * gain

    q_o[...] = head_rms(z[:, 0:G], qn_ref[...])
    k = head_rms(z[:, G:2 * G], kn_ref[...])
    v = z[:, 2 * G:3 * G]
    if not kv_transposed:
        k_o[...] = k
        v_o[...] = v
    elif len(k_o.shape) == 2:
        k_o[...] = k.T
        v_o[...] = v.T
    else:
        k_o[0] = k.T
        v_o[0] = v.T
        for slot in range(1, k_o.shape[0]):
            k_o[slot] = jnp.zeros(k_o.shape[1:], F32)
            v_o[slot] = jnp.zeros(v_o.shape[1:], F32)
    p_o[...] = z[:, 3 * G:4 * G]
    a_o[...] = z[:, 4 * G:5 * G] * jax.nn.sigmoid(z[:, 5 * G:6 * G])
    d = jax.nn.gelu(z[:, 6 * G:8 * G])
    u_o[...] = d[:, 0:G]
    vg_o[...] = _rms(d[:, G:2 * G], gn_ref[...])


def _proj_in(x, norm, w_in, qn, kn, gn, seg, tm, kv_stack=None):
    n, d = x.shape
    cols = w_in.shape[1]
    row = lambda i: (i, 0)
    fixed = lambda i: (0, 0)
    out = jax.ShapeDtypeStruct((n, GROUP), F32)
    in_specs = [pl.BlockSpec((tm, d), row),
                pl.BlockSpec((1, d), fixed),
                pl.BlockSpec((d, cols), fixed),
                pl.BlockSpec((1, GROUP), fixed),
                pl.BlockSpec((1, GROUP), fixed),
                pl.BlockSpec((1, GROUP), fixed),
                pl.BlockSpec((GROUP, GROUP), fixed)]
    args = [x, norm, w_in, qn, kn, gn, seg]
    out_specs = [pl.BlockSpec((tm, GROUP), row)] * 7
    out_shape = [out] * 7
    aliases = {}
    if kv_stack is not None:
        layer, depth, b, t, prev = kv_stack
        per_b = t // tm
        if prev is None:
            assert layer == 0
            kv_spec = pl.BlockSpec((depth, None, GROUP, tm), lambda i: (0, i // per_b, 0, i % per_b))
        else:
            kv_spec = pl.BlockSpec((None, None, GROUP, tm), lambda i: (layer, i // per_b, 0, i % per_b))
        out_specs[1:3] = [kv_spec, kv_spec]
        out_shape[1:3] = [jax.ShapeDtypeStruct((depth, b, GROUP, t), F32)] * 2
        if prev is not None:
            in_specs += [pl.BlockSpec(memory_space=pl.ANY)] * 2
            aliases = {len(args): 1, len(args) + 1: 2}
            args += list(prev)
    return pl.pallas_call(
        functools.partial(_proj_in_kernel, kv_transposed=kv_stack is not None, n_alias=len(aliases)),
        grid=(n // tm,),
        in_specs=in_specs,
        out_specs=out_specs,
        out_shape=out_shape,
        input_output_aliases=aliases,
        compiler_params=_cparams(("parallel",)),
        name="proj_in",
    )(*args)


GATE_ROWS = 8
VT_ROWS = HEAD_DIM + 16


def _select_bias_rows(gate, n_valid):
    rows = lax.broadcasted_iota(jnp.int32, gate.shape, 0)
    gm = jnp.where(rows < n_valid, gate, NEG_INF)
    bias = jnp.where(rows == n_valid, 0.0, NEG_INF)
    for n in range(gate.shape[0]):
        gn = gm[n:n + 1, :]
        beats = (gm > gn) | ((gm == gn) & (rows < n))
        rank = jnp.sum(jnp.where(beats, 1.0, 0.0), axis=0, keepdims=True)
        bias = jnp.where((rows == n) & (rows < n_valid) & (rank < MOBA_TOP_K), 0.0, bias)
    return bias


def _attn_prompt_kernel(q_ref, k_ref, v_ref, o_ref, kb_scr, vt_scr, kbd_scr, bias_scr, *, n_blk):
    i = pl.program_id(1)
    B = MOBA_BLOCK
    lane_head = _lane_group((1, GROUP), HEAD_DIM)

    @pl.when(i == 0)
    def _():
        kbd_scr[...] = jnp.zeros_like(kbd_scr)
        ones = jnp.ones((VT_ROWS - HEAD_DIM, B), BF16)
        for n in range(n_blk):
            kn = k_ref[:, n * B:(n + 1) * B].T
            kb_scr[n] = kn.astype(BF16)
            mean = jnp.mean(kn, axis=0, keepdims=True)
            vt = v_ref[:, n * B:(n + 1) * B]
            for h in range(N_HEADS):
                r = h * GATE_ROWS + n
                kbd_scr[r:r + 1, :] = jnp.where(lane_head == h, mean, 0.0)
                vt_scr[n, h, 0:HEAD_DIM, :] = vt[h * HEAD_DIM:(h + 1) * HEAD_DIM, :].astype(BF16)
                vt_scr[n, h, HEAD_DIM:VT_ROWS, :] = ones

    q = q_ref[0]
    gate = _dot3_nt(kbd_scr[...], q)
    for h in range(N_HEADS):
        rows = slice(h * GATE_ROWS, (h + 1) * GATE_ROWS)
        bias_scr[rows, :] = _select_bias_rows(gate[rows, :], i)

    neg_rel = (lax.broadcasted_iota(jnp.int32, (B, B), 0) - lax.broadcasted_iota(jnp.int32, (B, B), 1)).astype(F32)
    slopes = [2.0 ** (-8.0 * (h + 1) / N_HEADS) for h in range(N_HEADS)]
    q_scaled = q * (HEAD_DIM ** -0.5)
    in_head = _lane_group((B, GROUP), HEAD_DIM)
    qh = [jnp.where(in_head == h, q_scaled, 0.0).astype(BF16) for h in range(N_HEADS)]

    alibi = [slopes[h] * neg_rel for h in range(N_HEADS)]
    alibi_causal = [jnp.where(neg_rel <= 0.0, a, NEG_INF) for a in alibi]

    def scores(blk):
        kn = kb_scr[blk]
        return tuple(_dot_nt(kn, qh[h]) for h in range(N_HEADS))

    def absorb(blk, qk, ms, accs):
        is_own = blk == i
        dist = ((i - blk) * B).astype(F32)
        new_ms, new_accs = [], []
        for h in range(N_HEADS):
            bias = bias_scr[pl.ds(h * GATE_ROWS + blk, 1), :] - slopes[h] * dist
            s = qk[h] + jnp.where(is_own, alibi_causal[h], alibi[h]) + bias
            m_new = jnp.maximum(ms[h], jnp.max(s, axis=0, keepdims=True))
            alpha = jnp.exp(ms[h] - m_new)
            p = jnp.exp(s - m_new)
            new_ms.append(m_new)
            new_accs.append(alpha * accs[h] + _dot(vt_scr[blk, h], p.astype(BF16)))
        return tuple(new_ms), tuple(new_accs)

    def body(j, carry):
        qk, ms, accs = carry
        qk_next = scores(j)
        ms, accs = absorb(jnp.where(j == 0, i, j - 1), qk, ms, accs)
        return qk_next, ms, accs

    init = (scores(i), (jnp.full((1, B), NEG_INF, F32),) * N_HEADS, (jnp.zeros((VT_ROWS, B), F32),) * N_HEADS)
    qk, ms, accs = lax.fori_loop(0, i, body, init)
    ms, accs = absorb(jnp.where(i == 0, i, i - 1), qk, ms, accs)
    out_t = jnp.concatenate([a[0:HEAD_DIM, :] / a[HEAD_DIM:HEAD_DIM + 1, :] for a in accs], axis=0)
    o_ref[0] = out_t.T


def _attn_prompt(q, kt, vt, layer):
    b, t, g = q.shape
    n_blk = t // MOBA_BLOCK
    assert t % MOBA_BLOCK == 0 and n_blk <= GATE_ROWS
    kv_spec = pl.BlockSpec((None, None, g, t), lambda bi, i: (layer, bi, 0, 0))
    return pl.pallas_call(
        functools.partial(_attn_prompt_kernel, n_blk=n_blk),
        grid=(b, n_blk),
        in_specs=[pl.BlockSpec((1, MOBA_BLOCK, g), lambda bi, i: (bi, i, 0)), kv_spec, kv_spec],
        out_specs=pl.BlockSpec((1, MOBA_BLOCK, g), lambda bi, i: (bi, i, 0)),
        out_shape=jax.ShapeDtypeStruct((b, t, g), F32),
        scratch_shapes=[pltpu.VMEM((n_blk, MOBA_BLOCK, g), BF16),
                        pltpu.VMEM((n_blk, N_HEADS, VT_ROWS, MOBA_BLOCK), BF16),
                        pltpu.VMEM((N_HEADS * GATE_ROWS, g), F32),
                        pltpu.VMEM((N_HEADS * GATE_ROWS, MOBA_BLOCK), F32)],
        compiler_params=_cparams(("parallel", "arbitrary")),
        name="attn_prompt",
    )(q, kt, vt)


def _attn_sample_kernel(pt_ref, q_ref, kn_ref, vn_ref, ck_hbm, cv_hbm, o_ref, k_buf, v_buf, sems,
                        *, n_pages, t_new, page_base):
    b = pl.program_id(0)

    def page_copies(seq, slot):
        out = []
        for pg in range(n_pages):
            page = page_base + pt_ref[seq, pg]
            out.append(pltpu.make_async_copy(ck_hbm.at[page], k_buf.at[slot, pg], sems.at[0, slot]))
            out.append(pltpu.make_async_copy(cv_hbm.at[page], v_buf.at[slot, pg], sems.at[1, slot]))
        return out

    @pl.when(b == 0)
    def _():
        for cp in page_copies(0, 0):
            cp.start()

    @pl.when(b + 1 < pl.num_programs(0))
    def _():
        for cp in page_copies(b + 1, (b + 1) % 2):
            cp.start()

    slot = b % 2
    for cp in page_copies(b, slot):
        cp.wait()
    k_pages = [k_buf.at[slot, pg] for pg in range(n_pages)]
    v_pages = [v_buf.at[slot, pg] for pg in range(n_pages)]
    T = t_new
    R = N_HEADS * T
    n_blk = n_pages * PAGE // MOBA_BLOCK
    past = n_pages * PAGE
    scale = HEAD_DIM ** -0.5

    q = q_ref[0] * scale
    q_rows = jnp.concatenate([q] * N_HEADS, axis=0)
    row_head = lax.broadcasted_iota(jnp.int32, (R, GROUP), 0) // T
    qbd = jnp.where(_lane_group((R, GROUP), HEAD_DIM) == row_head, q_rows, 0.0)
    q_hi, q_lo = _split(qbd)

    raw = []
    for pg in range(n_pages):
        k_hi, k_lo = _split(k_pages[pg][...])
        raw.append(_dot(q_hi, k_hi) + _dot(q_hi, k_lo) + _dot(q_lo, k_hi))
    per_blk = MOBA_BLOCK // PAGE
    gates = [jnp.sum(sum(raw[n * per_blk:(n + 1) * per_blk]), axis=-1, keepdims=True) for n in range(n_blk)]
    biases = []
    for n in range(n_blk):
        rank = jnp.zeros((R, 1), F32)
        for mth in range(n_blk):
            if mth != n:
                beats = (gates[mth] >= gates[n]) if mth < n else (gates[mth] > gates[n])
                rank = rank + jnp.where(beats, 1.0, 0.0)
        biases.append(jnp.where(rank < MOBA_TOP_K, 0.0, NEG_INF))

    r1 = lax.broadcasted_iota(jnp.int32, (R, 1), 0)
    slope = jnp.zeros((R, 1), F32)
    for h in range(N_HEADS):
        slope = jnp.where(r1 // T == h, 2.0 ** (-8.0 * (h + 1) / N_HEADS), slope)
    t_row = r1 % T
    pos_q = (past + t_row).astype(F32)
    col = lax.broadcasted_iota(jnp.int32, (R, PAGE), 1)

    scores = []
    for pg in range(n_pages):
        key_pos = (col + pg * PAGE).astype(F32)
        scores.append(raw[pg] - slope * (pos_q - key_pos) + biases[pg // per_blk])
    pad = jnp.zeros((PAGE - T, GROUP), F32)
    k_new = jnp.concatenate([kn_ref[0], pad], axis=0).astype(BF16)
    v_new = jnp.concatenate([vn_ref[0], pad], axis=0).astype(BF16)
    s_new = _dot_nt(q_hi, k_new) - slope * (t_row - col).astype(F32)
    s_new = jnp.where(col <= t_row, s_new, NEG_INF)
    scores.append(s_new)

    m = functools.reduce(jnp.maximum, [jnp.max(s, axis=-1, keepdims=True) for s in scores])
    l = jnp.zeros((R, 1), F32)
    acc = jnp.zeros((R, GROUP), F32)
    for pg in range(n_pages + 1):
        p = jnp.exp(scores[pg] - m)
        l = l + jnp.sum(p, axis=-1, keepdims=True)
        if pg == n_pages:
            acc = acc + _dot(p.astype(BF16), v_new)
        else:
            acc = acc + _dot_nt(p.astype(BF16), v_pages[pg][...].astype(BF16))
    acc = acc / l

    lane_head = _lane_group((T, GROUP), HEAD_DIM)
    out = jnp.zeros((T, GROUP), F32)
    for h in range(N_HEADS):
        out = jnp.where(lane_head == h, acc[h * T:(h + 1) * T, :], out)
    o_ref[0] = out


def _attn_sample(q, k_new, v_new, cache_k, cache_v, page_table, page_base):
    db, t, g = q.shape
    n_pages = page_table.shape[1]
    tok = pl.BlockSpec((1, t, g), lambda b, pt: (b, 0, 0))
    hbm = pl.BlockSpec(memory_space=pl.ANY)
    page_buf = pltpu.VMEM((2, n_pages, g, PAGE), F32)
    return pl.pallas_call(
        functools.partial(_attn_sample_kernel, n_pages=n_pages, t_new=t, page_base=page_base),
        grid_spec=pltpu.PrefetchScalarGridSpec(
            num_scalar_prefetch=1,
            grid=(db,),
            in_specs=[tok, tok, tok, hbm, hbm],
            out_specs=pl.BlockSpec((1, t, g), lambda b, pt: (b, 0, 0)),
            scratch_shapes=[page_buf, page_buf, pltpu.SemaphoreType.DMA((2, 2))],
        ),
        out_shape=jax.ShapeDtypeStruct((db, t, g), F32),
        compiler_params=_cparams(("arbitrary",)),
        name="attn_sample",
    )(page_table, q, k_new, v_new, cache_k, cache_v)


def _layernorm_silu(y, g, b):
    mu = jnp.mean(y, axis=-1, keepdims=True)
    yc = y - mu
    var = jnp.mean(yc * yc, axis=-1, keepdims=True)
    yn = yc * lax.rsqrt(var + LN_EPS) * g + b
    return yn * jax.nn.sigmoid(yn)


def _pool_window_select(sums, shape):
    grp = _lane_group(shape, GROUP // len(POOL_WINDOWS))
    sel = sums[POOL_WINDOWS[-1]]
    for gi in range(len(POOL_WINDOWS) - 2, -1, -1):
        sel = jnp.where(grp == gi, sums[POOL_WINDOWS[gi]], sel)
    win = jnp.left_shift(jnp.int32(POOL_WINDOWS[0]), grp)
    return sel, win


MIX_ROWS = 256
HALO = 32


def _mix_prompt_kernel(pp_ref, p_ref, ap_ref, a_ref, u_ref, vg_ref, poolw_ref, pscale_ref,
                       convw_ref, convb_ref, lng_ref, lnb_ref, convpw_ref, sw_ref, sb_ref,
                       pool_o, conv_o, gmlp_o, p_scr, a_scr):
    c = pl.program_id(1)
    C = CHUNK
    has_prev = c > 0
    p_scr[0:HALO, :] = jnp.where(has_prev, pp_ref[0], 0.0)
    p_scr[HALO:HALO + MIX_ROWS, :] = p_ref[0]
    a_scr[0:HALO, :] = jnp.where(has_prev, ap_ref[0], 0.0)
    a_scr[HALO:HALO + MIX_ROWS, :] = a_ref[0]
    tri = lax.broadcasted_iota(jnp.int32, (C, C), 0) >= lax.broadcasted_iota(jnp.int32, (C, C), 1)
    w_causal = [jnp.where(tri, sw_ref[g], 0.0).astype(BF16) for g in range(N_HEADS)]
    grp = _lane_group((C, GROUP), HEAD_DIM)

    for sub in range(MIX_ROWS // C):
        s0 = HALO + sub * C
        rows = slice(sub * C, (sub + 1) * C)

        w2 = p_scr[s0 - 24:s0 + C, :] + p_scr[s0 - 25:s0 + C - 1, :]
        w4 = w2[8:C + 24] + w2[6:C + 22]
        w8 = w4[8:C + 16] + w4[4:C + 12]
        w16 = w8[8:C + 8] + w8[0:C]
        sums = {2: w2[24:C + 24], 4: w4[16:C + 16], 8: w8[8:C + 8], 16: w16}
        sel, win = _pool_window_select(sums, (C, GROUP))
        t_glob = c * MIX_ROWS + sub * C + lax.broadcasted_iota(jnp.int32, (C, GROUP), 0)
        count = jnp.minimum(t_glob + 1, win).astype(F32)
        m = sel / count - p_scr[s0:s0 + C, :]
        pool_o[0, rows, :] = _dot(m.astype(BF16), poolw_ref[...]) * pscale_ref[...]

        base = s0 - CONV_HIST
        y = jnp.zeros((C, GROUP), F32)
        for r in range(8):
            taps = [j for j in range(CONV_WIDTH) if (base + j) % 8 == r]
            n_rows = C + (8 if r else 0)
            z = jnp.zeros((n_rows, GROUP), F32)
            for j in taps:
                start = base + j - r
                z = z + a_scr[start:start + n_rows, :] * convw_ref[j:j + 1, :]
            y = y + z[r:r + C]
        y = _layernorm_silu(y + convb_ref[...], lng_ref[...], lnb_ref[...])
        conv_o[0, rows, :] = _dot(y.astype(BF16), convpw_ref[...])

        vgb = vg_ref[0, rows, :].astype(BF16)
        s = sb_ref[...]
        for g in range(N_HEADS):
            s = s + jnp.where(grp == g, _dot(w_causal[g], vgb), 0.0)
        gmlp_o[0, rows, :] = u_ref[0, rows, :] * s


def _mix_prompt(p, a, u, vg, poolw, pscale, convw, convb, lng, lnb, convpw, sw, sb):
    b, t, g = p.shape
    assert POOL_WINDOWS == (2, 4, 8, 16) and t % MIX_ROWS == 0 and MIX_ROWS % CHUNK == 0 and HALO >= CONV_HIST
    per = MIX_ROWS // HALO
    cur = pl.BlockSpec((1, MIX_ROWS, g), lambda bi, c: (bi, c, 0))
    prev = pl.BlockSpec((1, HALO, g), lambda bi, c: (bi, jnp.maximum(c * per - 1, 0), 0))

    def full(arr):
        nd = arr.ndim
        return pl.BlockSpec(arr.shape, lambda bi, c: (0,) * nd)

    out = jax.ShapeDtypeStruct((b, t, g), F32)
    weights = (poolw, pscale, convw, convb, lng, lnb, convpw, sw, sb)
    scratch = pltpu.VMEM((HALO + MIX_ROWS, g), F32)
    return pl.pallas_call(
        _mix_prompt_kernel,
        grid=(b, t // MIX_ROWS),
        in_specs=[prev, cur, prev, cur, cur, cur] + [full(w) for w in weights],
        out_specs=[cur] * 3,
        out_shape=[out] * 3,
        scratch_shapes=[scratch, scratch],
        compiler_params=_cparams(("parallel", "arbitrary")),
        name="mix_prompt",
    )(p, p, a, a, u, vg, *weights)


def _mix_sample_kernel(hp_ref, p_ref, hc_ref, a_ref, u_ref, vg_ref, poolw_ref, pscale_ref,
                       convw_ref, convb_ref, lng_ref, lnb_ref, convpw_ref, sw_ref, sb_ref,
                       pool_o, conv_o, gmlp_o, *, t_new):
    T = t_new
    nb = p_ref.shape[1]
    shape = (nb, GROUP)

    def pool_row(k):
        return hp_ref[k] if k < POOL_HIST else p_ref[k - POOL_HIST]

    def conv_row(k):
        return hc_ref[k] if k < CONV_HIST else a_ref[k - CONV_HIST]

    for t in range(T):
        acc = jnp.zeros(shape, F32)
        sums = {}
        for i in range(POOL_WINDOWS[-1]):
            k = POOL_HIST + t - i
            if k >= 0:
                acc = acc + pool_row(k)
            if i + 1 in POOL_WINDOWS:
                sums[i + 1] = acc
        sel, win = _pool_window_select(sums, shape)
        count = jnp.minimum(POOL_HIST + t + 1, win).astype(F32)
        m = sel / count - p_ref[t]
        pool_o[t] = _dot(m.astype(BF16), poolw_ref[...]) * pscale_ref[...]

        y = jnp.zeros(shape, F32)
        for j in range(CONV_WIDTH):
            y = y + conv_row(t + j) * convw_ref[j:j + 1, :]
        y = _layernorm_silu(y + convb_ref[...], lng_ref[...], lnb_ref[...])
        conv_o[t] = _dot(y.astype(BF16), convpw_ref[...])

        s = jnp.broadcast_to(sb_ref[t:t + 1, :], shape)
        for j in range(t + 1):
            s = s + sw_ref[t, j:j + 1, :] * vg_ref[j]
        gmlp_o[t] = u_ref[t] * s


def _mix_sample(hp, p, hc, a, u, vg, poolw, pscale, convw, convb, lng, lnb, convpw, sw, sb):
    t, db, g = p.shape

    def full(arr):
        nd = arr.ndim
        return pl.BlockSpec(arr.shape, lambda i: (0,) * nd)

    args = (hp, p, hc, a, u, vg, poolw, pscale, convw, convb, lng, lnb, convpw, sw, sb)
    out = jax.ShapeDtypeStruct((t, db, g), F32)
    return pl.pallas_call(
        functools.partial(_mix_sample_kernel, t_new=t),
        grid=(1,),
        in_specs=[full(x) for x in args],
        out_specs=[full(out)] * 3,
        out_shape=[out] * 3,
        compiler_params=_cparams(("arbitrary",)),
        name="mix_sample",
    )(*args)


def _row_tile(n, cap):
    t = min(n, cap)
    while n % t:
        t //= 2
    return t


def _ffn_tile(f):
    for tf in (512, 256, 128):
        if f % tf == 0:
            return tf
    return f


def _layer_weights(l, ffn1_norm, ffn1_w_gate, ffn1_w_up, ffn1_w_down, mix_norm, w_in, q_norm, k_norm,
                   pool_w, pool_scale, conv_w, conv_b, conv_ln_g, conv_ln_b, conv_pw,
                   gate_norm, spatial_w, spatial_b, w_out, ffn2_norm, ffn2_w_gate, ffn2_w_up, ffn2_w_down):
    row = lambda v: v[l].reshape(1, -1)
    head_id = jnp.arange(GROUP) // HEAD_DIM
    seg = jnp.where(head_id[:, None] == head_id[None, :], 1.0 / HEAD_DIM, 0.0).astype(BF16)
    n_win = len(POOL_WINDOWS)
    pg = GROUP // n_win
    pool_bd = jnp.zeros((GROUP, GROUP), F32)
    for gi in range(n_win):
        pool_bd = pool_bd.at[gi * pg:(gi + 1) * pg, gi * pg:(gi + 1) * pg].set(pool_w[l, gi])
    sb_lanes = jnp.repeat(spatial_b[l].T, HEAD_DIM, axis=1)
    return dict(
        layer=l,
        ffn1=(row(ffn1_norm), ffn1_w_gate, ffn1_w_up, ffn1_w_down),
        ffn2=(row(ffn2_norm), ffn2_w_gate, ffn2_w_up, ffn2_w_down),
        proj_in=(row(mix_norm), w_in[l].astype(BF16), jnp.tile(q_norm[l], N_HEADS).reshape(1, -1),
                 jnp.tile(k_norm[l], N_HEADS).reshape(1, -1), row(gate_norm), seg),
        w_out=w_out[l].astype(BF16),
        mix=(pool_bd.astype(BF16), row(pool_scale), conv_w[l], row(conv_b), row(conv_ln_g), row(conv_ln_b),
             conv_pw[l].astype(BF16)),
        spatial_w=spatial_w[l],
        spatial_b_lanes=sb_lanes,
    )


def _token_stage_in(x, w, kv_stack=None):
    n = x.shape[0]
    x1 = _ffn(x, (), None, *w["ffn1"], layer=w["layer"], tm=_row_tile(n, FFN_ROWS),
              tf=_ffn_tile(w["ffn1"][1].shape[2]))
    tm = _row_tile(n if kv_stack is None else kv_stack[3], 512)
    return x1, _proj_in(x1, *w["proj_in"], tm=tm, kv_stack=kv_stack)


def _token_stage_out(x1, att, pool, conv, gmlp, w):
    n = x1.shape[0]
    return _ffn(x1, (att, pool, conv, gmlp), w["w_out"], *w["ffn2"], layer=w["layer"], tm=_row_tile(n, FFN_ROWS),
                tf=_ffn_tile(w["ffn2"][1].shape[2]))


def _prompt_layer(x, w, depth, kv_prev):
    b, t, d = x.shape
    x1, (q, kt, vt, p, a, u, vg) = _token_stage_in(x.reshape(b * t, d), w, (w["layer"], depth, b, t, kv_prev))
    r3 = lambda arr: arr.reshape(b, t, GROUP)
    q, p, a, u, vg = map(r3, (q, p, a, u, vg))
    att = _attn_prompt(q, kt, vt, w["layer"])
    pool, conv, gmlp = _mix_prompt(p, a, u, vg, *w["mix"], w["spatial_w"], w["spatial_b_lanes"])
    flat = lambda arr: arr.reshape(b * t, GROUP)
    y = _token_stage_out(x1, flat(att), flat(pool), flat(conv), flat(gmlp), w)
    return y.reshape(b, t, d), (kt, vt), p[:, -POOL_HIST:], a[:, -CONV_HIST:]


def _sample_layer(x, cache_k, cache_v, page_table, page_base, hist_pool, hist_conv, w):
    db, t, d = x.shape
    x1, (q, k, v, p, a, u, vg) = _token_stage_in(x.reshape(db * t, d), w)
    r3 = lambda arr: arr.reshape(db, t, GROUP)
    q, k, v, p, a, u, vg = map(r3, (q, k, v, p, a, u, vg))
    att = _attn_sample(q, k, v, cache_k, cache_v, page_table, page_base)
    tm = lambda arr: jnp.swapaxes(arr, 0, 1)
    sw_lanes = jnp.repeat(jnp.transpose(w["spatial_w"][:, :t, :t], (1, 2, 0)), HEAD_DIM, axis=2)
    hp_t, p_t, hc_t, a_t = tm(hist_pool), tm(p), tm(hist_conv), tm(a)
    pool, conv, gmlp = _mix_sample(hp_t, p_t, hc_t, a_t, tm(u), tm(vg),
                                   *w["mix"], sw_lanes, w["spatial_b_lanes"][:t])
    flat = lambda arr: tm(arr).reshape(db * t, GROUP)
    y = _token_stage_out(x1, att.reshape(db * t, GROUP), flat(pool), flat(conv), flat(gmlp), w)
    pool_state = tm(jnp.concatenate([hp_t, p_t], axis=0)[-POOL_HIST:])
    conv_state = tm(jnp.concatenate([hc_t, a_t], axis=0)[-CONV_HIST:])
    return y.reshape(db, t, d), k, v, pool_state, conv_state, vg


def kernel(x_prompt, x_sample, cache_k, cache_v, state_pool, state_conv, page_table, ffn1_norm, ffn1_w_gate, ffn1_w_up, ffn1_w_down, mix_norm, w_in, q_norm, k_norm, pool_w, pool_scale, conv_w, conv_b, conv_ln_g, conv_ln_b, conv_pw, gate_norm, spatial_w, spatial_b, w_out, ffn2_norm, ffn2_w_gate, ffn2_w_up, ffn2_w_down):
    depth, n_pool = cache_k.shape[0], cache_k.shape[1]
    b, t = x_prompt.shape[0], x_prompt.shape[1]
    db, ts = x_sample.shape[0], x_sample.shape[1]
    ck = jnp.transpose(cache_k, (0, 1, 3, 4, 2)).reshape(depth * n_pool, GROUP, PAGE)
    cv = jnp.transpose(cache_v, (0, 1, 3, 4, 2)).reshape(depth * n_pool, GROUP, PAGE)
    hp, hs = x_prompt, x_sample
    outs = [[] for _ in range(7)]
    kv_prompt = None
    for l in range(depth):
        w = _layer_weights(l, ffn1_norm, ffn1_w_gate, ffn1_w_up, ffn1_w_down, mix_norm, w_in, q_norm, k_norm,
                           pool_w, pool_scale, conv_w, conv_b, conv_ln_g, conv_ln_b, conv_pw,
                           gate_norm, spatial_w, spatial_b, w_out, ffn2_norm, ffn2_w_gate, ffn2_w_up, ffn2_w_down)
        hp, kv_prompt, pp, cp = _prompt_layer(hp, w, depth, kv_prompt)
        hs, ks, vs, ps, cs, gs = _sample_layer(hs, ck, cv, page_table, l * n_pool, state_pool[l], state_conv[l], w)
        heads = lambda arr: arr.reshape(db, ts, N_HEADS, HEAD_DIM)
        for lst, val in zip(outs, (heads(ks), heads(vs), pp, ps, cp, cs, gs)):
            lst.append(val)
    k_prompt, v_prompt = (jnp.transpose(s.reshape(depth, b, N_HEADS, HEAD_DIM, t), (0, 1, 4, 2, 3)) for s in kv_prompt)
    return (hp, hs, k_prompt, v_prompt) + tuple(jnp.stack(lst) for lst in outs)
```

```python
import functools

import jax
import jax.numpy as jnp
from jax import lax
from jax.experimental import pallas as pl
from jax.experimental.pallas import tpu as pltpu

F32 = jnp.float32
BF16 = jnp.bfloat16
NEG_INF = float("-inf")

GROUP = 256
HEAD_DIM = 64
N_HEADS = GROUP // HEAD_DIM
MOBA_BLOCK = 256
MOBA_TOP_K = 3
PAGE = 128
POOL_WINDOWS = (2, 4, 8, 16)
POOL_HIST = 15
CONV_WIDTH = 31
CONV_HIST = 30
CHUNK = 128
FFN_ROWS = 512
RMS_EPS = 1e-6
LN_EPS = 1e-5
VMEM_LIMIT = 56 * 1024 * 1024


def _cparams(sem):
    return pltpu.CompilerParams(dimension_semantics=sem, vmem_limit_bytes=VMEM_LIMIT)


def _rms(x, g):
    ms = jnp.mean(x * x, axis=-1, keepdims=True)
    return x * lax.rsqrt(ms + RMS_EPS) * g


def _dot(a, b):
    return jnp.dot(a, b, preferred_element_type=F32)


def _dot_nt(a, b):
    return lax.dot_general(a, b, (((1,), (1,)), ((), ())), preferred_element_type=F32)


def _split(a):
    hi = a.astype(BF16)
    lo = (a - hi.astype(F32)).astype(BF16)
    return hi, lo


def _dot3_nt(a, b):
    ah, al = _split(a)
    bh, bl = _split(b)
    return _dot_nt(ah, bh) + _dot_nt(ah, bl) + _dot_nt(al, bh)


def _lane_group(shape, width):
    return lax.broadcasted_iota(jnp.int32, shape, len(shape) - 1) // width


def _ffn_kernel(*refs, n_f, tf, n_mix, layer):
    x_ref = refs[0]
    mix_refs = refs[1:1 + n_mix]
    rest = refs[1 + n_mix:]
    if n_mix:
        wo_ref, rest = rest[0], rest[1:]
    g_ref, wg_hbm, wu_hbm, wd_hbm, o_ref, wg_b, wu_b, wd_b, gu_stage, d_stage, sems, acc_scr = rest
    first = pl.program_id(0) == 0

    def chunk_copies(c):
        slot = c % 2
        cols = pl.ds(c * tf, tf)
        return (pltpu.make_async_copy(wg_hbm.at[layer, :, cols], gu_stage.at[slot, 0], sems.at[slot, 0]),
                pltpu.make_async_copy(wu_hbm.at[layer, :, cols], gu_stage.at[slot, 1], sems.at[slot, 1]),
                pltpu.make_async_copy(wd_hbm.at[layer, cols, :], d_stage.at[slot], sems.at[slot, 2]))

    @pl.when(first)
    def _():
        for cp in chunk_copies(0):
            cp.start()

    x = x_ref[...]
    for idx, m_ref in enumerate(mix_refs):
        x = x + _dot(m_ref[...].astype(BF16), wo_ref[idx * GROUP:(idx + 1) * GROUP, :])
    h = _rms(x, g_ref[...]).astype(BF16)

    def land_chunk(c):
        cols = slice(c * tf, (c + 1) * tf)
        if c + 1 < n_f:
            for cp in chunk_copies(c + 1):
                cp.start()
        for cp in chunk_copies(c):
            cp.wait()
        wg_b[:, cols] = gu_stage[c % 2, 0].astype(BF16)
        wu_b[:, cols] = gu_stage[c % 2, 1].astype(BF16)
        wd_b[cols, :] = d_stage[c % 2].astype(BF16)

    def swiglu(fetch):
        for c in range(n_f):
            cols = slice(c * tf, (c + 1) * tf)
            fetch(c)
            gate = _dot(h, wg_b[:, cols])
            up = _dot(h, wu_b[:, cols])
            act = (gate * jax.nn.sigmoid(gate) * up).astype(BF16)
            part = _dot(act, wd_b[cols, :])
            if c == 0:
                acc_scr[...] = part
            else:
                acc_scr[...] += part
        o_ref[...] = x + 0.5 * acc_scr[...]

    @pl.when(first)
    def _():
        swiglu(land_chunk)

    @pl.when(jnp.logical_not(first))
    def _():
        swiglu(lambda c: None)


def _ffn(x, mixes, w_out, norm, wg, wu, wd, layer, tm, tf):
    n, d = x.shape
    f = wg.shape[2]
    n_mix = len(mixes)
    row = lambda i: (i, 0)
    hbm = pl.BlockSpec(memory_space=pl.ANY)
    in_specs = [pl.BlockSpec((tm, d), row)] + [pl.BlockSpec((tm, GROUP), row)] * n_mix
    args = [x, *mixes]
    if n_mix:
        in_specs.append(pl.BlockSpec(w_out.shape, lambda i: (0, 0)))
        args.append(w_out)
    in_specs += [pl.BlockSpec((1, d), lambda i: (0, 0)), hbm, hbm, hbm]
    args += [norm, wg, wu, wd]
    return pl.pallas_call(
        functools.partial(_ffn_kernel, n_f=f // tf, tf=tf, n_mix=n_mix, layer=layer),
        grid=(n // tm,),
        in_specs=in_specs,
        out_specs=pl.BlockSpec((tm, d), row),
        out_shape=jax.ShapeDtypeStruct((n, d), F32),
        scratch_shapes=[pltpu.VMEM((d, f), BF16), pltpu.VMEM((d, f), BF16), pltpu.VMEM((f, d), BF16),
                        pltpu.VMEM((2, 2, d, tf), F32), pltpu.VMEM((2, tf, d), F32),
                        pltpu.SemaphoreType.DMA((2, 3)), pltpu.VMEM((tm, d), F32)],
        compiler_params=_cparams(("arbitrary",)),
        name="ffn_out" if n_mix else "ffn",
    )(*args)


def _proj_in_kernel(*refs, kv_transposed, n_alias):
    x_ref, g_ref, w_ref, qn_ref, kn_ref, gn_ref, seg_ref = refs[:7]
    q_o, k_o, v_o, p_o, a_o, u_o, vg_o = refs[7 + n_alias:]
    G = GROUP
    h = _rms(x_ref[...], g_ref[...]).astype(BF16)
    z = _dot(h, w_ref[...])
    seg = seg_ref[...]

    def head_rms(t, gain):
        sq = t * t
        hi, lo = _split(sq)
        ms = _dot(hi, seg) + _dot(lo, seg)
        return t * lax.rsqrt(ms + RMS_EPS) * gain

    q_o[...] = head_rms(z[:, 0:G], qn_ref[...])
    k = head_rms(z[:, G:2 * G], kn_ref[...])
    v = z[:, 2 * G:3 * G]
    if not kv_transposed:
        k_o[...] = k
        v_o[...] = v
    elif len(k_o.shape) == 2:
        k_o[...] = k.T
        v_o[...] = v.T
    else:
        k_o[0] = k.T
        v_o[0] = v.T
        for slot in range(1, k_o.shape[0]):
            k_o[slot] = jnp.zeros(k_o.shape[1:], F32)
            v_o[slot] = jnp.zeros(v_o.shape[1:], F32)
    p_o[...] = z[:, 3 * G:4 * G]
    a_o[...] = z[:, 4 * G:5 * G] * jax.nn.sigmoid(z[:, 5 * G:6 * G])
    d = jax.nn.gelu(z[:, 6 * G:8 * G])
    u_o[...] = d[:, 0:G]
    vg_o[...] = _rms(d[:, G:2 * G], gn_ref[...])


def _proj_in(x, norm, w_in, qn, kn, gn, seg, tm, kv_stack=None):
    n, d = x.shape
    cols = w_in.shape[1]
    row = lambda i: (i, 0)
    fixed = lambda i: (0, 0)
    out = jax.ShapeDtypeStruct((n, GROUP), F32)
    in_specs = [pl.BlockSpec((tm, d), row),
                pl.BlockSpec((1, d), fixed),
                pl.BlockSpec((d, cols), fixed),
                pl.BlockSpec((1, GROUP), fixed),
                pl.BlockSpec((1, GROUP), fixed),
                pl.BlockSpec((1, GROUP), fixed),
                pl.BlockSpec((GROUP, GROUP), fixed)]
    args = [x, norm, w_in, qn, kn, gn, seg]
    out_specs = [pl.BlockSpec((tm, GROUP), row)] * 7
    out_shape = [out] * 7
    aliases = {}
    if kv_stack is not None:
        layer, depth, b, t, prev = kv_stack
        per_b = t // tm
        if prev is None:
            assert layer == 0
            kv_spec = pl.BlockSpec((depth, None, GROUP, tm), lambda i: (0, i // per_b, 0, i % per_b))
        else:
            kv_spec = pl.BlockSpec((None, None, GROUP, tm), lambda i: (layer, i // per_b, 0, i % per_b))
        out_specs[1:3] = [kv_spec, kv_spec]
        out_shape[1:3] = [jax.ShapeDtypeStruct((depth, b, GROUP, t), F32)] * 2
        if prev is not None:
            in_specs += [pl.BlockSpec(memory_space=pl.ANY)] * 2
            aliases = {len(args): 1, len(args) + 1: 2}
            args += list(prev)
    return pl.pallas_call(
        functools.partial(_proj_in_kernel, kv_transposed=kv_stack is not None, n_alias=len(aliases)),
        grid=(n // tm,),
        in_specs=in_specs,
        out_specs=out_specs,
        out_shape=out_shape,
        input_output_aliases=aliases,
        compiler_params=_cparams(("parallel",)),
        name="proj_in",
    )(*args)


GATE_ROWS = 8
VT_ROWS = HEAD_DIM + 16


def _select_bias_rows(gate, n_valid):
    rows = lax.broadcasted_iota(jnp.int32, gate.shape, 0)
    rows_f = rows.astype(F32)
    cur = jnp.where(rows < n_valid, gate, NEG_INF)
    bias = jnp.where(rows == n_valid, 0.0, NEG_INF)
    for _ in range(MOBA_TOP_K):
        top = jnp.max(cur, axis=0, keepdims=True)
        first = jnp.min(jnp.where(cur == top, rows_f, float(gate.shape[0])), axis=0, keepdims=True)
        pick = (rows_f == first) & (top > NEG_INF)
        bias = jnp.where(pick, 0.0, bias)
        cur = jnp.where(pick, NEG_INF, cur)
    return bias


def _attn_prompt_kernel(q_ref, k_ref, v_ref, o_ref, kb_scr, vt_scr, kbd_scr, bias_scr, qk_scr, alibi_scr,
                        *, n_blk):
    i = pl.program_id(1)
    B = MOBA_BLOCK
    lane_head = _lane_group((1, GROUP), HEAD_DIM)
    slopes = [2.0 ** (-8.0 * (h + 1) / N_HEADS) for h in range(N_HEADS)]

    @pl.when(i == 0)
    def _():
        neg_rel = (lax.broadcasted_iota(jnp.int32, (B, B), 0)
                   - lax.broadcasted_iota(jnp.int32, (B, B), 1)).astype(F32)
        for h in range(N_HEADS):
            alibi_scr[h] = slopes[h] * neg_rel
            alibi_scr[N_HEADS + h] = jnp.where(neg_rel <= 0.0, slopes[h] * neg_rel, NEG_INF)
        kbd_scr[...] = jnp.zeros_like(kbd_scr)
        ones = jnp.ones((VT_ROWS - HEAD_DIM, B), BF16)
        for n in range(n_blk):
            kn = k_ref[:, n * B:(n + 1) * B].T
            kb_scr[n] = kn.astype(BF16)
            mean = jnp.mean(kn, axis=0, keepdims=True)
            vt = v_ref[:, n * B:(n + 1) * B]
            for h in range(N_HEADS):
                r = h * GATE_ROWS + n
                kbd_scr[r:r + 1, :] = jnp.where(lane_head == h, mean, 0.0)
                vt_scr[n, h, 0:HEAD_DIM, :] = vt[h * HEAD_DIM:(h + 1) * HEAD_DIM, :].astype(BF16)
                vt_scr[n, h, HEAD_DIM:VT_ROWS, :] = ones

    q = q_ref[0]
    gate = _dot3_nt(kbd_scr[...], q)
    for h in range(N_HEADS):
        rows = slice(h * GATE_ROWS, (h + 1) * GATE_ROWS)
        bias_scr[rows, :] = _select_bias_rows(gate[rows, :], i)

    q_scaled = q * (HEAD_DIM ** -0.5)
    in_head = _lane_group((B, GROUP), HEAD_DIM)
    qh = [jnp.where(in_head == h, q_scaled, 0.0).astype(BF16) for h in range(N_HEADS)]

    def scores(blk):
        kn = kb_scr[blk]
        return tuple(_dot_nt(kn, qh[h]) for h in range(N_HEADS))

    def absorb(blk, ms, accs):
        causal = jnp.where(blk == i, N_HEADS, 0)
        dist = ((i - blk) * B).astype(F32)
        new_ms, new_accs = [], []
        for h in range(N_HEADS):
            bias = bias_scr[pl.ds(h * GATE_ROWS + blk, 1), :] - slopes[h] * dist
            s = qk_scr[h] + alibi_scr[h + causal] + bias
            m_new = jnp.maximum(ms[h], jnp.max(s, axis=0, keepdims=True))
            alpha = jnp.exp(ms[h] - m_new)
            p = jnp.exp(s - m_new)
            new_ms.append(m_new)
            new_accs.append(alpha * accs[h] + _dot(vt_scr[blk, h], p.astype(BF16)))
        return tuple(new_ms), tuple(new_accs)

    def stash(qk):
        for h in range(N_HEADS):
            qk_scr[h] = qk[h]

    def body(j, carry):
        qk_next = scores(j)
        carry = absorb(jnp.where(j == 0, i, j - 1), *carry)
        stash(qk_next)
        return carry

    stash(scores(i))
    init = ((jnp.full((1, B), NEG_INF, F32),) * N_HEADS, (jnp.zeros((VT_ROWS, B), F32),) * N_HEADS)
    ms, accs = lax.fori_loop(0, i, body, init)
    ms, accs = absorb(jnp.where(i == 0, i, i - 1), ms, accs)
    out_t = jnp.concatenate([a[0:HEAD_DIM, :] / a[HEAD_DIM:HEAD_DIM + 1, :] for a in accs], axis=0)
    o_ref[0] = out_t.T


def _attn_prompt(q, kt, vt, layer):
    b, t, g = q.shape
    n_blk = t // MOBA_BLOCK
    assert t % MOBA_BLOCK == 0 and n_blk <= GATE_ROWS
    kv_spec = pl.BlockSpec((None, None, g, t), lambda bi, i: (layer, bi, 0, 0))
    return pl.pallas_call(
        functools.partial(_attn_prompt_kernel, n_blk=n_blk),
        grid=(b, n_blk),
        in_specs=[pl.BlockSpec((1, MOBA_BLOCK, g), lambda bi, i: (bi, i, 0)), kv_spec, kv_spec],
        out_specs=pl.BlockSpec((1, MOBA_BLOCK, g), lambda bi, i: (bi, i, 0)),
        out_shape=jax.ShapeDtypeStruct((b, t, g), F32),
        scratch_shapes=[pltpu.VMEM((n_blk, MOBA_BLOCK, g), BF16),
                        pltpu.VMEM((n_blk, N_HEADS, VT_ROWS, MOBA_BLOCK), BF16),
                        pltpu.VMEM((N_HEADS * GATE_ROWS, g), F32),
                        pltpu.VMEM((N_HEADS * GATE_ROWS, MOBA_BLOCK), F32),
                        pltpu.VMEM((N_HEADS, MOBA_BLOCK, MOBA_BLOCK), F32),
                        pltpu.VMEM((2 * N_HEADS, MOBA_BLOCK, MOBA_BLOCK), F32)],
        compiler_params=_cparams(("parallel", "arbitrary")),
        name="attn_prompt",
    )(q, kt, vt)


def _attn_sample_kernel(pt_ref, q_ref, kn_ref, vn_ref, ck_hbm, cv_hbm, o_ref, k_buf, v_buf, sems,
                        *, n_pages, t_new, page_base):
    b = pl.program_id(0)

    def page_copies(seq, slot):
        out = []
        for pg in range(n_pages):
            page = page_base + pt_ref[seq, pg]
            out.append(pltpu.make_async_copy(ck_hbm.at[page], k_buf.at[slot, pg], sems.at[0, slot]))
            out.append(pltpu.make_async_copy(cv_hbm.at[page], v_buf.at[slot, pg], sems.at[1, slot]))
        return out

    @pl.when(b == 0)
    def _():
        for idx, cp in enumerate(page_copies(0, 0)):
            cp.start(priority=idx % 2)

    @pl.when(b + 1 < pl.num_programs(0))
    def _():
        for idx, cp in enumerate(page_copies(b + 1, (b + 1) % 2)):
            cp.start(priority=idx % 2)

    slot = b % 2
    for cp in page_copies(b, slot):
        cp.wait()
    k_pages = [k_buf.at[slot, pg] for pg in range(n_pages)]
    v_pages = [v_buf.at[slot, pg] for pg in range(n_pages)]
    T = t_new
    R = N_HEADS * T
    n_blk = n_pages * PAGE // MOBA_BLOCK
    past = n_pages * PAGE
    scale = HEAD_DIM ** -0.5

    q = q_ref[0] * scale
    q_rows = jnp.concatenate([q] * N_HEADS, axis=0)
    row_head = lax.broadcasted_iota(jnp.int32, (R, GROUP), 0) // T
    qbd = jnp.where(_lane_group((R, GROUP), HEAD_DIM) == row_head, q_rows, 0.0)
    q_hi, q_lo = _split(qbd)

    raw = []
    for pg in range(n_pages):
        k_hi, k_lo = _split(k_pages[pg][...])
        raw.append(_dot(q_hi, k_hi) + _dot(q_hi, k_lo) + _dot(q_lo, k_hi))
    per_blk = MOBA_BLOCK // PAGE
    gates = [jnp.sum(sum(raw[n * per_blk:(n + 1) * per_blk]), axis=-1, keepdims=True) for n in range(n_blk)]
    biases = []
    for n in range(n_blk):
        rank = jnp.zeros((R, 1), F32)
        for mth in range(n_blk):
            if mth != n:
                beats = (gates[mth] >= gates[n]) if mth < n else (gates[mth] > gates[n])
                rank = rank + jnp.where(beats, 1.0, 0.0)
        biases.append(jnp.where(rank < MOBA_TOP_K, 0.0, NEG_INF))

    r1 = lax.broadcasted_iota(jnp.int32, (R, 1), 0)
    slope = jnp.zeros((R, 1), F32)
    for h in range(N_HEADS):
        slope = jnp.where(r1 // T == h, 2.0 ** (-8.0 * (h + 1) / N_HEADS), slope)
    t_row = r1 % T
    pos_q = (past + t_row).astype(F32)
    col = lax.broadcasted_iota(jnp.int32, (R, PAGE), 1)

    scores = []
    for pg in range(n_pages):
        key_pos = (col + pg * PAGE).astype(F32)
        scores.append(raw[pg] - slope * (pos_q - key_pos) + biases[pg // per_blk])
    pad = jnp.zeros((PAGE - T, GROUP), F32)
    k_new = jnp.concatenate([kn_ref[0], pad], axis=0).astype(BF16)
    v_new = jnp.concatenate([vn_ref[0], pad], axis=0).astype(BF16)
    s_new = _dot_nt(q_hi, k_new) - slope * (t_row - col).astype(F32)
    s_new = jnp.where(col <= t_row, s_new, NEG_INF)
    scores.append(s_new)

    m = functools.reduce(jnp.maximum, [jnp.max(s, axis=-1, keepdims=True) for s in scores])
    l = jnp.zeros((R, 1), F32)
    acc = jnp.zeros((R, GROUP), F32)
    for pg in range(n_pages + 1):
        p = jnp.exp(scores[pg] - m)
        l = l + jnp.sum(p, axis=-1, keepdims=True)
        if pg == n_pages:
            acc = acc + _dot(p.astype(BF16), v_new)
        else:
            acc = acc + _dot_nt(p.astype(BF16), v_pages[pg][...].astype(BF16))
    acc = acc / l

    lane_head = _lane_group((T, GROUP), HEAD_DIM)
    out = jnp.zeros((T, GROUP), F32)
    for h in range(N_HEADS):
        out = jnp.where(lane_head == h, acc[h * T:(h + 1) * T, :], out)
    o_ref[0] = out


def _attn_sample(q, k_new, v_new, cache_k, cache_v, page_table, page_base):
    db, t, g = q.shape
    n_pages = page_table.shape[1]
    tok = pl.BlockSpec((1, t, g), lambda b, pt: (b, 0, 0))
    hbm = pl.BlockSpec(memory_space=pl.ANY)
    page_buf = pltpu.VMEM((2, n_pages, g, PAGE), F32)
    return pl.pallas_call(
        functools.partial(_attn_sample_kernel, n_pages=n_pages, t_new=t, page_base=page_base),
        grid_spec=pltpu.PrefetchScalarGridSpec(
            num_scalar_prefetch=1,
            grid=(db,),
            in_specs=[tok, tok, tok, hbm, hbm],
            out_specs=pl.BlockSpec((1, t, g), lambda b, pt: (b, 0, 0)),
            scratch_shapes=[page_buf, page_buf, pltpu.SemaphoreType.DMA((2, 2))],
        ),
        out_shape=jax.ShapeDtypeStruct((db, t, g), F32),
        compiler_params=_cparams(("arbitrary",)),
        name="attn_sample",
    )(page_table, q, k_new, v_new, cache_k, cache_v)


def _layernorm_silu(y, g, b):
    mu = jnp.mean(y, axis=-1, keepdims=True)
    yc = y - mu
    var = jnp.mean(yc * yc, axis=-1, keepdims=True)
    yn = yc * lax.rsqrt(var + LN_EPS) * g + b
    return yn * jax.nn.sigmoid(yn)


def _pool_window_select(sums, shape):
    grp = _lane_group(shape, GROUP // len(POOL_WINDOWS))
    sel = sums[POOL_WINDOWS[-1]]
    for gi in range(len(POOL_WINDOWS) - 2, -1, -1):
        sel = jnp.where(grp == gi, sums[POOL_WINDOWS[gi]], sel)
    win = jnp.left_shift(jnp.int32(POOL_WINDOWS[0]), grp)
    return sel, win


MIX_ROWS = 256
HALO = 32


def _mix_prompt_kernel(pp_ref, p_ref, ap_ref, a_ref, u_ref, vg_ref, poolw_ref, pscale_ref,
                       convw_ref, convb_ref, lng_ref, lnb_ref, convpw_ref, sw_ref, sb_ref,
                       pool_o, conv_o, gmlp_o, p_scr, a_scr):
    c = pl.program_id(1)
    C = CHUNK
    has_prev = c > 0
    p_scr[0:HALO, :] = jnp.where(has_prev, pp_ref[0], 0.0)
    p_scr[HALO:HALO + MIX_ROWS, :] = p_ref[0]
    a_scr[0:HALO, :] = jnp.where(has_prev, ap_ref[0], 0.0)
    a_scr[HALO:HALO + MIX_ROWS, :] = a_ref[0]
    tri = lax.broadcasted_iota(jnp.int32, (C, C), 0) >= lax.broadcasted_iota(jnp.int32, (C, C), 1)
    w_causal = [jnp.where(tri, sw_ref[g], 0.0).astype(BF16) for g in range(N_HEADS)]
    grp = _lane_group((C, GROUP), HEAD_DIM)

    for sub in range(MIX_ROWS // C):
        s0 = HALO + sub * C
        rows = slice(sub * C, (sub + 1) * C)

        w2 = p_scr[s0 - 24:s0 + C, :] + p_scr[s0 - 25:s0 + C - 1, :]
        w4 = w2[8:C + 24] + w2[6:C + 22]
        w8 = w4[8:C + 16] + w4[4:C + 12]
        w16 = w8[8:C + 8] + w8[0:C]
        sums = {2: w2[24:C + 24], 4: w4[16:C + 16], 8: w8[8:C + 8], 16: w16}
        sel, win = _pool_window_select(sums, (C, GROUP))
        t_glob = c * MIX_ROWS + sub * C + lax.broadcasted_iota(jnp.int32, (C, GROUP), 0)
        count = jnp.minimum(t_glob + 1, win).astype(F32)
        m = sel / count - p_scr[s0:s0 + C, :]
        pool_o[0, rows, :] = _dot(m.astype(BF16), poolw_ref[...]) * pscale_ref[...]

        base = s0 - CONV_HIST
        y = jnp.zeros((C, GROUP), F32)
        for r in range(8):
            taps = [j for j in range(CONV_WIDTH) if (base + j) % 8 == r]
            n_rows = C + (8 if r else 0)
            z = jnp.zeros((n_rows, GROUP), F32)
            for j in taps:
                start = base + j - r
                z = z + a_scr[start:start + n_rows, :] * convw_ref[j:j + 1, :]
            y = y + z[r:r + C]
        y = _layernorm_silu(y + convb_ref[...], lng_ref[...], lnb_ref[...])
        conv_o[0, rows, :] = _dot(y.astype(BF16), convpw_ref[...])

        vgb = vg_ref[0, rows, :].astype(BF16)
        s = sb_ref[...]
        for g in range(N_HEADS):
            s = s + jnp.where(grp == g, _dot(w_causal[g], vgb), 0.0)
        gmlp_o[0, rows, :] = u_ref[0, rows, :] * s


def _mix_prompt(p, a, u, vg, poolw, pscale, convw, convb, lng, lnb, convpw, sw, sb):
    b, t, g = p.shape
    assert POOL_WINDOWS == (2, 4, 8, 16) and t % MIX_ROWS == 0 and MIX_ROWS % CHUNK == 0 and HALO >= CONV_HIST
    per = MIX_ROWS // HALO
    cur = pl.BlockSpec((1, MIX_ROWS, g), lambda bi, c: (bi, c, 0))
    prev = pl.BlockSpec((1, HALO, g), lambda bi, c: (bi, jnp.maximum(c * per - 1, 0), 0))

    def full(arr):
        nd = arr.ndim
        return pl.BlockSpec(arr.shape, lambda bi, c: (0,) * nd)

    out = jax.ShapeDtypeStruct((b, t, g), F32)
    weights = (poolw, pscale, convw, convb, lng, lnb, convpw, sw, sb)
    scratch = pltpu.VMEM((HALO + MIX_ROWS, g), F32)
    return pl.pallas_call(
        _mix_prompt_kernel,
        grid=(b, t // MIX_ROWS),
        in_specs=[prev, cur, prev, cur, cur, cur] + [full(w) for w in weights],
        out_specs=[cur] * 3,
        out_shape=[out] * 3,
        scratch_shapes=[scratch, scratch],
        compiler_params=_cparams(("parallel", "arbitrary")),
        name="mix_prompt",
    )(p, p, a, a, u, vg, *weights)


def _mix_sample_kernel(hp_ref, p_ref, hc_ref, a_ref, u_ref, vg_ref, poolw_ref, pscale_ref,
                       convw_ref, convb_ref, lng_ref, lnb_ref, convpw_ref, sw_ref, sb_ref,
                       pool_o, conv_o, gmlp_o, *, t_new):
    T = t_new
    nb = p_ref.shape[1]
    shape = (nb, GROUP)

    def pool_row(k):
        return hp_ref[k] if k < POOL_HIST else p_ref[k - POOL_HIST]

    def conv_row(k):
        return hc_ref[k] if k < CONV_HIST else a_ref[k - CONV_HIST]

    for t in range(T):
        acc = jnp.zeros(shape, F32)
        sums = {}
        for i in range(POOL_WINDOWS[-1]):
            k = POOL_HIST + t - i
            if k >= 0:
                acc = acc + pool_row(k)
            if i + 1 in POOL_WINDOWS:
                sums[i + 1] = acc
        sel, win = _pool_window_select(sums, shape)
        count = jnp.minimum(POOL_HIST + t + 1, win).astype(F32)
        m = sel / count - p_ref[t]
        pool_o[t] = _dot(m.astype(BF16), poolw_ref[...]) * pscale_ref[...]

        y = jnp.zeros(shape, F32)
        for j in range(CONV_WIDTH):
            y = y + conv_row(t + j) * convw_ref[j:j + 1, :]
        y = _layernorm_silu(y + convb_ref[...], lng_ref[...], lnb_ref[...])
        conv_o[t] = _dot(y.astype(BF16), convpw_ref[...])

        s = jnp.broadcast_to(sb_ref[t:t + 1, :], shape)
        for j in range(t + 1):
            s = s + sw_ref[t, j:j + 1, :] * vg_ref[j]
        gmlp_o[t] = u_ref[t] * s


def _mix_sample(hp, p, hc, a, u, vg, poolw, pscale, convw, convb, lng, lnb, convpw, sw, sb):
    t, db, g = p.shape

    def full(arr):
        nd = arr.ndim
        return pl.BlockSpec(arr.shape, lambda i: (0,) * nd)

    args = (hp, p, hc, a, u, vg, poolw, pscale, convw, convb, lng, lnb, convpw, sw, sb)
    out = jax.ShapeDtypeStruct((t, db, g), F32)
    return pl.pallas_call(
        functools.partial(_mix_sample_kernel, t_new=t),
        grid=(1,),
        in_specs=[full(x) for x in args],
        out_specs=[full(out)] * 3,
        out_shape=[out] * 3,
        compiler_params=_cparams(("arbitrary",)),
        name="mix_sample",
    )(*args)


def _row_tile(n, cap):
    t = min(n, cap)
    while n % t:
        t //= 2
    return t


def _ffn_tile(f):
    for tf in (512, 256, 128):
        if f % tf == 0:
            return tf
    return f


def _layer_weights(l, ffn1_norm, ffn1_w_gate, ffn1_w_up, ffn1_w_down, mix_norm, w_in, q_norm, k_norm,
                   pool_w, pool_scale, conv_w, conv_b, conv_ln_g, conv_ln_b, conv_pw,
                   gate_norm, spatial_w, spatial_b, w_out, ffn2_norm, ffn2_w_gate, ffn2_w_up, ffn2_w_down):
    row = lambda v: v[l].reshape(1, -1)
    head_id = jnp.arange(GROUP) // HEAD_DIM
    seg = jnp.where(head_id[:, None] == head_id[None, :], 1.0 / HEAD_DIM, 0.0).astype(BF16)
    n_win = len(POOL_WINDOWS)
    pg = GROUP // n_win
    pool_bd = jnp.zeros((GROUP, GROUP), F32)
    for gi in range(n_win):
        pool_bd = pool_bd.at[gi * pg:(gi + 1) * pg, gi * pg:(gi + 1) * pg].set(pool_w[l, gi])
    sb_lanes = jnp.repeat(spatial_b[l].T, HEAD_DIM, axis=1)
    return dict(
        layer=l,
        ffn1=(row(ffn1_norm), ffn1_w_gate, ffn1_w_up, ffn1_w_down),
        ffn2=(row(ffn2_norm), ffn2_w_gate, ffn2_w_up, ffn2_w_down),
        proj_in=(row(mix_norm), w_in[l].astype(BF16), jnp.tile(q_norm[l], N_HEADS).reshape(1, -1),
                 jnp.tile(k_norm[l], N_HEADS).reshape(1, -1), row(gate_norm), seg),
        w_out=w_out[l].astype(BF16),
        mix=(pool_bd.astype(BF16), row(pool_scale), conv_w[l], row(conv_b), row(conv_ln_g), row(conv_ln_b),
             conv_pw[l].astype(BF16)),
        spatial_w=spatial_w[l],
        spatial_b_lanes=sb_lanes,
    )


def _token_stage_in(x, w, kv_stack=None):
    n = x.shape[0]
    x1 = _ffn(x, (), None, *w["ffn1"], layer=w["layer"], tm=_row_tile(n, FFN_ROWS),
              tf=_ffn_tile(w["ffn1"][1].shape[2]))
    tm = _row_tile(n if kv_stack is None else kv_stack[3], 512)
    return x1, _proj_in(x1, *w["proj_in"], tm=tm, kv_stack=kv_stack)


def _token_stage_out(x1, att, pool, conv, gmlp, w):
    n = x1.shape[0]
    return _ffn(x1, (att, pool, conv, gmlp), w["w_out"], *w["ffn2"], layer=w["layer"], tm=_row_tile(n, FFN_ROWS),
                tf=_ffn_tile(w["ffn2"][1].shape[2]))


def _prompt_layer(x, w, depth, kv_prev):
    b, t, d = x.shape
    x1, (q, kt, vt, p, a, u, vg) = _token_stage_in(x.reshape(b * t, d), w, (w["layer"], depth, b, t, kv_prev))
    r3 = lambda arr: arr.reshape(b, t, GROUP)
    q, p, a, u, vg = map(r3, (q, p, a, u, vg))
    att = _attn_prompt(q, kt, vt, w["layer"])
    pool, conv, gmlp = _mix_prompt(p, a, u, vg, *w["mix"], w["spatial_w"], w["spatial_b_lanes"])
    flat = lambda arr: arr.reshape(b * t, GROUP)
    y = _token_stage_out(x1, flat(att), flat(pool), flat(conv), flat(gmlp), w)
    return y.reshape(b, t, d), (kt, vt), p[:, -POOL_HIST:], a[:, -CONV_HIST:]


def _sample_layer(x, cache_k, cache_v, page_table, page_base, hist_pool, hist_conv, w):
    db, t, d = x.shape
    x1, (q, k, v, p, a, u, vg) = _token_stage_in(x.reshape(db * t, d), w)
    r3 = lambda arr: arr.reshape(db, t, GROUP)
    q, k, v, p, a, u, vg = map(r3, (q, k, v, p, a, u, vg))
    att = _attn_sample(q, k, v, cache_k, cache_v, page_table, page_base)
    tm = lambda arr: jnp.swapaxes(arr, 0, 1)
    sw_lanes = jnp.repeat(jnp.transpose(w["spatial_w"][:, :t, :t], (1, 2, 0)), HEAD_DIM, axis=2)
    hp_t, p_t, hc_t, a_t = tm(hist_pool), tm(p), tm(hist_conv), tm(a)
    pool, conv, gmlp = _mix_sample(hp_t, p_t, hc_t, a_t, tm(u), tm(vg),
                                   *w["mix"], sw_lanes, w["spatial_b_lanes"][:t])
    flat = lambda arr: tm(arr).reshape(db * t, GROUP)
    y = _token_stage_out(x1, att.reshape(db * t, GROUP), flat(pool), flat(conv), flat(gmlp), w)
    pool_state = tm(jnp.concatenate([hp_t, p_t], axis=0)[-POOL_HIST:])
    conv_state = tm(jnp.concatenate([hc_t, a_t], axis=0)[-CONV_HIST:])
    return y.reshape(db, t, d), k, v, pool_state, conv_state, vg


def kernel(x_prompt, x_sample, cache_k, cache_v, state_pool, state_conv, page_table, ffn1_norm, ffn1_w_gate, ffn1_w_up, ffn1_w_down, mix_norm, w_in, q_norm, k_norm, pool_w, pool_scale, conv_w, conv_b, conv_ln_g, conv_ln_b, conv_pw, gate_norm, spatial_w, spatial_b, w_out, ffn2_norm, ffn2_w_gate, ffn2_w_up, ffn2_w_down):
    depth, n_pool = cache_k.shape[0], cache_k.shape[1]
    b, t = x_prompt.shape[0], x_prompt.shape[1]
    db, ts = x_sample.shape[0], x_sample.shape[1]
    ck = jnp.transpose(cache_k, (0, 1, 3, 4, 2)).reshape(depth * n_pool, GROUP, PAGE)
    cv = jnp.transpose(cache_v, (0, 1, 3, 4, 2)).reshape(depth * n_pool, GROUP, PAGE)
    hp, hs = x_prompt, x_sample
    outs = [[] for _ in range(7)]
    kv_prompt = None
    for l in range(depth):
        w = _layer_weights(l, ffn1_norm, ffn1_w_gate, ffn1_w_up, ffn1_w_down, mix_norm, w_in, q_norm, k_norm,
                           pool_w, pool_scale, conv_w, conv_b, conv_ln_g, conv_ln_b, conv_pw,
                           gate_norm, spatial_w, spatial_b, w_out, ffn2_norm, ffn2_w_gate, ffn2_w_up, ffn2_w_down)
        hp, kv_prompt, pp, cp = _prompt_layer(hp, w, depth, kv_prompt)
        hs, ks, vs, ps, cs, gs = _sample_layer(hs, ck, cv, page_table, l * n_pool, state_pool[l], state_conv[l], w)
        heads = lambda arr: arr.reshape(db, ts, N_HEADS, HEAD_DIM)
        for lst, val in zip(outs, (heads(ks), heads(vs), pp, ps, cp, cs, gs)):
            lst.append(val)
    k_prompt, v_prompt = (jnp.transpose(s.reshape(depth, b, N_HEADS, HEAD_DIM, t), (0, 1, 4, 2, 3)) for s in kv_prompt)
    return (hp, hs, k_prompt, v_prompt) + tuple(jnp.stack(lst) for lst in outs)
```

```python
import functools

import jax
import jax.numpy as jnp
from jax import lax
from jax.experimental import pallas as pl
from jax.experimental.pallas import tpu as pltpu

F32 = jnp.float32
BF16 = jnp.bfloat16
NEG_INF = float("-inf")

GROUP = 256
HEAD_DIM = 64
N_HEADS = GROUP // HEAD_DIM
MOBA_BLOCK = 256
MOBA_TOP_K = 3
PAGE = 128
POOL_WINDOWS = (2, 4, 8, 16)
POOL_HIST = 15
CONV_WIDTH = 31
CONV_HIST = 30
CHUNK = 128
FFN_ROWS = 512
SAMPLE_SEQS = 4
RMS_EPS = 1e-6
LN_EPS = 1e-5
VMEM_LIMIT = 56 * 1024 * 1024


def _cparams(sem):
    return pltpu.CompilerParams(dimension_semantics=sem, vmem_limit_bytes=VMEM_LIMIT)


def _rms(x, g):
    ms = jnp.mean(x * x, axis=-1, keepdims=True)
    return x * lax.rsqrt(ms + RMS_EPS) * g


def _dot(a, b):
    return jnp.dot(a, b, preferred_element_type=F32)


def _dot_nt(a, b):
    return lax.dot_general(a, b, (((1,), (1,)), ((), ())), preferred_element_type=F32)


def _split(a):
    hi = a.astype(BF16)
    lo = (a - hi.astype(F32)).astype(BF16)
    return hi, lo


def _dot3_nt(a, b):
    ah, al = _split(a)
    bh, bl = _split(b)
    return _dot_nt(ah, bh) + _dot_nt(ah, bl) + _dot_nt(al, bh)


def _lane_group(shape, width):
    return lax.broadcasted_iota(jnp.int32, shape, len(shape) - 1) // width


def _ffn_kernel(*refs, n_f, tf, n_mix, layer):
    x_ref = refs[0]
    mix_refs = refs[1:1 + n_mix]
    rest = refs[1 + n_mix:]
    if n_mix:
        wo_ref, rest = rest[0], rest[1:]
    g_ref, wg_hbm, wu_hbm, wd_hbm, o_ref, wg_b, wu_b, wd_b, gu_stage, d_stage, sems, acc_scr = rest
    first = pl.program_id(0) == 0

    def chunk_copies(c):
        slot = c % 2
        cols = pl.ds(c * tf, tf)
        return (pltpu.make_async_copy(wg_hbm.at[layer, :, cols], gu_stage.at[slot, 0], sems.at[slot, 0]),
                pltpu.make_async_copy(wu_hbm.at[layer, :, cols], gu_stage.at[slot, 1], sems.at[slot, 1]),
                pltpu.make_async_copy(wd_hbm.at[layer, cols, :], d_stage.at[slot], sems.at[slot, 2]))

    @pl.when(first)
    def _():
        for cp in chunk_copies(0):
            cp.start()

    x = x_ref[...]
    for idx, m_ref in enumerate(mix_refs):
        x = x + _dot(m_ref[...].astype(BF16), wo_ref[idx * GROUP:(idx + 1) * GROUP, :])
    h = _rms(x, g_ref[...]).astype(BF16)

    def land_chunk(c):
        cols = slice(c * tf, (c + 1) * tf)
        if c + 1 < n_f:
            for cp in chunk_copies(c + 1):
                cp.start()
        for cp in chunk_copies(c):
            cp.wait()
        wg_b[:, cols] = gu_stage[c % 2, 0].astype(BF16)
        wu_b[:, cols] = gu_stage[c % 2, 1].astype(BF16)
        wd_b[cols, :] = d_stage[c % 2].astype(BF16)

    def swiglu(fetch):
        for c in range(n_f):
            cols = slice(c * tf, (c + 1) * tf)
            fetch(c)
            gate = _dot(h, wg_b[:, cols])
            up = _dot(h, wu_b[:, cols])
            act = (gate * jax.nn.sigmoid(gate) * up).astype(BF16)
            part = _dot(act, wd_b[cols, :])
            if c == 0:
                acc_scr[...] = part
            else:
                acc_scr[...] += part
        o_ref[...] = x + 0.5 * acc_scr[...]

    @pl.when(first)
    def _():
        swiglu(land_chunk)

    @pl.when(jnp.logical_not(first))
    def _():
        swiglu(lambda c: None)


def _ffn(x, mixes, w_out, norm, wg, wu, wd, layer, tm, tf):
    n, d = x.shape
    f = wg.shape[2]
    n_mix = len(mixes)
    row = lambda i: (i, 0)
    hbm = pl.BlockSpec(memory_space=pl.ANY)
    in_specs = [pl.BlockSpec((tm, d), row)] + [pl.BlockSpec((tm, GROUP), row)] * n_mix
    args = [x, *mixes]
    if n_mix:
        in_specs.append(pl.BlockSpec(w_out.shape, lambda i: (0, 0)))
        args.append(w_out)
    in_specs += [pl.BlockSpec((1, d), lambda i: (0, 0)), hbm, hbm, hbm]
    args += [norm, wg, wu, wd]
    return pl.pallas_call(
        functools.partial(_ffn_kernel, n_f=f // tf, tf=tf, n_mix=n_mix, layer=layer),
        grid=(n // tm,),
        in_specs=in_specs,
        out_specs=pl.BlockSpec((tm, d), row),
        out_shape=jax.ShapeDtypeStruct((n, d), F32),
        scratch_shapes=[pltpu.VMEM((d, f), BF16), pltpu.VMEM((d, f), BF16), pltpu.VMEM((f, d), BF16),
                        pltpu.VMEM((2, 2, d, tf), F32), pltpu.VMEM((2, tf, d), F32),
                        pltpu.SemaphoreType.DMA((2, 3)), pltpu.VMEM((tm, d), F32)],
        compiler_params=_cparams(("arbitrary",)),
        name="ffn_out" if n_mix else "ffn",
    )(*args)


def _proj_in_kernel(*refs, kv_transposed, n_alias):
    x_ref, g_ref, w_ref, qn_ref, kn_ref, gn_ref, seg_ref = refs[:7]
    q_o, k_o, v_o, p_o, a_o, u_o, vg_o = refs[7 + n_alias:]
    G = GROUP
    h = _rms(x_ref[...], g_ref[...]).astype(BF16)
    z = _dot(h, w_ref[...])
    seg = seg_ref[...]

    def head_rms(t, gain):
        sq = t * t
        hi, lo = _split(sq)
        ms = _dot(hi, seg) + _dot(lo, seg)
        return t * lax.rsqrt(ms + RMS_EPS) * gain

    q_o[...] = head_rms(z[:, 0:G], qn_ref[...])
    k = head_rms(z[:, G:2 * G], kn_ref[...])
    v = z[:, 2 * G:3 * G]
    if not kv_transposed:
        k_o[...] = k
        v_o[...] = v
    elif len(k_o.shape) == 2:
        k_o[...] = k.T
        v_o[...] = v.T
    else:
        k_o[0] = k.T
        v_o[0] = v.T
        for slot in range(1, k_o.shape[0]):
            k_o[slot] = jnp.zeros(k_o.shape[1:], F32)
            v_o[slot] = jnp.zeros(v_o.shape[1:], F32)
    p_o[...] = z[:, 3 * G:4 * G]
    a_o[...] = z[:, 4 * G:5 * G] * jax.nn.sigmoid(z[:, 5 * G:6 * G])
    d = jax.nn.gelu(z[:, 6 * G:8 * G])
    u_o[...] = d[:, 0:G]
    vg_o[...] = _rms(d[:, G:2 * G], gn_ref[...])


def _proj_in(x, norm, w_in, qn, kn, gn, seg, tm, kv_stack=None):
    n, d = x.shape
    cols = w_in.shape[1]
    row = lambda i: (i, 0)
    fixed = lambda i: (0, 0)
    out = jax.ShapeDtypeStruct((n, GROUP), F32)
    in_specs = [pl.BlockSpec((tm, d), row),
                pl.BlockSpec((1, d), fixed),
                pl.BlockSpec((d, cols), fixed),
                pl.BlockSpec((1, GROUP), fixed),
                pl.BlockSpec((1, GROUP), fixed),
                pl.BlockSpec((1, GROUP), fixed),
                pl.BlockSpec((GROUP, GROUP), fixed)]
    args = [x, norm, w_in, qn, kn, gn, seg]
    out_specs = [pl.BlockSpec((tm, GROUP), row)] * 7
    out_shape = [out] * 7
    aliases = {}
    if kv_stack is not None:
        layer, depth, b, t, prev = kv_stack
        per_b = t // tm
        if prev is None:
            assert layer == 0
            kv_spec = pl.BlockSpec((depth, None, GROUP, tm), lambda i: (0, i // per_b, 0, i % per_b))
        else:
            kv_spec = pl.BlockSpec((None, None, GROUP, tm), lambda i: (layer, i // per_b, 0, i % per_b))
        out_specs[1:3] = [kv_spec, kv_spec]
        out_shape[1:3] = [jax.ShapeDtypeStruct((depth, b, GROUP, t), F32)] * 2
        if prev is not None:
            in_specs += [pl.BlockSpec(memory_space=pl.ANY)] * 2
            aliases = {len(args): 1, len(args) + 1: 2}
            args += list(prev)
    return pl.pallas_call(
        functools.partial(_proj_in_kernel, kv_transposed=kv_stack is not None, n_alias=len(aliases)),
        grid=(n // tm,),
        in_specs=in_specs,
        out_specs=out_specs,
        out_shape=out_shape,
        input_output_aliases=aliases,
        compiler_params=_cparams(("parallel",)),
        name="proj_in",
    )(*args)


GATE_ROWS = 8
VT_ROWS = HEAD_DIM + 16


def _select_bias_rows(gate, n_valid):
    rows = lax.broadcasted_iota(jnp.int32, gate.shape, 0)
    rows_f = rows.astype(F32)
    cur = jnp.where(rows < n_valid, gate, NEG_INF)
    bias = jnp.where(rows == n_valid, 0.0, NEG_INF)
    for _ in range(MOBA_TOP_K):
        top = jnp.max(cur, axis=0, keepdims=True)
        first = jnp.min(jnp.where(cur == top, rows_f, float(gate.shape[0])), axis=0, keepdims=True)
        pick = (rows_f == first) & (top > NEG_INF)
        bias = jnp.where(pick, 0.0, bias)
        cur = jnp.where(pick, NEG_INF, cur)
    return bias


def _attn_prompt_kernel(q_ref, k_ref, v_ref, o_ref, kb_scr, vt_scr, kbd_scr, bias_scr, qk_scr, alibi_scr,
                        *, n_blk):
    i = pl.program_id(1)
    B = MOBA_BLOCK
    lane_head = _lane_group((1, GROUP), HEAD_DIM)
    slopes = [2.0 ** (-8.0 * (h + 1) / N_HEADS) for h in range(N_HEADS)]

    @pl.when(i == 0)
    def _():
        neg_rel = (lax.broadcasted_iota(jnp.int32, (B, B), 0)
                   - lax.broadcasted_iota(jnp.int32, (B, B), 1)).astype(F32)
        for h in range(N_HEADS):
            alibi_scr[h] = slopes[h] * neg_rel
            alibi_scr[N_HEADS + h] = jnp.where(neg_rel <= 0.0, slopes[h] * neg_rel, NEG_INF)
        kbd_scr[...] = jnp.zeros_like(kbd_scr)
        ones = jnp.ones((VT_ROWS - HEAD_DIM, B), BF16)
        for n in range(n_blk):
            kn = k_ref[:, n * B:(n + 1) * B].T
            kb_scr[n] = kn.astype(BF16)
            mean = jnp.mean(kn, axis=0, keepdims=True)
            vt = v_ref[:, n * B:(n + 1) * B]
            for h in range(N_HEADS):
                r = h * GATE_ROWS + n
                kbd_scr[r:r + 1, :] = jnp.where(lane_head == h, mean, 0.0)
                vt_scr[n, h, 0:HEAD_DIM, :] = vt[h * HEAD_DIM:(h + 1) * HEAD_DIM, :].astype(BF16)
                vt_scr[n, h, HEAD_DIM:VT_ROWS, :] = ones

    q = q_ref[0]
    gate = _dot3_nt(kbd_scr[...], q)
    for h in range(N_HEADS):
        rows = slice(h * GATE_ROWS, (h + 1) * GATE_ROWS)
        bias_scr[rows, :] = _select_bias_rows(gate[rows, :], i)

    q_scaled = q * (HEAD_DIM ** -0.5)
    in_head = _lane_group((B, GROUP), HEAD_DIM)
    qh = [jnp.where(in_head == h, q_scaled, 0.0).astype(BF16) for h in range(N_HEADS)]

    def scores(blk):
        kn = kb_scr[blk]
        return tuple(_dot_nt(kn, qh[h]) for h in range(N_HEADS))

    def absorb(blk, ms, accs):
        causal = jnp.where(blk == i, N_HEADS, 0)
        dist = ((i - blk) * B).astype(F32)
        new_ms, new_accs = [], []
        for h in range(N_HEADS):
            bias = bias_scr[pl.ds(h * GATE_ROWS + blk, 1), :] - slopes[h] * dist
            s = qk_scr[h] + alibi_scr[h + causal] + bias
            m_new = jnp.maximum(ms[h], jnp.max(s, axis=0, keepdims=True))
            alpha = jnp.exp(ms[h] - m_new)
            p = jnp.exp(s - m_new)
            new_ms.append(m_new)
            new_accs.append(alpha * accs[h] + _dot(vt_scr[blk, h], p.astype(BF16)))
        return tuple(new_ms), tuple(new_accs)

    def stash(qk):
        for h in range(N_HEADS):
            qk_scr[h] = qk[h]

    def body(j, carry):
        qk_next = scores(j)
        carry = absorb(jnp.where(j == 0, i, j - 1), *carry)
        stash(qk_next)
        return carry

    stash(scores(i))
    init = ((jnp.full((1, B), NEG_INF, F32),) * N_HEADS, (jnp.zeros((VT_ROWS, B), F32),) * N_HEADS)
    ms, accs = lax.fori_loop(0, i, body, init)
    ms, accs = absorb(jnp.where(i == 0, i, i - 1), ms, accs)
    out_t = jnp.concatenate([a[0:HEAD_DIM, :] / a[HEAD_DIM:HEAD_DIM + 1, :] for a in accs], axis=0)
    o_ref[0] = out_t.T


def _attn_prompt(q, kt, vt, layer):
    b, t, g = q.shape
    n_blk = t // MOBA_BLOCK
    assert t % MOBA_BLOCK == 0 and n_blk <= GATE_ROWS
    kv_spec = pl.BlockSpec((None, None, g, t), lambda bi, i: (layer, bi, 0, 0))
    return pl.pallas_call(
        functools.partial(_attn_prompt_kernel, n_blk=n_blk),
        grid=(b, n_blk),
        in_specs=[pl.BlockSpec((1, MOBA_BLOCK, g), lambda bi, i: (bi, i, 0)), kv_spec, kv_spec],
        out_specs=pl.BlockSpec((1, MOBA_BLOCK, g), lambda bi, i: (bi, i, 0)),
        out_shape=jax.ShapeDtypeStruct((b, t, g), F32),
        scratch_shapes=[pltpu.VMEM((n_blk, MOBA_BLOCK, g), BF16),
                        pltpu.VMEM((n_blk, N_HEADS, VT_ROWS, MOBA_BLOCK), BF16),
                        pltpu.VMEM((N_HEADS * GATE_ROWS, g), F32),
                        pltpu.VMEM((N_HEADS * GATE_ROWS, MOBA_BLOCK), F32),
                        pltpu.VMEM((N_HEADS, MOBA_BLOCK, MOBA_BLOCK), F32),
                        pltpu.VMEM((2 * N_HEADS, MOBA_BLOCK, MOBA_BLOCK), F32)],
        compiler_params=_cparams(("parallel", "arbitrary")),
        name="attn_prompt",
    )(q, kt, vt)


def _attn_sample_kernel(pt_ref, q_ref, kn_ref, vn_ref, ck_hbm, cv_hbm, o_ref, k_buf, v_buf, sems,
                        *, n_pages, t_new, page_base, n_seq):
    b = pl.program_id(0)

    def page_copies(step, slot):
        out = []
        for s in range(n_seq):
            for pg in range(n_pages):
                page = page_base + pt_ref[step * n_seq + s, pg]
                dst = s * n_pages + pg
                out.append(pltpu.make_async_copy(ck_hbm.at[page], k_buf.at[slot, dst], sems.at[0, slot]))
                out.append(pltpu.make_async_copy(cv_hbm.at[page], v_buf.at[slot, dst], sems.at[1, slot]))
        return out

    @pl.when(b == 0)
    def _():
        for idx, cp in enumerate(page_copies(0, 0)):
            cp.start(priority=idx % 2)

    @pl.when(b + 1 < pl.num_programs(0))
    def _():
        for idx, cp in enumerate(page_copies(b + 1, (b + 1) % 2)):
            cp.start(priority=idx % 2)

    slot = b % 2
    for cp in page_copies(b, slot):
        cp.wait()
    seqs = []
    for s in range(n_seq):
        k_pages = [k_buf.at[slot, s * n_pages + pg] for pg in range(n_pages)]
        v_pages = [v_buf.at[slot, s * n_pages + pg] for pg in range(n_pages)]
        seqs.append((q_ref[s], kn_ref[s], vn_ref[s], k_pages, v_pages))
    scored = [_sample_scores(q, k_new, k_pages, t_new) for q, k_new, _, k_pages, _ in seqs]
    probs = [_sample_softmax(*sc, t_new) for sc in scored]
    for s, ((_, _, v_new, _, v_pages), (ps, l)) in enumerate(zip(seqs, probs)):
        o_ref[s] = _sample_values(ps, l, v_new, v_pages, t_new)


def _sample_scores(q_in, k_in, k_pages, t_new):
    T = t_new
    R = N_HEADS * T
    q = q_in * HEAD_DIM ** -0.5
    q_rows = jnp.concatenate([q] * N_HEADS, axis=0)
    row_head = lax.broadcasted_iota(jnp.int32, (R, GROUP), 0) // T
    qbd = jnp.where(_lane_group((R, GROUP), HEAD_DIM) == row_head, q_rows, 0.0)
    q_hi, q_lo = _split(qbd)
    raw = []
    for kp in k_pages:
        k_hi, k_lo = _split(kp[...])
        raw.append(_dot(q_hi, k_hi) + _dot(q_hi, k_lo) + _dot(q_lo, k_hi))
    pad = jnp.zeros((PAGE - T, GROUP), F32)
    k_new = jnp.concatenate([k_in, pad], axis=0).astype(BF16)
    return raw, _dot_nt(q_hi, k_new)


def _sample_softmax(raw, raw_new, t_new):
    T = t_new
    R = N_HEADS * T
    n_pages = len(raw)
    per_blk = MOBA_BLOCK // PAGE
    n_blk = n_pages // per_blk
    past = n_pages * PAGE
    gates = [jnp.sum(sum(raw[n * per_blk:(n + 1) * per_blk]), axis=-1, keepdims=True) for n in range(n_blk)]
    biases = []
    for n in range(n_blk):
        rank = jnp.zeros((R, 1), F32)
        for mth in range(n_blk):
            if mth != n:
                beats = (gates[mth] >= gates[n]) if mth < n else (gates[mth] > gates[n])
                rank = rank + jnp.where(beats, 1.0, 0.0)
        biases.append(jnp.where(rank < MOBA_TOP_K, 0.0, NEG_INF))

    r1 = lax.broadcasted_iota(jnp.int32, (R, 1), 0)
    slope = jnp.zeros((R, 1), F32)
    for h in range(N_HEADS):
        slope = jnp.where(r1 // T == h, 2.0 ** (-8.0 * (h + 1) / N_HEADS), slope)
    t_row = r1 % T
    pos_q = (past + t_row).astype(F32)
    col = lax.broadcasted_iota(jnp.int32, (R, PAGE), 1)
    scores = []
    for pg in range(n_pages):
        key_pos = (col + pg * PAGE).astype(F32)
        scores.append(raw[pg] - slope * (pos_q - key_pos) + biases[pg // per_blk])
    s_new = raw_new - slope * (t_row - col).astype(F32)
    scores.append(jnp.where(col <= t_row, s_new, NEG_INF))

    m = functools.reduce(jnp.maximum, [jnp.max(s, axis=-1, keepdims=True) for s in scores])
    ps = [jnp.exp(s - m) for s in scores]
    l = functools.reduce(lambda a, b: a + b, [jnp.sum(p, axis=-1, keepdims=True) for p in ps])
    return ps, l


def _sample_values(ps, l, v_in, v_pages, t_new):
    T = t_new
    pad = jnp.zeros((PAGE - T, GROUP), F32)
    v_new = jnp.concatenate([v_in, pad], axis=0).astype(BF16)
    acc = _dot(ps[-1].astype(BF16), v_new)
    for p, vp in zip(ps[:-1], v_pages):
        acc = acc + _dot_nt(p.astype(BF16), vp[...].astype(BF16))
    acc = acc / l
    lane_head = _lane_group((T, GROUP), HEAD_DIM)
    out = jnp.zeros((T, GROUP), F32)
    for h in range(N_HEADS):
        out = jnp.where(lane_head == h, acc[h * T:(h + 1) * T, :], out)
    return out


def _attn_sample(q, k_new, v_new, cache_k, cache_v, page_table, page_base):
    db, t, g = q.shape
    n_pages = page_table.shape[1]
    n_seq = SAMPLE_SEQS if db % SAMPLE_SEQS == 0 else 1
    tok = pl.BlockSpec((n_seq, t, g), lambda b, pt: (b, 0, 0))
    hbm = pl.BlockSpec(memory_space=pl.ANY)
    page_buf = pltpu.VMEM((2, n_seq * n_pages, g, PAGE), F32)
    return pl.pallas_call(
        functools.partial(_attn_sample_kernel, n_pages=n_pages, t_new=t, page_base=page_base, n_seq=n_seq),
        grid_spec=pltpu.PrefetchScalarGridSpec(
            num_scalar_prefetch=1,
            grid=(db // n_seq,),
            in_specs=[tok, tok, tok, hbm, hbm],
            out_specs=tok,
            scratch_shapes=[page_buf, page_buf, pltpu.SemaphoreType.DMA((2, 2))],
        ),
        out_shape=jax.ShapeDtypeStruct((db, t, g), F32),
        compiler_params=_cparams(("arbitrary",)),
        name="attn_sample",
    )(page_table, q, k_new, v_new, cache_k, cache_v)


def _layernorm_silu(y, g, b):
    mu = jnp.mean(y, axis=-1, keepdims=True)
    yc = y - mu
    var = jnp.mean(yc * yc, axis=-1, keepdims=True)
    yn = yc * lax.rsqrt(var + LN_EPS) * g + b
    return yn * jax.nn.sigmoid(yn)


def _pool_window_select(sums, shape):
    grp = _lane_group(shape, GROUP // len(POOL_WINDOWS))
    sel = sums[POOL_WINDOWS[-1]]
    for gi in range(len(POOL_WINDOWS) - 2, -1, -1):
        sel = jnp.where(grp == gi, sums[POOL_WINDOWS[gi]], sel)
    win = jnp.left_shift(jnp.int32(POOL_WINDOWS[0]), grp)
    return sel, win


MIX_ROWS = 256
HALO = 32


def _mix_prompt_kernel(pp_ref, p_ref, ap_ref, a_ref, u_ref, vg_ref, poolw_ref, pscale_ref,
                       convw_ref, convb_ref, lng_ref, lnb_ref, convpw_ref, sw_ref, sb_ref,
                       pool_o, conv_o, gmlp_o, p_scr, a_scr):
    c = pl.program_id(1)
    C = CHUNK
    has_prev = c > 0
    p_scr[0:HALO, :] = jnp.where(has_prev, pp_ref[0], 0.0)
    p_scr[HALO:HALO + MIX_ROWS, :] = p_ref[0]
    a_scr[0:HALO, :] = jnp.where(has_prev, ap_ref[0], 0.0)
    a_scr[HALO:HALO + MIX_ROWS, :] = a_ref[0]
    tri = lax.broadcasted_iota(jnp.int32, (C, C), 0) >= lax.broadcasted_iota(jnp.int32, (C, C), 1)
    w_causal = [jnp.where(tri, sw_ref[g], 0.0).astype(BF16) for g in range(N_HEADS)]
    grp = _lane_group((C, GROUP), HEAD_DIM)

    for sub in range(MIX_ROWS // C):
        s0 = HALO + sub * C
        rows = slice(sub * C, (sub + 1) * C)

        w2 = p_scr[s0 - 24:s0 + C, :] + p_scr[s0 - 25:s0 + C - 1, :]
        w4 = w2[8:C + 24] + w2[6:C + 22]
        w8 = w4[8:C + 16] + w4[4:C + 12]
        w16 = w8[8:C + 8] + w8[0:C]
        sums = {2: w2[24:C + 24], 4: w4[16:C + 16], 8: w8[8:C + 8], 16: w16}
        sel, win = _pool_window_select(sums, (C, GROUP))
        t_glob = c * MIX_ROWS + sub * C + lax.broadcasted_iota(jnp.int32, (C, GROUP), 0)
        count = jnp.minimum(t_glob + 1, win).astype(F32)
        m = sel / count - p_scr[s0:s0 + C, :]
        pool_o[0, rows, :] = _dot(m.astype(BF16), poolw_ref[...]) * pscale_ref[...]

        base = s0 - CONV_HIST
        y = jnp.zeros((C, GROUP), F32)
        for r in range(8):
            taps = [j for j in range(CONV_WIDTH) if (base + j) % 8 == r]
            n_rows = C + (8 if r else 0)
            z = jnp.zeros((n_rows, GROUP), F32)
            for j in taps:
                start = base + j - r
                z = z + a_scr[start:start + n_rows, :] * convw_ref[j:j + 1, :]
            y = y + z[r:r + C]
        y = _layernorm_silu(y + convb_ref[...], lng_ref[...], lnb_ref[...])
        conv_o[0, rows, :] = _dot(y.astype(BF16), convpw_ref[...])

        vgb = vg_ref[0, rows, :].astype(BF16)
        s = sb_ref[...]
        for g in range(N_HEADS):
            s = s + jnp.where(grp == g, _dot(w_causal[g], vgb), 0.0)
        gmlp_o[0, rows, :] = u_ref[0, rows, :] * s


def _mix_prompt(p, a, u, vg, poolw, pscale, convw, convb, lng, lnb, convpw, sw, sb):
    b, t, g = p.shape
    assert POOL_WINDOWS == (2, 4, 8, 16) and t % MIX_ROWS == 0 and MIX_ROWS % CHUNK == 0 and HALO >= CONV_HIST
    per = MIX_ROWS // HALO
    cur = pl.BlockSpec((1, MIX_ROWS, g), lambda bi, c: (bi, c, 0))
    prev = pl.BlockSpec((1, HALO, g), lambda bi, c: (bi, jnp.maximum(c * per - 1, 0), 0))

    def full(arr):
        nd = arr.ndim
        return pl.BlockSpec(arr.shape, lambda bi, c: (0,) * nd)

    out = jax.ShapeDtypeStruct((b, t, g), F32)
    weights = (poolw, pscale, convw, convb, lng, lnb, convpw, sw, sb)
    scratch = pltpu.VMEM((HALO + MIX_ROWS, g), F32)
    return pl.pallas_call(
        _mix_prompt_kernel,
        grid=(b, t // MIX_ROWS),
        in_specs=[prev, cur, prev, cur, cur, cur] + [full(w) for w in weights],
        out_specs=[cur] * 3,
        out_shape=[out] * 3,
        scratch_shapes=[scratch, scratch],
        compiler_params=_cparams(("parallel", "arbitrary")),
        name="mix_prompt",
    )(p, p, a, a, u, vg, *weights)


def _mix_sample_kernel(hp_ref, p_ref, hc_ref, a_ref, u_ref, vg_ref, poolw_ref, pscale_ref,
                       convw_ref, convb_ref, lng_ref, lnb_ref, convpw_ref, sw_ref, sb_ref,
                       pool_o, conv_o, gmlp_o, *, t_new):
    T = t_new
    nb = p_ref.shape[1]
    shape = (nb, GROUP)

    def pool_row(k):
        return hp_ref[k] if k < POOL_HIST else p_ref[k - POOL_HIST]

    def conv_row(k):
        return hc_ref[k] if k < CONV_HIST else a_ref[k - CONV_HIST]

    for t in range(T):
        acc = jnp.zeros(shape, F32)
        sums = {}
        for i in range(POOL_WINDOWS[-1]):
            k = POOL_HIST + t - i
            if k >= 0:
                acc = acc + pool_row(k)
            if i + 1 in POOL_WINDOWS:
                sums[i + 1] = acc
        sel, win = _pool_window_select(sums, shape)
        count = jnp.minimum(POOL_HIST + t + 1, win).astype(F32)
        m = sel / count - p_ref[t]
        pool_o[t] = _dot(m.astype(BF16), poolw_ref[...]) * pscale_ref[...]

        y = jnp.zeros(shape, F32)
        for j in range(CONV_WIDTH):
            y = y + conv_row(t + j) * convw_ref[j:j + 1, :]
        y = _layernorm_silu(y + convb_ref[...], lng_ref[...], lnb_ref[...])
        conv_o[t] = _dot(y.astype(BF16), convpw_ref[...])

        s = jnp.broadcast_to(sb_ref[t:t + 1, :], shape)
        for j in range(t + 1):
            s = s + sw_ref[t, j:j + 1, :] * vg_ref[j]
        gmlp_o[t] = u_ref[t] * s


def _mix_sample(hp, p, hc, a, u, vg, poolw, pscale, convw, convb, lng, lnb, convpw, sw, sb):
    t, db, g = p.shape

    def full(arr):
        nd = arr.ndim
        return pl.BlockSpec(arr.shape, lambda i: (0,) * nd)

    args = (hp, p, hc, a, u, vg, poolw, pscale, convw, convb, lng, lnb, convpw, sw, sb)
    out = jax.ShapeDtypeStruct((t, db, g), F32)
    return pl.pallas_call(
        functools.partial(_mix_sample_kernel, t_new=t),
        grid=(1,),
        in_specs=[full(x) for x in args],
        out_specs=[full(out)] * 3,
        out_shape=[out] * 3,
        compiler_params=_cparams(("arbitrary",)),
        name="mix_sample",
    )(*args)


def _row_tile(n, cap):
    t = min(n, cap)
    while n % t:
        t //= 2
    return t


def _ffn_tile(f):
    for tf in (512, 256, 128):
        if f % tf == 0:
            return tf
    return f


def _layer_weights(l, ffn1_norm, ffn1_w_gate, ffn1_w_up, ffn1_w_down, mix_norm, w_in, q_norm, k_norm,
                   pool_w, pool_scale, conv_w, conv_b, conv_ln_g, conv_ln_b, conv_pw,
                   gate_norm, spatial_w, spatial_b, w_out, ffn2_norm, ffn2_w_gate, ffn2_w_up, ffn2_w_down):
    row = lambda v: v[l].reshape(1, -1)
    head_id = jnp.arange(GROUP) // HEAD_DIM
    seg = jnp.where(head_id[:, None] == head_id[None, :], 1.0 / HEAD_DIM, 0.0).astype(BF16)
    n_win = len(POOL_WINDOWS)
    pg = GROUP // n_win
    pool_bd = jnp.zeros((GROUP, GROUP), F32)
    for gi in range(n_win):
        pool_bd = pool_bd.at[gi * pg:(gi + 1) * pg, gi * pg:(gi + 1) * pg].set(pool_w[l, gi])
    sb_lanes = jnp.repeat(spatial_b[l].T, HEAD_DIM, axis=1)
    return dict(
        layer=l,
        ffn1=(row(ffn1_norm), ffn1_w_gate, ffn1_w_up, ffn1_w_down),
        ffn2=(row(ffn2_norm), ffn2_w_gate, ffn2_w_up, ffn2_w_down),
        proj_in=(row(mix_norm), w_in[l].astype(BF16), jnp.tile(q_norm[l], N_HEADS).reshape(1, -1),
                 jnp.tile(k_norm[l], N_HEADS).reshape(1, -1), row(gate_norm), seg),
        w_out=w_out[l].astype(BF16),
        mix=(pool_bd.astype(BF16), row(pool_scale), conv_w[l], row(conv_b), row(conv_ln_g), row(conv_ln_b),
             conv_pw[l].astype(BF16)),
        spatial_w=spatial_w[l],
        spatial_b_lanes=sb_lanes,
    )


def _token_stage_in(x, w, kv_stack=None):
    n = x.shape[0]
    x1 = _ffn(x, (), None, *w["ffn1"], layer=w["layer"], tm=_row_tile(n, FFN_ROWS),
              tf=_ffn_tile(w["ffn1"][1].shape[2]))
    tm = _row_tile(n if kv_stack is None else kv_stack[3], 512)
    return x1, _proj_in(x1, *w["proj_in"], tm=tm, kv_stack=kv_stack)


def _token_stage_out(x1, att, pool, conv, gmlp, w):
    n = x1.shape[0]
    return _ffn(x1, (att, pool, conv, gmlp), w["w_out"], *w["ffn2"], layer=w["layer"], tm=_row_tile(n, FFN_ROWS),
                tf=_ffn_tile(w["ffn2"][1].shape[2]))


def _prompt_layer(x, w, depth, kv_prev):
    b, t, d = x.shape
    x1, (q, kt, vt, p, a, u, vg) = _token_stage_in(x.reshape(b * t, d), w, (w["layer"], depth, b, t, kv_prev))
    r3 = lambda arr: arr.reshape(b, t, GROUP)
    q, p, a, u, vg = map(r3, (q, p, a, u, vg))
    att = _attn_prompt(q, kt, vt, w["layer"])
    pool, conv, gmlp = _mix_prompt(p, a, u, vg, *w["mix"], w["spatial_w"], w["spatial_b_lanes"])
    flat = lambda arr: arr.reshape(b * t, GROUP)
    y = _token_stage_out(x1, flat(att), flat(pool), flat(conv), flat(gmlp), w)
    return y.reshape(b, t, d), (kt, vt), p[:, -POOL_HIST:], a[:, -CONV_HIST:]


def _sample_layer(x, cache_k, cache_v, page_table, page_base, hist_pool, hist_conv, w):
    db, t, d = x.shape
    x1, (q, k, v, p, a, u, vg) = _token_stage_in(x.reshape(db * t, d), w)
    r3 = lambda arr: arr.reshape(db, t, GROUP)
    q, k, v, p, a, u, vg = map(r3, (q, k, v, p, a, u, vg))
    att = _attn_sample(q, k, v, cache_k, cache_v, page_table, page_base)
    tm = lambda arr: jnp.swapaxes(arr, 0, 1)
    sw_lanes = jnp.repeat(jnp.transpose(w["spatial_w"][:, :t, :t], (1, 2, 0)), HEAD_DIM, axis=2)
    hp_t, p_t, hc_t, a_t = tm(hist_pool), tm(p), tm(hist_conv), tm(a)
    pool, conv, gmlp = _mix_sample(hp_t, p_t, hc_t, a_t, tm(u), tm(vg),
                                   *w["mix"], sw_lanes, w["spatial_b_lanes"][:t])
    flat = lambda arr: tm(arr).reshape(db * t, GROUP)
    y = _token_stage_out(x1, att.reshape(db * t, GROUP), flat(pool), flat(conv), flat(gmlp), w)
    pool_state = tm(jnp.concatenate([hp_t, p_t], axis=0)[-POOL_HIST:])
    conv_state = tm(jnp.concatenate([hc_t, a_t], axis=0)[-CONV_HIST:])
    return y.reshape(db, t, d), k, v, pool_state, conv_state, vg


def kernel(x_prompt, x_sample, cache_k, cache_v, state_pool, state_conv, page_table, ffn1_norm, ffn1_w_gate, ffn1_w_up, ffn1_w_down, mix_norm, w_in, q_norm, k_norm, pool_w, pool_scale, conv_w, conv_b, conv_ln_g, conv_ln_b, conv_pw, gate_norm, spatial_w, spatial_b, w_out, ffn2_norm, ffn2_w_gate, ffn2_w_up, ffn2_w_down):
    depth, n_pool = cache_k.shape[0], cache_k.shape[1]
    b, t = x_prompt.shape[0], x_prompt.shape[1]
    db, ts = x_sample.shape[0], x_sample.shape[1]
    ck = jnp.transpose(cache_k, (0, 1, 3, 4, 2)).reshape(depth * n_pool, GROUP, PAGE)
    cv = jnp.transpose(cache_v, (0, 1, 3, 4, 2)).reshape(depth * n_pool, GROUP, PAGE)
    hp, hs = x_prompt, x_sample
    outs = [[] for _ in range(7)]
    kv_prompt = None
    for l in range(depth):
        w = _layer_weights(l, ffn1_norm, ffn1_w_gate, ffn1_w_up, ffn1_w_down, mix_norm, w_in, q_norm, k_norm,
                           pool_w, pool_scale, conv_w, conv_b, conv_ln_g, conv_ln_b, conv_pw,
                           gate_norm, spatial_w, spatial_b, w_out, ffn2_norm, ffn2_w_gate, ffn2_w_up, ffn2_w_down)
        hp, kv_prompt, pp, cp = _prompt_layer(hp, w, depth, kv_prompt)
        hs, ks, vs, ps, cs, gs = _sample_layer(hs, ck, cv, page_table, l * n_pool, state_pool[l], state_conv[l], w)
        heads = lambda arr: arr.reshape(db, ts, N_HEADS, HEAD_DIM)
        for lst, val in zip(outs, (heads(ks), heads(vs), pp, ps, cp, cs, gs)):
            lst.append(val)
    k_prompt, v_prompt = (jnp.transpose(s.reshape(depth, b, N_HEADS, HEAD_DIM, t), (0, 1, 4, 2, 3)) for s in kv_prompt)
    return (hp, hs, k_prompt, v_prompt) + tuple(jnp.stack(lst) for lst in outs)
```

```python
import functools

import jax
import jax.numpy as jnp
from jax import lax
from jax.experimental import pallas as pl
from jax.experimental.pallas import tpu as pltpu

F32 = jnp.float32
BF16 = jnp.bfloat16
NEG_INF = float("-inf")

GROUP = 256
HEAD_DIM = 64
N_HEADS = GROUP // HEAD_DIM
MOBA_BLOCK = 256
MOBA_TOP_K = 3
PAGE = 128
POOL_WINDOWS = (2, 4, 8, 16)
POOL_HIST = 15
CONV_WIDTH = 31
CONV_HIST = 30
CHUNK = 128
FFN_ROWS = 512
PROJ_ROWS = 1024
SAMPLE_SEQS = 4
RMS_EPS = 1e-6
LN_EPS = 1e-5
VMEM_LIMIT = 56 * 1024 * 1024


def _cparams(sem):
    return pltpu.CompilerParams(dimension_semantics=sem, vmem_limit_bytes=VMEM_LIMIT)


def _rms(x, g):
    ms = jnp.mean(x * x, axis=-1, keepdims=True)
    return x * lax.rsqrt(ms + RMS_EPS) * g


def _dot(a, b):
    return jnp.dot(a, b, preferred_element_type=F32)


def _dot_nt(a, b):
    return lax.dot_general(a, b, (((1,), (1,)), ((), ())), preferred_element_type=F32)


def _split(a):
    hi = a.astype(BF16)
    lo = (a - hi.astype(F32)).astype(BF16)
    return hi, lo


def _dot3_nt(a, b):
    ah, al = _split(a)
    bh, bl = _split(b)
    return _dot_nt(ah, bh) + _dot_nt(ah, bl) + _dot_nt(al, bh)


def _lane_group(shape, width):
    return lax.broadcasted_iota(jnp.int32, shape, len(shape) - 1) // width


def _ffn_kernel(*refs, n_f, tf, n_mix, layer):
    x_ref = refs[0]
    mix_refs = refs[1:1 + n_mix]
    rest = refs[1 + n_mix:]
    if n_mix:
        wo_ref, rest = rest[0], rest[1:]
    g_ref, wg_hbm, wu_hbm, wd_hbm, o_ref, wg_b, wu_b, wd_b, gu_stage, d_stage, sems, acc_scr = rest
    first = pl.program_id(0) == 0

    def chunk_copies(c):
        slot = c % 2
        cols = pl.ds(c * tf, tf)
        return (pltpu.make_async_copy(wg_hbm.at[layer, :, cols], gu_stage.at[slot, 0], sems.at[slot, 0]),
                pltpu.make_async_copy(wu_hbm.at[layer, :, cols], gu_stage.at[slot, 1], sems.at[slot, 1]),
                pltpu.make_async_copy(wd_hbm.at[layer, cols, :], d_stage.at[slot], sems.at[slot, 2]))

    @pl.when(first)
    def _():
        for cp in chunk_copies(0):
            cp.start()

    x = x_ref[...]
    for idx, m_ref in enumerate(mix_refs):
        x = x + _dot(m_ref[...].astype(BF16), wo_ref[idx * GROUP:(idx + 1) * GROUP, :])
    h = _rms(x, g_ref[...]).astype(BF16)

    def land_chunk(c):
        cols = slice(c * tf, (c + 1) * tf)
        if c + 1 < n_f:
            for cp in chunk_copies(c + 1):
                cp.start()
        for cp in chunk_copies(c):
            cp.wait()
        wg_b[:, cols] = gu_stage[c % 2, 0].astype(BF16)
        wu_b[:, cols] = gu_stage[c % 2, 1].astype(BF16)
        wd_b[cols, :] = d_stage[c % 2].astype(BF16)

    def swiglu(fetch):
        for c in range(n_f):
            cols = slice(c * tf, (c + 1) * tf)
            fetch(c)
            gate = _dot(h, wg_b[:, cols])
            up = _dot(h, wu_b[:, cols])
            act = (gate * jax.nn.sigmoid(gate) * up).astype(BF16)
            part = _dot(act, wd_b[cols, :])
            if c == 0:
                acc_scr[...] = part
            else:
                acc_scr[...] += part
        o_ref[...] = x + 0.5 * acc_scr[...]

    @pl.when(first)
    def _():
        swiglu(land_chunk)

    @pl.when(jnp.logical_not(first))
    def _():
        swiglu(lambda c: None)


def _ffn(x, mixes, w_out, norm, wg, wu, wd, layer, tm, tf):
    n, d = x.shape
    f = wg.shape[2]
    n_mix = len(mixes)
    row = lambda i: (i, 0)
    hbm = pl.BlockSpec(memory_space=pl.ANY)
    in_specs = [pl.BlockSpec((tm, d), row)] + [pl.BlockSpec((tm, GROUP), row)] * n_mix
    args = [x, *mixes]
    if n_mix:
        in_specs.append(pl.BlockSpec(w_out.shape, lambda i: (0, 0)))
        args.append(w_out)
    in_specs += [pl.BlockSpec((1, d), lambda i: (0, 0)), hbm, hbm, hbm]
    args += [norm, wg, wu, wd]
    return pl.pallas_call(
        functools.partial(_ffn_kernel, n_f=f // tf, tf=tf, n_mix=n_mix, layer=layer),
        grid=(n // tm,),
        in_specs=in_specs,
        out_specs=pl.BlockSpec((tm, d), row),
        out_shape=jax.ShapeDtypeStruct((n, d), F32),
        scratch_shapes=[pltpu.VMEM((d, f), BF16), pltpu.VMEM((d, f), BF16), pltpu.VMEM((f, d), BF16),
                        pltpu.VMEM((2, 2, d, tf), F32), pltpu.VMEM((2, tf, d), F32),
                        pltpu.SemaphoreType.DMA((2, 3)), pltpu.VMEM((tm, d), F32)],
        compiler_params=_cparams(("arbitrary",)),
        name="ffn_out" if n_mix else "ffn",
    )(*args)


def _proj_in_kernel(*refs, kv_transposed, n_alias):
    x_ref, g_ref, w_ref, qn_ref, kn_ref, gn_ref, seg_ref = refs[:7]
    q_o, k_o, v_o, p_o, a_o, u_o, vg_o = refs[7 + n_alias:]
    G = GROUP
    h = _rms(x_ref[...], g_ref[...]).astype(BF16)
    z = _dot(h, w_ref[...])
    seg = seg_ref[...]

    def head_rms(t, gain):
        sq = t * t
        hi, lo = _split(sq)
        ms = _dot(hi, seg) + _dot(lo, seg)
        return t * lax.rsqrt(ms + RMS_EPS) * gain

    q_o[...] = head_rms(z[:, 0:G], qn_ref[...])
    k = head_rms(z[:, G:2 * G], kn_ref[...])
    v = z[:, 2 * G:3 * G]
    if not kv_transposed:
        k_o[...] = k
        v_o[...] = v
    elif len(k_o.shape) == 2:
        k_o[...] = k.T
        v_o[...] = v.T
    else:
        k_o[0] = k.T
        v_o[0] = v.T
        for slot in range(1, k_o.shape[0]):
            k_o[slot] = jnp.zeros(k_o.shape[1:], F32)
            v_o[slot] = jnp.zeros(v_o.shape[1:], F32)
    p_o[...] = z[:, 3 * G:4 * G]
    a_o[...] = z[:, 4 * G:5 * G] * jax.nn.sigmoid(z[:, 5 * G:6 * G])
    d = jax.nn.gelu(z[:, 6 * G:8 * G])
    u_o[...] = d[:, 0:G]
    vg_o[...] = _rms(d[:, G:2 * G], gn_ref[...])


def _proj_in(x, norm, w_in, qn, kn, gn, seg, tm, kv_stack=None):
    n, d = x.shape
    cols = w_in.shape[1]
    row = lambda i: (i, 0)
    fixed = lambda i: (0, 0)
    out = jax.ShapeDtypeStruct((n, GROUP), F32)
    in_specs = [pl.BlockSpec((tm, d), row),
                pl.BlockSpec((1, d), fixed),
                pl.BlockSpec((d, cols), fixed),
                pl.BlockSpec((1, GROUP), fixed),
                pl.BlockSpec((1, GROUP), fixed),
                pl.BlockSpec((1, GROUP), fixed),
                pl.BlockSpec((GROUP, GROUP), fixed)]
    args = [x, norm, w_in, qn, kn, gn, seg]
    out_specs = [pl.BlockSpec((tm, GROUP), row)] * 7
    out_shape = [out] * 7
    aliases = {}
    if kv_stack is not None:
        layer, depth, b, t, prev = kv_stack
        per_b = t // tm
        if prev is None:
            assert layer == 0
            kv_spec = pl.BlockSpec((depth, None, GROUP, tm), lambda i: (0, i // per_b, 0, i % per_b))
        else:
            kv_spec = pl.BlockSpec((None, None, GROUP, tm), lambda i: (layer, i // per_b, 0, i % per_b))
        out_specs[1:3] = [kv_spec, kv_spec]
        out_shape[1:3] = [jax.ShapeDtypeStruct((depth, b, GROUP, t), F32)] * 2
        if prev is not None:
            in_specs += [pl.BlockSpec(memory_space=pl.ANY)] * 2
            aliases = {len(args): 1, len(args) + 1: 2}
            args += list(prev)
    return pl.pallas_call(
        functools.partial(_proj_in_kernel, kv_transposed=kv_stack is not None, n_alias=len(aliases)),
        grid=(n // tm,),
        in_specs=in_specs,
        out_specs=out_specs,
        out_shape=out_shape,
        input_output_aliases=aliases,
        compiler_params=_cparams(("parallel",)),
        name="proj_in",
    )(*args)


GATE_ROWS = 8
VT_ROWS = HEAD_DIM + 16


def _select_bias_rows(gate, n_valid):
    rows = lax.broadcasted_iota(jnp.int32, gate.shape, 0)
    rows_f = rows.astype(F32)
    cur = jnp.where(rows < n_valid, gate, NEG_INF)
    bias = jnp.where(rows == n_valid, 0.0, NEG_INF)
    for _ in range(MOBA_TOP_K):
        top = jnp.max(cur, axis=0, keepdims=True)
        first = jnp.min(jnp.where(cur == top, rows_f, float(gate.shape[0])), axis=0, keepdims=True)
        pick = (rows_f == first) & (top > NEG_INF)
        bias = jnp.where(pick, 0.0, bias)
        cur = jnp.where(pick, NEG_INF, cur)
    return bias


def _attn_prompt_kernel(q_ref, k_ref, v_ref, o_ref, kb_scr, vt_scr, kbd_scr, bias_scr, qk_scr, alibi_scr,
                        *, n_blk):
    i = pl.program_id(1)
    B = MOBA_BLOCK
    lane_head = _lane_group((1, GROUP), HEAD_DIM)
    slopes = [2.0 ** (-8.0 * (h + 1) / N_HEADS) for h in range(N_HEADS)]

    @pl.when(i == 0)
    def _():
        neg_rel = (lax.broadcasted_iota(jnp.int32, (B, B), 0)
                   - lax.broadcasted_iota(jnp.int32, (B, B), 1)).astype(F32)
        for h in range(N_HEADS):
            alibi_scr[h] = slopes[h] * neg_rel
            alibi_scr[N_HEADS + h] = jnp.where(neg_rel <= 0.0, slopes[h] * neg_rel, NEG_INF)
        kbd_scr[...] = jnp.zeros_like(kbd_scr)
        ones = jnp.ones((VT_ROWS - HEAD_DIM, B), BF16)
        for n in range(n_blk):
            kn = k_ref[:, n * B:(n + 1) * B].T
            kb_scr[n] = kn.astype(BF16)
            mean = jnp.mean(kn, axis=0, keepdims=True)
            vt = v_ref[:, n * B:(n + 1) * B]
            for h in range(N_HEADS):
                r = h * GATE_ROWS + n
                kbd_scr[r:r + 1, :] = jnp.where(lane_head == h, mean, 0.0)
                vt_scr[n, h, 0:HEAD_DIM, :] = vt[h * HEAD_DIM:(h + 1) * HEAD_DIM, :].astype(BF16)
                vt_scr[n, h, HEAD_DIM:VT_ROWS, :] = ones

    q = q_ref[0]
    gate = _dot3_nt(kbd_scr[...], q)
    for h in range(N_HEADS):
        rows = slice(h * GATE_ROWS, (h + 1) * GATE_ROWS)
        bias_scr[rows, :] = _select_bias_rows(gate[rows, :], i)

    q_scaled = q * (HEAD_DIM ** -0.5)
    in_head = _lane_group((B, GROUP), HEAD_DIM)
    qh = [jnp.where(in_head == h, q_scaled, 0.0).astype(BF16) for h in range(N_HEADS)]

    def scores(blk):
        kn = kb_scr[blk]
        return tuple(_dot_nt(kn, qh[h]) for h in range(N_HEADS))

    def absorb(blk, ms, accs):
        causal = jnp.where(blk == i, N_HEADS, 0)
        dist = ((i - blk) * B).astype(F32)
        ss, new_ms, ps = [], [], []
        for h in range(N_HEADS):
            bias = bias_scr[pl.ds(h * GATE_ROWS + blk, 1), :] - slopes[h] * dist
            ss.append(qk_scr[h] + alibi_scr[h + causal] + bias)
            new_ms.append(jnp.maximum(ms[h], jnp.max(ss[h], axis=0, keepdims=True)))
        for h in range(N_HEADS):
            ps.append(jnp.exp(ss[h] - new_ms[h]).astype(BF16))
        new_accs = [jnp.exp(ms[h] - new_ms[h]) * accs[h] + _dot(vt_scr[blk, h], ps[h])
                    for h in range(N_HEADS)]
        return tuple(new_ms), tuple(new_accs)

    def stash(qk):
        for h in range(N_HEADS):
            qk_scr[h] = qk[h]

    def body(j, carry):
        qk_next = scores(j)
        carry = absorb(jnp.where(j == 0, i, j - 1), *carry)
        stash(qk_next)
        return carry

    stash(scores(i))
    init = ((jnp.full((1, B), NEG_INF, F32),) * N_HEADS, (jnp.zeros((VT_ROWS, B), F32),) * N_HEADS)
    ms, accs = lax.fori_loop(0, i, body, init)
    ms, accs = absorb(jnp.where(i == 0, i, i - 1), ms, accs)
    out_t = jnp.concatenate([a[0:HEAD_DIM, :] / a[HEAD_DIM:HEAD_DIM + 1, :] for a in accs], axis=0)
    o_ref[0] = out_t.T


def _attn_prompt(q, kt, vt, layer):
    b, t, g = q.shape
    n_blk = t // MOBA_BLOCK
    assert t % MOBA_BLOCK == 0 and n_blk <= GATE_ROWS
    kv_spec = pl.BlockSpec((None, None, g, t), lambda bi, i: (layer, bi, 0, 0))
    return pl.pallas_call(
        functools.partial(_attn_prompt_kernel, n_blk=n_blk),
        grid=(b, n_blk),
        in_specs=[pl.BlockSpec((1, MOBA_BLOCK, g), lambda bi, i: (bi, i, 0)), kv_spec, kv_spec],
        out_specs=pl.BlockSpec((1, MOBA_BLOCK, g), lambda bi, i: (bi, i, 0)),
        out_shape=jax.ShapeDtypeStruct((b, t, g), F32),
        scratch_shapes=[pltpu.VMEM((n_blk, MOBA_BLOCK, g), BF16),
                        pltpu.VMEM((n_blk, N_HEADS, VT_ROWS, MOBA_BLOCK), BF16),
                        pltpu.VMEM((N_HEADS * GATE_ROWS, g), F32),
                        pltpu.VMEM((N_HEADS * GATE_ROWS, MOBA_BLOCK), F32),
                        pltpu.VMEM((N_HEADS, MOBA_BLOCK, MOBA_BLOCK), F32),
                        pltpu.VMEM((2 * N_HEADS, MOBA_BLOCK, MOBA_BLOCK), F32)],
        compiler_params=_cparams(("parallel", "arbitrary")),
        name="attn_prompt",
    )(q, kt, vt)


def _attn_sample_kernel(pt_ref, q_ref, kn_ref, vn_ref, ck_hbm, cv_hbm, o_ref, k_buf, v_buf, sems,
                        *, n_pages, t_new, page_base, n_seq):
    b = pl.program_id(0)

    def page_copies(step, slot):
        out = []
        for s in range(n_seq):
            for pg in range(n_pages):
                page = page_base + pt_ref[step * n_seq + s, pg]
                dst = s * n_pages + pg
                out.append(pltpu.make_async_copy(ck_hbm.at[page], k_buf.at[slot, dst], sems.at[0, slot]))
                out.append(pltpu.make_async_copy(cv_hbm.at[page], v_buf.at[slot, dst], sems.at[1, slot]))
        return out

    @pl.when(b == 0)
    def _():
        for idx, cp in enumerate(page_copies(0, 0)):
            cp.start(priority=idx % 2)

    @pl.when(b + 1 < pl.num_programs(0))
    def _():
        for idx, cp in enumerate(page_copies(b + 1, (b + 1) % 2)):
            cp.start(priority=idx % 2)

    slot = b % 2
    for cp in page_copies(b, slot):
        cp.wait()
    seqs = []
    for s in range(n_seq):
        k_pages = [k_buf.at[slot, s * n_pages + pg] for pg in range(n_pages)]
        v_pages = [v_buf.at[slot, s * n_pages + pg] for pg in range(n_pages)]
        seqs.append((q_ref[s], kn_ref[s], vn_ref[s], k_pages, v_pages))
    scored = [_sample_scores(q, k_new, k_pages, t_new) for q, k_new, _, k_pages, _ in seqs]
    probs = [_sample_softmax(*sc, t_new) for sc in scored]
    for s, ((_, _, v_new, _, v_pages), (ps, l)) in enumerate(zip(seqs, probs)):
        o_ref[s] = _sample_values(ps, l, v_new, v_pages, t_new)


def _sample_scores(q_in, k_in, k_pages, t_new):
    T = t_new
    R = N_HEADS * T
    q = q_in * HEAD_DIM ** -0.5
    q_rows = jnp.concatenate([q] * N_HEADS, axis=0)
    row_head = lax.broadcasted_iota(jnp.int32, (R, GROUP), 0) // T
    qbd = jnp.where(_lane_group((R, GROUP), HEAD_DIM) == row_head, q_rows, 0.0)
    q_hi, q_lo = _split(qbd)
    raw = []
    for kp in k_pages:
        k_hi, k_lo = _split(kp[...])
        raw.append(_dot(q_hi, k_hi) + _dot(q_hi, k_lo) + _dot(q_lo, k_hi))
    pad = jnp.zeros((PAGE - T, GROUP), F32)
    k_new = jnp.concatenate([k_in, pad], axis=0).astype(BF16)
    return raw, _dot_nt(q_hi, k_new)


def _sample_softmax(raw, raw_new, t_new):
    T = t_new
    R = N_HEADS * T
    n_pages = len(raw)
    per_blk = MOBA_BLOCK // PAGE
    n_blk = n_pages // per_blk
    past = n_pages * PAGE
    gates = [jnp.sum(sum(raw[n * per_blk:(n + 1) * per_blk]), axis=-1, keepdims=True) for n in range(n_blk)]
    biases = []
    for n in range(n_blk):
        rank = jnp.zeros((R, 1), F32)
        for mth in range(n_blk):
            if mth != n:
                beats = (gates[mth] >= gates[n]) if mth < n else (gates[mth] > gates[n])
                rank = rank + jnp.where(beats, 1.0, 0.0)
        biases.append(jnp.where(rank < MOBA_TOP_K, 0.0, NEG_INF))

    r1 = lax.broadcasted_iota(jnp.int32, (R, 1), 0)
    slope = jnp.zeros((R, 1), F32)
    for h in range(N_HEADS):
        slope = jnp.where(r1 // T == h, 2.0 ** (-8.0 * (h + 1) / N_HEADS), slope)
    t_row = r1 % T
    pos_q = (past + t_row).astype(F32)
    col = lax.broadcasted_iota(jnp.int32, (R, PAGE), 1)
    scores = []
    for pg in range(n_pages):
        key_pos = (col + pg * PAGE).astype(F32)
        scores.append(raw[pg] - slope * (pos_q - key_pos) + biases[pg // per_blk])
    s_new = raw_new - slope * (t_row - col).astype(F32)
    scores.append(jnp.where(col <= t_row, s_new, NEG_INF))

    m = functools.reduce(jnp.maximum, [jnp.max(s, axis=-1, keepdims=True) for s in scores])
    ps = [jnp.exp(s - m) for s in scores]
    l = functools.reduce(lambda a, b: a + b, [jnp.sum(p, axis=-1, keepdims=True) for p in ps])
    return ps, l


def _sample_values(ps, l, v_in, v_pages, t_new):
    T = t_new
    pad = jnp.zeros((PAGE - T, GROUP), F32)
    v_new = jnp.concatenate([v_in, pad], axis=0).astype(BF16)
    acc = _dot(ps[-1].astype(BF16), v_new)
    for p, vp in zip(ps[:-1], v_pages):
        acc = acc + _dot_nt(p.astype(BF16), vp[...].astype(BF16))
    acc = acc / l
    lane_head = _lane_group((T, GROUP), HEAD_DIM)
    out = jnp.zeros((T, GROUP), F32)
    for h in range(N_HEADS):
        out = jnp.where(lane_head == h, acc[h * T:(h + 1) * T, :], out)
    return out


def _attn_sample(q, k_new, v_new, cache_k, cache_v, page_table, page_base):
    db, t, g = q.shape
    n_pages = page_table.shape[1]
    n_seq = SAMPLE_SEQS if db % SAMPLE_SEQS == 0 else 1
    tok = pl.BlockSpec((n_seq, t, g), lambda b, pt: (b, 0, 0))
    hbm = pl.BlockSpec(memory_space=pl.ANY)
    page_buf = pltpu.VMEM((2, n_seq * n_pages, g, PAGE), F32)
    return pl.pallas_call(
        functools.partial(_attn_sample_kernel, n_pages=n_pages, t_new=t, page_base=page_base, n_seq=n_seq),
        grid_spec=pltpu.PrefetchScalarGridSpec(
            num_scalar_prefetch=1,
            grid=(db // n_seq,),
            in_specs=[tok, tok, tok, hbm, hbm],
            out_specs=tok,
            scratch_shapes=[page_buf, page_buf, pltpu.SemaphoreType.DMA((2, 2))],
        ),
        out_shape=jax.ShapeDtypeStruct((db, t, g), F32),
        compiler_params=_cparams(("arbitrary",)),
        name="attn_sample",
    )(page_table, q, k_new, v_new, cache_k, cache_v)


def _layernorm_silu(y, g, b):
    mu = jnp.mean(y, axis=-1, keepdims=True)
    yc = y - mu
    var = jnp.mean(yc * yc, axis=-1, keepdims=True)
    yn = yc * lax.rsqrt(var + LN_EPS) * g + b
    return yn * jax.nn.sigmoid(yn)


def _pool_window_select(sums, shape):
    grp = _lane_group(shape, GROUP // len(POOL_WINDOWS))
    sel = sums[POOL_WINDOWS[-1]]
    for gi in range(len(POOL_WINDOWS) - 2, -1, -1):
        sel = jnp.where(grp == gi, sums[POOL_WINDOWS[gi]], sel)
    win = jnp.left_shift(jnp.int32(POOL_WINDOWS[0]), grp)
    return sel, win


MIX_ROWS = 512
HALO = 32


def _mix_prompt_kernel(pp_ref, p_ref, ap_ref, a_ref, u_ref, vg_ref, poolw_ref, pscale_ref,
                       convw_ref, convb_ref, lng_ref, lnb_ref, convpw_ref, sw_ref, sb_ref,
                       pool_o, conv_o, gmlp_o, p_scr, a_scr):
    c = pl.program_id(1)
    C = CHUNK
    has_prev = c > 0
    p_scr[0:HALO, :] = jnp.where(has_prev, pp_ref[0], 0.0)
    p_scr[HALO:HALO + MIX_ROWS, :] = p_ref[0]
    a_scr[0:HALO, :] = jnp.where(has_prev, ap_ref[0], 0.0)
    a_scr[HALO:HALO + MIX_ROWS, :] = a_ref[0]
    tri = lax.broadcasted_iota(jnp.int32, (C, C), 0) >= lax.broadcasted_iota(jnp.int32, (C, C), 1)
    w_causal = [jnp.where(tri, sw_ref[g], 0.0).astype(BF16) for g in range(N_HEADS)]
    grp = _lane_group((C, GROUP), HEAD_DIM)

    for sub in range(MIX_ROWS // C):
        s0 = HALO + sub * C
        rows = slice(sub * C, (sub + 1) * C)

        w2 = p_scr[s0 - 24:s0 + C, :] + p_scr[s0 - 25:s0 + C - 1, :]
        w4 = w2[8:C + 24] + w2[6:C + 22]
        w8 = w4[8:C + 16] + w4[4:C + 12]
        w16 = w8[8:C + 8] + w8[0:C]
        sums = {2: w2[24:C + 24], 4: w4[16:C + 16], 8: w8[8:C + 8], 16: w16}
        sel, win = _pool_window_select(sums, (C, GROUP))
        t_glob = c * MIX_ROWS + sub * C + lax.broadcasted_iota(jnp.int32, (C, GROUP), 0)
        count = jnp.minimum(t_glob + 1, win).astype(F32)
        m = sel / count - p_scr[s0:s0 + C, :]
        pool_o[0, rows, :] = _dot(m.astype(BF16), poolw_ref[...]) * pscale_ref[...]

        base = s0 - CONV_HIST
        y = jnp.zeros((C, GROUP), F32)
        for r in range(8):
            taps = [j for j in range(CONV_WIDTH) if (base + j) % 8 == r]
            n_rows = C + (8 if r else 0)
            z = jnp.zeros((n_rows, GROUP), F32)
            for j in taps:
                start = base + j - r
                z = z + a_scr[start:start + n_rows, :] * convw_ref[j:j + 1, :]
            y = y + z[r:r + C]
        y = _layernorm_silu(y + convb_ref[...], lng_ref[...], lnb_ref[...])
        conv_o[0, rows, :] = _dot(y.astype(BF16), convpw_ref[...])

        vgb = vg_ref[0, rows, :].astype(BF16)
        s = sb_ref[...]
        for g in range(N_HEADS):
            s = s + jnp.where(grp == g, _dot(w_causal[g], vgb), 0.0)
        gmlp_o[0, rows, :] = u_ref[0, rows, :] * s


def _mix_prompt(p, a, u, vg, poolw, pscale, convw, convb, lng, lnb, convpw, sw, sb):
    b, t, g = p.shape
    assert POOL_WINDOWS == (2, 4, 8, 16) and t % MIX_ROWS == 0 and MIX_ROWS % CHUNK == 0 and HALO >= CONV_HIST
    per = MIX_ROWS // HALO
    cur = pl.BlockSpec((1, MIX_ROWS, g), lambda bi, c: (bi, c, 0))
    prev = pl.BlockSpec((1, HALO, g), lambda bi, c: (bi, jnp.maximum(c * per - 1, 0), 0))

    def full(arr):
        nd = arr.ndim
        return pl.BlockSpec(arr.shape, lambda bi, c: (0,) * nd)

    out = jax.ShapeDtypeStruct((b, t, g), F32)
    weights = (poolw, pscale, convw, convb, lng, lnb, convpw, sw, sb)
    scratch = pltpu.VMEM((HALO + MIX_ROWS, g), F32)
    return pl.pallas_call(
        _mix_prompt_kernel,
        grid=(b, t // MIX_ROWS),
        in_specs=[prev, cur, prev, cur, cur, cur] + [full(w) for w in weights],
        out_specs=[cur] * 3,
        out_shape=[out] * 3,
        scratch_shapes=[scratch, scratch],
        compiler_params=_cparams(("parallel", "arbitrary")),
        name="mix_prompt",
    )(p, p, a, a, u, vg, *weights)


def _mix_sample_kernel(hp_ref, p_ref, hc_ref, a_ref, u_ref, vg_ref, poolw_ref, pscale_ref,
                       convw_ref, convb_ref, lng_ref, lnb_ref, convpw_ref, sw_ref, sb_ref,
                       pool_o, conv_o, gmlp_o, *, t_new):
    T = t_new
    nb = p_ref.shape[1]
    shape = (nb, GROUP)

    def pool_row(k):
        return hp_ref[k] if k < POOL_HIST else p_ref[k - POOL_HIST]

    def conv_row(k):
        return hc_ref[k] if k < CONV_HIST else a_ref[k - CONV_HIST]

    for t in range(T):
        acc = jnp.zeros(shape, F32)
        sums = {}
        for i in range(POOL_WINDOWS[-1]):
            k = POOL_HIST + t - i
            if k >= 0:
                acc = acc + pool_row(k)
            if i + 1 in POOL_WINDOWS:
                sums[i + 1] = acc
        sel, win = _pool_window_select(sums, shape)
        count = jnp.minimum(POOL_HIST + t + 1, win).astype(F32)
        m = sel / count - p_ref[t]
        pool_o[t] = _dot(m.astype(BF16), poolw_ref[...]) * pscale_ref[...]

        y = jnp.zeros(shape, F32)
        for j in range(CONV_WIDTH):
            y = y + conv_row(t + j) * convw_ref[j:j + 1, :]
        y = _layernorm_silu(y + convb_ref[...], lng_ref[...], lnb_ref[...])
        conv_o[t] = _dot(y.astype(BF16), convpw_ref[...])

        s = jnp.broadcast_to(sb_ref[t:t + 1, :], shape)
        for j in range(t + 1):
            s = s + sw_ref[t, j:j + 1, :] * vg_ref[j]
        gmlp_o[t] = u_ref[t] * s


def _mix_sample(hp, p, hc, a, u, vg, poolw, pscale, convw, convb, lng, lnb, convpw, sw, sb):
    t, db, g = p.shape

    def full(arr):
        nd = arr.ndim
        return pl.BlockSpec(arr.shape, lambda i: (0,) * nd)

    args = (hp, p, hc, a, u, vg, poolw, pscale, convw, convb, lng, lnb, convpw, sw, sb)
    out = jax.ShapeDtypeStruct((t, db, g), F32)
    return pl.pallas_call(
        functools.partial(_mix_sample_kernel, t_new=t),
        grid=(1,),
        in_specs=[full(x) for x in args],
        out_specs=[full(out)] * 3,
        out_shape=[out] * 3,
        compiler_params=_cparams(("arbitrary",)),
        name="mix_sample",
    )(*args)


def _row_tile(n, cap):
    t = min(n, cap)
    while n % t:
        t //= 2
    return t


def _ffn_tile(f):
    for tf in (512, 256, 128):
        if f % tf == 0:
            return tf
    return f


def _layer_weights(l, ffn1_norm, ffn1_w_gate, ffn1_w_up, ffn1_w_down, mix_norm, w_in, q_norm, k_norm,
                   pool_w, pool_scale, conv_w, conv_b, conv_ln_g, conv_ln_b, conv_pw,
                   gate_norm, spatial_w, spatial_b, w_out, ffn2_norm, ffn2_w_gate, ffn2_w_up, ffn2_w_down):
    row = lambda v: v[l].reshape(1, -1)
    head_id = jnp.arange(GROUP) // HEAD_DIM
    seg = jnp.where(head_id[:, None] == head_id[None, :], 1.0 / HEAD_DIM, 0.0).astype(BF16)
    n_win = len(POOL_WINDOWS)
    pg = GROUP // n_win
    pool_bd = jnp.zeros((GROUP, GROUP), F32)
    for gi in range(n_win):
        pool_bd = pool_bd.at[gi * pg:(gi + 1) * pg, gi * pg:(gi + 1) * pg].set(pool_w[l, gi])
    sb_lanes = jnp.repeat(spatial_b[l].T, HEAD_DIM, axis=1)
    return dict(
        layer=l,
        ffn1=(row(ffn1_norm), ffn1_w_gate, ffn1_w_up, ffn1_w_down),
        ffn2=(row(ffn2_norm), ffn2_w_gate, ffn2_w_up, ffn2_w_down),
        proj_in=(row(mix_norm), w_in[l].astype(BF16), jnp.tile(q_norm[l], N_HEADS).reshape(1, -1),
                 jnp.tile(k_norm[l], N_HEADS).reshape(1, -1), row(gate_norm), seg),
        w_out=w_out[l].astype(BF16),
        mix=(pool_bd.astype(BF16), row(pool_scale), conv_w[l], row(conv_b), row(conv_ln_g), row(conv_ln_b),
             conv_pw[l].astype(BF16)),
        spatial_w=spatial_w[l],
        spatial_b_lanes=sb_lanes,
    )


def _token_stage_in(x, w, kv_stack=None):
    n = x.shape[0]
    x1 = _ffn(x, (), None, *w["ffn1"], layer=w["layer"], tm=_row_tile(n, FFN_ROWS),
              tf=_ffn_tile(w["ffn1"][1].shape[2]))
    tm = _row_tile(n if kv_stack is None else kv_stack[3], PROJ_ROWS)
    return x1, _proj_in(x1, *w["proj_in"], tm=tm, kv_stack=kv_stack)


def _token_stage_out(x1, att, pool, conv, gmlp, w):
    n = x1.shape[0]
    return _ffn(x1, (att, pool, conv, gmlp), w["w_out"], *w["ffn2"], layer=w["layer"], tm=_row_tile(n, FFN_ROWS),
                tf=_ffn_tile(w["ffn2"][1].shape[2]))


def _prompt_layer(x, w, depth, kv_prev):
    b, t, d = x.shape
    x1, (q, kt, vt, p, a, u, vg) = _token_stage_in(x.reshape(b * t, d), w, (w["layer"], depth, b, t, kv_prev))
    r3 = lambda arr: arr.reshape(b, t, GROUP)
    q, p, a, u, vg = map(r3, (q, p, a, u, vg))
    att = _attn_prompt(q, kt, vt, w["layer"])
    pool, conv, gmlp = _mix_prompt(p, a, u, vg, *w["mix"], w["spatial_w"], w["spatial_b_lanes"])
    flat = lambda arr: arr.reshape(b * t, GROUP)
    y = _token_stage_out(x1, flat(att), flat(pool), flat(conv), flat(gmlp), w)
    return y.reshape(b, t, d), (kt, vt), p[:, -POOL_HIST:], a[:, -CONV_HIST:]


def _sample_layer(x, cache_k, cache_v, page_table, page_base, hist_pool, hist_conv, w):
    db, t, d = x.shape
    x1, (q, k, v, p, a, u, vg) = _token_stage_in(x.reshape(db * t, d), w)
    r3 = lambda arr: arr.reshape(db, t, GROUP)
    q, k, v, p, a, u, vg = map(r3, (q, k, v, p, a, u, vg))
    att = _attn_sample(q, k, v, cache_k, cache_v, page_table, page_base)
    tm = lambda arr: jnp.swapaxes(arr, 0, 1)
    sw_lanes = jnp.repeat(jnp.transpose(w["spatial_w"][:, :t, :t], (1, 2, 0)), HEAD_DIM, axis=2)
    hp_t, p_t, hc_t, a_t = tm(hist_pool), tm(p), tm(hist_conv), tm(a)
    pool, conv, gmlp = _mix_sample(hp_t, p_t, hc_t, a_t, tm(u), tm(vg),
                                   *w["mix"], sw_lanes, w["spatial_b_lanes"][:t])
    flat = lambda arr: tm(arr).reshape(db * t, GROUP)
    y = _token_stage_out(x1, att.reshape(db * t, GROUP), flat(pool), flat(conv), flat(gmlp), w)
    pool_state = tm(jnp.concatenate([hp_t, p_t], axis=0)[-POOL_HIST:])
    conv_state = tm(jnp.concatenate([hc_t, a_t], axis=0)[-CONV_HIST:])
    return y.reshape(db, t, d), k, v, pool_state, conv_state, vg


def kernel(x_prompt, x_sample, cache_k, cache_v, state_pool, state_conv, page_table, ffn1_norm, ffn1_w_gate, ffn1_w_up, ffn1_w_down, mix_norm, w_in, q_norm, k_norm, pool_w, pool_scale, conv_w, conv_b, conv_ln_g, conv_ln_b, conv_pw, gate_norm, spatial_w, spatial_b, w_out, ffn2_norm, ffn2_w_gate, ffn2_w_up, ffn2_w_down):
    depth, n_pool = cache_k.shape[0], cache_k.shape[1]
    b, t = x_prompt.shape[0], x_prompt.shape[1]
    db, ts = x_sample.shape[0], x_sample.shape[1]
    ck = jnp.transpose(cache_k, (0, 1, 3, 4, 2)).reshape(depth * n_pool, GROUP, PAGE)
    cv = jnp.transpose(cache_v, (0, 1, 3, 4, 2)).reshape(depth * n_pool, GROUP, PAGE)
    hp, hs = x_prompt, x_sample
    outs = [[] for _ in range(7)]
    kv_prompt = None
    for l in range(depth):
        w = _layer_weights(l, ffn1_norm, ffn1_w_gate, ffn1_w_up, ffn1_w_down, mix_norm, w_in, q_norm, k_norm,
                           pool_w, pool_scale, conv_w, conv_b, conv_ln_g, conv_ln_b, conv_pw,
                           gate_norm, spatial_w, spatial_b, w_out, ffn2_norm, ffn2_w_gate, ffn2_w_up, ffn2_w_down)
        hp, kv_prompt, pp, cp = _prompt_layer(hp, w, depth, kv_prompt)
        hs, ks, vs, ps, cs, gs = _sample_layer(hs, ck, cv, page_table, l * n_pool, state_pool[l], state_conv[l], w)
        heads = lambda arr: arr.reshape(db, ts, N_HEADS, HEAD_DIM)
        for lst, val in zip(outs, (heads(ks), heads(vs), pp, ps, cp, cs, gs)):
            lst.append(val)
    k_prompt, v_prompt = (jnp.transpose(s.reshape(depth, b, N_HEADS, HEAD_DIM, t), (0, 1, 4, 2, 3)) for s in kv_prompt)
    return (hp, hs, k_prompt, v_prompt) + tuple(jnp.stack(lst) for lst in outs)
```

```python
import functools

import jax
import jax.numpy as jnp
from jax import lax
from jax.experimental import pallas as pl
from jax.experimental.pallas import tpu as pltpu

F32 = jnp.float32
BF16 = jnp.bfloat16
NEG_INF = float("-inf")

GROUP = 256
HEAD_DIM = 64
N_HEADS = GROUP // HEAD_DIM
MOBA_BLOCK = 256
MOBA_TOP_K = 3
PAGE = 128
POOL_WINDOWS = (2, 4, 8, 16)
POOL_HIST = 15
CONV_WIDTH = 31
CONV_HIST = 30
CHUNK = 128
FFN_ROWS = 512
PROJ_ROWS = 1024
SAMPLE_SEQS = 4
RMS_EPS = 1e-6
LN_EPS = 1e-5
VMEM_LIMIT = 56 * 1024 * 1024


def _cparams(sem):
    return pltpu.CompilerParams(dimension_semantics=sem, vmem_limit_bytes=VMEM_LIMIT)


def _rms(x, g):
    ms = jnp.mean(x * x, axis=-1, keepdims=True)
    return x * lax.rsqrt(ms + RMS_EPS) * g


def _dot(a, b):
    return jnp.dot(a, b, preferred_element_type=F32)


def _dot_nt(a, b):
    return lax.dot_general(a, b, (((1,), (1,)), ((), ())), preferred_element_type=F32)


def _split(a):
    hi = a.astype(BF16)
    lo = (a - hi.astype(F32)).astype(BF16)
    return hi, lo


def _dot3_nt(a, b):
    ah, al = _split(a)
    bh, bl = _split(b)
    return _dot_nt(ah, bh) + _dot_nt(ah, bl) + _dot_nt(al, bh)


def _lane_group(shape, width):
    return lax.broadcasted_iota(jnp.int32, shape, len(shape) - 1) // width


def _ffn_kernel(*refs, n_f, tf, n_mix, layer):
    x_ref = refs[0]
    mix_refs = refs[1:1 + n_mix]
    rest = refs[1 + n_mix:]
    if n_mix:
        wo_ref, rest = rest[0], rest[1:]
    g_ref, wg_hbm, wu_hbm, wd_hbm, o_ref, wg_b, wu_b, wd_b, gu_stage, d_stage, sems, acc_scr = rest
    first = pl.program_id(0) == 0

    def chunk_copies(c):
        slot = c % 2
        cols = pl.ds(c * tf, tf)
        return (pltpu.make_async_copy(wg_hbm.at[layer, :, cols], gu_stage.at[slot, 0], sems.at[slot, 0]),
                pltpu.make_async_copy(wu_hbm.at[layer, :, cols], gu_stage.at[slot, 1], sems.at[slot, 1]),
                pltpu.make_async_copy(wd_hbm.at[layer, cols, :], d_stage.at[slot], sems.at[slot, 2]))

    @pl.when(first)
    def _():
        for cp in chunk_copies(0):
            cp.start()

    x = x_ref[...]
    for idx, m_ref in enumerate(mix_refs):
        x = x + _dot(m_ref[...].astype(BF16), wo_ref[idx * GROUP:(idx + 1) * GROUP, :])
    h = _rms(x, g_ref[...]).astype(BF16)

    def land_chunk(c):
        cols = slice(c * tf, (c + 1) * tf)
        if c + 1 < n_f:
            for cp in chunk_copies(c + 1):
                cp.start()
        for cp in chunk_copies(c):
            cp.wait()
        wg_b[:, cols] = gu_stage[c % 2, 0].astype(BF16)
        wu_b[:, cols] = gu_stage[c % 2, 1].astype(BF16)
        wd_b[cols, :] = d_stage[c % 2].astype(BF16)

    def swiglu(fetch):
        for c in range(n_f):
            cols = slice(c * tf, (c + 1) * tf)
            fetch(c)
            gate = _dot(h, wg_b[:, cols])
            up = _dot(h, wu_b[:, cols])
            act = (gate * jax.nn.sigmoid(gate) * up).astype(BF16)
            part = _dot(act, wd_b[cols, :])
            if c == 0:
                acc_scr[...] = part
            else:
                acc_scr[...] += part
        o_ref[...] = x + 0.5 * acc_scr[...]

    @pl.when(first)
    def _():
        swiglu(land_chunk)

    @pl.when(jnp.logical_not(first))
    def _():
        swiglu(lambda c: None)


def _ffn(x, mixes, w_out, norm, wg, wu, wd, layer, tm, tf):
    n, d = x.shape
    f = wg.shape[2]
    n_mix = len(mixes)
    row = lambda i: (i, 0)
    hbm = pl.BlockSpec(memory_space=pl.ANY)
    in_specs = [pl.BlockSpec((tm, d), row)] + [pl.BlockSpec((tm, GROUP), row)] * n_mix
    args = [x, *mixes]
    if n_mix:
        in_specs.append(pl.BlockSpec(w_out.shape, lambda i: (0, 0)))
        args.append(w_out)
    in_specs += [pl.BlockSpec((1, d), lambda i: (0, 0)), hbm, hbm, hbm]
    args += [norm, wg, wu, wd]
    return pl.pallas_call(
        functools.partial(_ffn_kernel, n_f=f // tf, tf=tf, n_mix=n_mix, layer=layer),
        grid=(n // tm,),
        in_specs=in_specs,
        out_specs=pl.BlockSpec((tm, d), row),
        out_shape=jax.ShapeDtypeStruct((n, d), F32),
        scratch_shapes=[pltpu.VMEM((d, f), BF16), pltpu.VMEM((d, f), BF16), pltpu.VMEM((f, d), BF16),
                        pltpu.VMEM((2, 2, d, tf), F32), pltpu.VMEM((2, tf, d), F32),
                        pltpu.SemaphoreType.DMA((2, 3)), pltpu.VMEM((tm, d), F32)],
        compiler_params=_cparams(("arbitrary",)),
        name="ffn_out" if n_mix else "ffn",
    )(*args)


def _proj_in_kernel(*refs, kv_transposed, n_alias):
    x_ref, g_ref, w_ref, qn_ref, kn_ref, gn_ref, seg_ref = refs[:7]
    q_o, k_o, v_o, p_o, a_o, u_o, vg_o = refs[7 + n_alias:]
    G = GROUP
    h = _rms(x_ref[...], g_ref[...]).astype(BF16)
    z = _dot(h, w_ref[...])
    seg = seg_ref[...]

    def head_rms(t, gain):
        sq = t * t
        hi, lo = _split(sq)
        ms = _dot(hi, seg) + _dot(lo, seg)
        return t * lax.rsqrt(ms + RMS_EPS) * gain

    q_o[...] = head_rms(z[:, 0:G], qn_ref[...])
    k = head_rms(z[:, G:2 * G], kn_ref[...])
    v = z[:, 2 * G:3 * G]
    if not kv_transposed:
        k_o[...] = k
        v_o[...] = v
    elif len(k_o.shape) == 2:
        k_o[...] = k.T
        v_o[...] = v.T
    else:
        k_o[0] = k.T
        v_o[0] = v.T
        for slot in range(1, k_o.shape[0]):
            k_o[slot] = jnp.zeros(k_o.shape[1:], F32)
            v_o[slot] = jnp.zeros(v_o.shape[1:], F32)
    p_o[...] = z[:, 3 * G:4 * G]
    a_o[...] = z[:, 4 * G:5 * G] * jax.nn.sigmoid(z[:, 5 * G:6 * G])
    d = jax.nn.gelu(z[:, 6 * G:8 * G])
    u_o[...] = d[:, 0:G]
    vg_o[...] = _rms(d[:, G:2 * G], gn_ref[...])


def _proj_in(x, norm, w_in, qn, kn, gn, seg, tm, kv_stack=None):
    n, d = x.shape
    cols = w_in.shape[1]
    row = lambda i: (i, 0)
    fixed = lambda i: (0, 0)
    out = jax.ShapeDtypeStruct((n, GROUP), F32)
    in_specs = [pl.BlockSpec((tm, d), row),
                pl.BlockSpec((1, d), fixed),
                pl.BlockSpec((d, cols), fixed),
                pl.BlockSpec((1, GROUP), fixed),
                pl.BlockSpec((1, GROUP), fixed),
                pl.BlockSpec((1, GROUP), fixed),
                pl.BlockSpec((GROUP, GROUP), fixed)]
    args = [x, norm, w_in, qn, kn, gn, seg]
    out_specs = [pl.BlockSpec((tm, GROUP), row)] * 7
    out_shape = [out] * 7
    aliases = {}
    if kv_stack is not None:
        layer, depth, b, t, prev = kv_stack
        per_b = t // tm
        if prev is None:
            assert layer == 0
            kv_spec = pl.BlockSpec((depth, None, GROUP, tm), lambda i: (0, i // per_b, 0, i % per_b))
        else:
            kv_spec = pl.BlockSpec((None, None, GROUP, tm), lambda i: (layer, i // per_b, 0, i % per_b))
        out_specs[1:3] = [kv_spec, kv_spec]
        out_shape[1:3] = [jax.ShapeDtypeStruct((depth, b, GROUP, t), F32)] * 2
        if prev is not None:
            in_specs += [pl.BlockSpec(memory_space=pl.ANY)] * 2
            aliases = {len(args): 1, len(args) + 1: 2}
            args += list(prev)
    return pl.pallas_call(
        functools.partial(_proj_in_kernel, kv_transposed=kv_stack is not None, n_alias=len(aliases)),
        grid=(n // tm,),
        in_specs=in_specs,
        out_specs=out_specs,
        out_shape=out_shape,
        input_output_aliases=aliases,
        compiler_params=_cparams(("parallel",)),
        name="proj_in",
    )(*args)


GATE_ROWS = 8
VT_ROWS = HEAD_DIM + 16


def _select_bias_rows(gate, n_valid):
    rows = lax.broadcasted_iota(jnp.int32, gate.shape, 0)
    rows_f = rows.astype(F32)
    cur = jnp.where(rows < n_valid, gate, NEG_INF)
    bias = jnp.where(rows == n_valid, 0.0, NEG_INF)
    for _ in range(MOBA_TOP_K):
        top = jnp.max(cur, axis=0, keepdims=True)
        first = jnp.min(jnp.where(cur == top, rows_f, float(gate.shape[0])), axis=0, keepdims=True)
        pick = (rows_f == first) & (top > NEG_INF)
        bias = jnp.where(pick, 0.0, bias)
        cur = jnp.where(pick, NEG_INF, cur)
    return bias


def _attn_prompt_kernel(q_ref, k_ref, v_ref, o_ref, kb_scr, vt_scr, kbd_scr, bias_scr, qk_scr, alibi_scr,
                        *, n_blk):
    B = MOBA_BLOCK
    lane_head = _lane_group((1, GROUP), HEAD_DIM)
    slopes = [2.0 ** (-8.0 * (h + 1) / N_HEADS) for h in range(N_HEADS)]

    neg_rel = (lax.broadcasted_iota(jnp.int32, (B, B), 0)
               - lax.broadcasted_iota(jnp.int32, (B, B), 1)).astype(F32)
    for h in range(N_HEADS):
        alibi_scr[h] = slopes[h] * neg_rel
        alibi_scr[N_HEADS + h] = jnp.where(neg_rel <= 0.0, slopes[h] * neg_rel, NEG_INF)
    kbd_scr[...] = jnp.zeros_like(kbd_scr)
    ones = jnp.ones((VT_ROWS - HEAD_DIM, B), BF16)
    for n in range(n_blk):
        kn = k_ref[:, n * B:(n + 1) * B].T
        kb_scr[n] = kn.astype(BF16)
        mean = jnp.mean(kn, axis=0, keepdims=True)
        vt = v_ref[:, n * B:(n + 1) * B]
        for h in range(N_HEADS):
            r = h * GATE_ROWS + n
            kbd_scr[r:r + 1, :] = jnp.where(lane_head == h, mean, 0.0)
            vt_scr[n, h, 0:HEAD_DIM, :] = vt[h * HEAD_DIM:(h + 1) * HEAD_DIM, :].astype(BF16)
            vt_scr[n, h, HEAD_DIM:VT_ROWS, :] = ones

    def query_block(i, _):
        rows = pl.ds(pl.multiple_of(i * B, B), B)
        o_ref[0, rows, :] = _attend_block(i, q_ref[0, rows, :], slopes, kb_scr, vt_scr, kbd_scr, bias_scr,
                                          qk_scr, alibi_scr)
        return 0

    lax.fori_loop(0, n_blk, query_block, 0)


def _attend_block(i, q, slopes, kb_scr, vt_scr, kbd_scr, bias_scr, qk_scr, alibi_scr):
    B = MOBA_BLOCK
    gate = _dot3_nt(kbd_scr[...], q)
    for h in range(N_HEADS):
        rows = slice(h * GATE_ROWS, (h + 1) * GATE_ROWS)
        bias_scr[rows, :] = _select_bias_rows(gate[rows, :], i)

    q_scaled = q * (HEAD_DIM ** -0.5)
    in_head = _lane_group((B, GROUP), HEAD_DIM)
    qh = [jnp.where(in_head == h, q_scaled, 0.0).astype(BF16) for h in range(N_HEADS)]

    def scores(blk):
        kn = kb_scr[blk]
        return tuple(_dot_nt(kn, qh[h]) for h in range(N_HEADS))

    def absorb(blk, ms, accs):
        causal = jnp.where(blk == i, N_HEADS, 0)
        dist = ((i - blk) * B).astype(F32)
        ss, new_ms, ps = [], [], []
        for h in range(N_HEADS):
            bias = bias_scr[pl.ds(h * GATE_ROWS + blk, 1), :] - slopes[h] * dist
            ss.append(qk_scr[h] + alibi_scr[h + causal] + bias)
            new_ms.append(jnp.maximum(ms[h], jnp.max(ss[h], axis=0, keepdims=True)))
        for h in range(N_HEADS):
            ps.append(jnp.exp(ss[h] - new_ms[h]).astype(BF16))
        new_accs = [jnp.exp(ms[h] - new_ms[h]) * accs[h] + _dot(vt_scr[blk, h], ps[h])
                    for h in range(N_HEADS)]
        return tuple(new_ms), tuple(new_accs)

    def stash(qk):
        for h in range(N_HEADS):
            qk_scr[h] = qk[h]

    def body(j, carry):
        qk_next = scores(j)
        carry = absorb(jnp.where(j == 0, i, j - 1), *carry)
        stash(qk_next)
        return carry

    stash(scores(i))
    init = ((jnp.full((1, B), NEG_INF, F32),) * N_HEADS, (jnp.zeros((VT_ROWS, B), F32),) * N_HEADS)
    ms, accs = lax.fori_loop(0, i, body, init)
    ms, accs = absorb(jnp.where(i == 0, i, i - 1), ms, accs)
    out_t = jnp.concatenate([a[0:HEAD_DIM, :] / a[HEAD_DIM:HEAD_DIM + 1, :] for a in accs], axis=0)
    return out_t.T


def _attn_prompt(q, kt, vt, layer):
    b, t, g = q.shape
    n_blk = t // MOBA_BLOCK
    assert t % MOBA_BLOCK == 0 and n_blk <= GATE_ROWS
    kv_spec = pl.BlockSpec((None, None, g, t), lambda bi: (layer, bi, 0, 0))
    seq_spec = pl.BlockSpec((1, t, g), lambda bi: (bi, 0, 0))
    return pl.pallas_call(
        functools.partial(_attn_prompt_kernel, n_blk=n_blk),
        grid=(b,),
        in_specs=[seq_spec, kv_spec, kv_spec],
        out_specs=seq_spec,
        out_shape=jax.ShapeDtypeStruct((b, t, g), F32),
        scratch_shapes=[pltpu.VMEM((n_blk, MOBA_BLOCK, g), BF16),
                        pltpu.VMEM((n_blk, N_HEADS, VT_ROWS, MOBA_BLOCK), BF16),
                        pltpu.VMEM((N_HEADS * GATE_ROWS, g), F32),
                        pltpu.VMEM((N_HEADS * GATE_ROWS, MOBA_BLOCK), F32),
                        pltpu.VMEM((N_HEADS, MOBA_BLOCK, MOBA_BLOCK), F32),
                        pltpu.VMEM((2 * N_HEADS, MOBA_BLOCK, MOBA_BLOCK), F32)],
        compiler_params=_cparams(("parallel",)),
        name="attn_prompt",
    )(q, kt, vt)


def _attn_sample_kernel(pt_ref, q_ref, kn_ref, vn_ref, ck_hbm, cv_hbm, o_ref, k_buf, v_buf, sems,
                        *, n_pages, t_new, page_base, n_seq):
    b = pl.program_id(0)

    def page_copies(step, slot):
        out = []
        for s in range(n_seq):
            for pg in range(n_pages):
                page = page_base + pt_ref[step * n_seq + s, pg]
                dst = s * n_pages + pg
                out.append(pltpu.make_async_copy(ck_hbm.at[page], k_buf.at[slot, dst], sems.at[0, slot]))
                out.append(pltpu.make_async_copy(cv_hbm.at[page], v_buf.at[slot, dst], sems.at[1, slot]))
        return out

    @pl.when(b == 0)
    def _():
        for cp in page_copies(0, 0):
            cp.start()

    @pl.when(b + 1 < pl.num_programs(0))
    def _():
        for cp in page_copies(b + 1, (b + 1) % 2):
            cp.start()

    slot = b % 2
    for cp in page_copies(b, slot):
        cp.wait()
    seqs = []
    for s in range(n_seq):
        k_pages = [k_buf.at[slot, s * n_pages + pg] for pg in range(n_pages)]
        v_pages = [v_buf.at[slot, s * n_pages + pg] for pg in range(n_pages)]
        seqs.append((q_ref[s], kn_ref[s], vn_ref[s], k_pages, v_pages))
    scored = [_sample_scores(q, k_new, k_pages, t_new) for q, k_new, _, k_pages, _ in seqs]
    probs = [_sample_softmax(*sc, t_new) for sc in scored]
    for s, ((_, _, v_new, _, v_pages), (ps, l)) in enumerate(zip(seqs, probs)):
        o_ref[s] = _sample_values(ps, l, v_new, v_pages, t_new)


def _sample_scores(q_in, k_in, k_pages, t_new):
    T = t_new
    R = N_HEADS * T
    q = q_in * HEAD_DIM ** -0.5
    q_rows = jnp.concatenate([q] * N_HEADS, axis=0)
    row_head = lax.broadcasted_iota(jnp.int32, (R, GROUP), 0) // T
    qbd = jnp.where(_lane_group((R, GROUP), HEAD_DIM) == row_head, q_rows, 0.0)
    q_hi, q_lo = _split(qbd)
    raw = []
    for kp in k_pages:
        k_hi, k_lo = _split(kp[...])
        raw.append(_dot(q_hi, k_hi) + _dot(q_hi, k_lo) + _dot(q_lo, k_hi))
    pad = jnp.zeros((PAGE - T, GROUP), F32)
    k_new = jnp.concatenate([k_in, pad], axis=0).astype(BF16)
    return raw, _dot_nt(q_hi, k_new)


def _sample_softmax(raw, raw_new, t_new):
    T = t_new
    R = N_HEADS * T
    n_pages = len(raw)
    per_blk = MOBA_BLOCK // PAGE
    n_blk = n_pages // per_blk
    past = n_pages * PAGE
    gates = [jnp.sum(sum(raw[n * per_blk:(n + 1) * per_blk]), axis=-1, keepdims=True) for n in range(n_blk)]
    biases = []
    for n in range(n_blk):
        rank = jnp.zeros((R, 1), F32)
        for mth in range(n_blk):
            if mth != n:
                beats = (gates[mth] >= gates[n]) if mth < n else (gates[mth] > gates[n])
                rank = rank + jnp.where(beats, 1.0, 0.0)
        biases.append(jnp.where(rank < MOBA_TOP_K, 0.0, NEG_INF))

    r1 = lax.broadcasted_iota(jnp.int32, (R, 1), 0)
    slope = jnp.zeros((R, 1), F32)
    for h in range(N_HEADS):
        slope = jnp.where(r1 // T == h, 2.0 ** (-8.0 * (h + 1) / N_HEADS), slope)
    t_row = r1 % T
    pos_q = (past + t_row).astype(F32)
    col = lax.broadcasted_iota(jnp.int32, (R, PAGE), 1)
    scores = []
    for pg in range(n_pages):
        key_pos = (col + pg * PAGE).astype(F32)
        scores.append(raw[pg] - slope * (pos_q - key_pos) + biases[pg // per_blk])
    s_new = raw_new - slope * (t_row - col).astype(F32)
    scores.append(jnp.where(col <= t_row, s_new, NEG_INF))

    m = functools.reduce(jnp.maximum, [jnp.max(s, axis=-1, keepdims=True) for s in scores])
    ps = [jnp.exp(s - m) for s in scores]
    l = functools.reduce(lambda a, b: a + b, [jnp.sum(p, axis=-1, keepdims=True) for p in ps])
    return ps, l


def _sample_values(ps, l, v_in, v_pages, t_new):
    T = t_new
    pad = jnp.zeros((PAGE - T, GROUP), F32)
    v_new = jnp.concatenate([v_in, pad], axis=0).astype(BF16)
    acc = _dot(ps[-1].astype(BF16), v_new)
    for p, vp in zip(ps[:-1], v_pages):
        acc = acc + _dot_nt(p.astype(BF16), vp[...].astype(BF16))
    acc = acc / l
    lane_head = _lane_group((T, GROUP), HEAD_DIM)
    out = jnp.zeros((T, GROUP), F32)
    for h in range(N_HEADS):
        out = jnp.where(lane_head == h, acc[h * T:(h + 1) * T, :], out)
    return out


def _attn_sample(q, k_new, v_new, cache_k, cache_v, page_table, page_base):
    db, t, g = q.shape
    n_pages = page_table.shape[1]
    n_seq = SAMPLE_SEQS if db % SAMPLE_SEQS == 0 else 1
    tok = pl.BlockSpec((n_seq, t, g), lambda b, pt: (b, 0, 0))
    hbm = pl.BlockSpec(memory_space=pl.ANY)
    page_buf = pltpu.VMEM((2, n_seq * n_pages, g, PAGE), F32)
    return pl.pallas_call(
        functools.partial(_attn_sample_kernel, n_pages=n_pages, t_new=t, page_base=page_base, n_seq=n_seq),
        grid_spec=pltpu.PrefetchScalarGridSpec(
            num_scalar_prefetch=1,
            grid=(db // n_seq,),
            in_specs=[tok, tok, tok, hbm, hbm],
            out_specs=tok,
            scratch_shapes=[page_buf, page_buf, pltpu.SemaphoreType.DMA((2, 2))],
        ),
        out_shape=jax.ShapeDtypeStruct((db, t, g), F32),
        compiler_params=_cparams(("arbitrary",)),
        name="attn_sample",
    )(page_table, q, k_new, v_new, cache_k, cache_v)


def _layernorm_silu(y, g, b):
    mu = jnp.mean(y, axis=-1, keepdims=True)
    yc = y - mu
    var = jnp.mean(yc * yc, axis=-1, keepdims=True)
    yn = yc * lax.rsqrt(var + LN_EPS) * g + b
    return yn * jax.nn.sigmoid(yn)


def _pool_window_select(sums, shape):
    grp = _lane_group(shape, GROUP // len(POOL_WINDOWS))
    sel = sums[POOL_WINDOWS[-1]]
    for gi in range(len(POOL_WINDOWS) - 2, -1, -1):
        sel = jnp.where(grp == gi, sums[POOL_WINDOWS[gi]], sel)
    win = jnp.left_shift(jnp.int32(POOL_WINDOWS[0]), grp)
    return sel, win


MIX_ROWS = 512
HALO = 32


def _mix_prompt_kernel(pp_ref, p_ref, ap_ref, a_ref, u_ref, vg_ref, poolw_ref, pscale_ref,
                       convw_ref, convb_ref, lng_ref, lnb_ref, convpw_ref, sw_ref, sb_ref,
                       pool_o, conv_o, gmlp_o, p_scr, a_scr):
    c = pl.program_id(1)
    C = CHUNK
    has_prev = c > 0
    p_scr[0:HALO, :] = jnp.where(has_prev, pp_ref[0], 0.0)
    p_scr[HALO:HALO + MIX_ROWS, :] = p_ref[0]
    a_scr[0:HALO, :] = jnp.where(has_prev, ap_ref[0], 0.0)
    a_scr[HALO:HALO + MIX_ROWS, :] = a_ref[0]
    tri = lax.broadcasted_iota(jnp.int32, (C, C), 0) >= lax.broadcasted_iota(jnp.int32, (C, C), 1)
    w_causal = [jnp.where(tri, sw_ref[g], 0.0).astype(BF16) for g in range(N_HEADS)]
    grp = _lane_group((C, GROUP), HEAD_DIM)

    for sub in range(MIX_ROWS // C):
        s0 = HALO + sub * C
        rows = slice(sub * C, (sub + 1) * C)

        w2 = p_scr[s0 - 24:s0 + C, :] + p_scr[s0 - 25:s0 + C - 1, :]
        w4 = w2[8:C + 24] + w2[6:C + 22]
        w8 = w4[8:C + 16] + w4[4:C + 12]
        w16 = w8[8:C + 8] + w8[0:C]
        sums = {2: w2[24:C + 24], 4: w4[16:C + 16], 8: w8[8:C + 8], 16: w16}
        sel, win = _pool_window_select(sums, (C, GROUP))
        t_glob = c * MIX_ROWS + sub * C + lax.broadcasted_iota(jnp.int32, (C, GROUP), 0)
        count = jnp.minimum(t_glob + 1, win).astype(F32)
        m = sel / count - p_scr[s0:s0 + C, :]
        pool_o[0, rows, :] = _dot(m.astype(BF16), poolw_ref[...]) * pscale_ref[...]

        base = s0 - CONV_HIST
        y = jnp.zeros((C, GROUP), F32)
        for r in range(8):
            taps = [j for j in range(CONV_WIDTH) if (base + j) % 8 == r]
            n_rows = C + (8 if r else 0)
            z = jnp.zeros((n_rows, GROUP), F32)
            for j in taps:
                start = base + j - r
                z = z + a_scr[start:start + n_rows, :] * convw_ref[j:j + 1, :]
            y = y + z[r:r + C]
        y = _layernorm_silu(y + convb_ref[...], lng_ref[...], lnb_ref[...])
        conv_o[0, rows, :] = _dot(y.astype(BF16), convpw_ref[...])

        vgb = vg_ref[0, rows, :].astype(BF16)
        s = sb_ref[...]
        for g in range(N_HEADS):
            s = s + jnp.where(grp == g, _dot(w_causal[g], vgb), 0.0)
        gmlp_o[0, rows, :] = u_ref[0, rows, :] * s


def _mix_prompt(p, a, u, vg, poolw, pscale, convw, convb, lng, lnb, convpw, sw, sb):
    b, t, g = p.shape
    assert POOL_WINDOWS == (2, 4, 8, 16) and t % MIX_ROWS == 0 and MIX_ROWS % CHUNK == 0 and HALO >= CONV_HIST
    per = MIX_ROWS // HALO
    cur = pl.BlockSpec((1, MIX_ROWS, g), lambda bi, c: (bi, c, 0))
    prev = pl.BlockSpec((1, HALO, g), lambda bi, c: (bi, jnp.maximum(c * per - 1, 0), 0))

    def full(arr):
        nd = arr.ndim
        return pl.BlockSpec(arr.shape, lambda bi, c: (0,) * nd)

    out = jax.ShapeDtypeStruct((b, t, g), F32)
    weights = (poolw, pscale, convw, convb, lng, lnb, convpw, sw, sb)
    scratch = pltpu.VMEM((HALO + MIX_ROWS, g), F32)
    return pl.pallas_call(
        _mix_prompt_kernel,
        grid=(b, t // MIX_ROWS),
        in_specs=[prev, cur, prev, cur, cur, cur] + [full(w) for w in weights],
        out_specs=[cur] * 3,
        out_shape=[out] * 3,
        scratch_shapes=[scratch, scratch],
        compiler_params=_cparams(("parallel", "arbitrary")),
        name="mix_prompt",
    )(p, p, a, a, u, vg, *weights)


def _mix_sample_kernel(hp_ref, p_ref, hc_ref, a_ref, u_ref, vg_ref, poolw_ref, pscale_ref,
                       convw_ref, convb_ref, lng_ref, lnb_ref, convpw_ref, sw_ref, sb_ref,
                       pool_o, conv_o, gmlp_o, *, t_new):
    T = t_new
    nb = p_ref.shape[1]
    shape = (nb, GROUP)

    def pool_row(k):
        return hp_ref[k] if k < POOL_HIST else p_ref[k - POOL_HIST]

    def conv_row(k):
        return hc_ref[k] if k < CONV_HIST else a_ref[k - CONV_HIST]

    for t in range(T):
        acc = jnp.zeros(shape, F32)
        sums = {}
        for i in range(POOL_WINDOWS[-1]):
            k = POOL_HIST + t - i
            if k >= 0:
                acc = acc + pool_row(k)
            if i + 1 in POOL_WINDOWS:
                sums[i + 1] = acc
        sel, win = _pool_window_select(sums, shape)
        count = jnp.minimum(POOL_HIST + t + 1, win).astype(F32)
        m = sel / count - p_ref[t]
        pool_o[t] = _dot(m.astype(BF16), poolw_ref[...]) * pscale_ref[...]

        y = jnp.zeros(shape, F32)
        for j in range(CONV_WIDTH):
            y = y + conv_row(t + j) * convw_ref[j:j + 1, :]
        y = _layernorm_silu(y + convb_ref[...], lng_ref[...], lnb_ref[...])
        conv_o[t] = _dot(y.astype(BF16), convpw_ref[...])

        s = jnp.broadcast_to(sb_ref[t:t + 1, :], shape)
        for j in range(t + 1):
            s = s + sw_ref[t, j:j + 1, :] * vg_ref[j]
        gmlp_o[t] = u_ref[t] * s


def _mix_sample(hp, p, hc, a, u, vg, poolw, pscale, convw, convb, lng, lnb, convpw, sw, sb):
    t, db, g = p.shape

    def full(arr):
        nd = arr.ndim
        return pl.BlockSpec(arr.shape, lambda i: (0,) * nd)

    args = (hp, p, hc, a, u, vg, poolw, pscale, convw, convb, lng, lnb, convpw, sw, sb)
    out = jax.ShapeDtypeStruct((t, db, g), F32)
    return pl.pallas_call(
        functools.partial(_mix_sample_kernel, t_new=t),
        grid=(1,),
        in_specs=[full(x) for x in args],
        out_specs=[full(out)] * 3,
        out_shape=[out] * 3,
        compiler_params=_cparams(("arbitrary",)),
        name="mix_sample",
    )(*args)


def _row_tile(n, cap):
    t = min(n, cap)
    while n % t:
        t //= 2
    return t


def _ffn_tile(f):
    for tf in (512, 256, 128):
        if f % tf == 0:
            return tf
    return f


def _layer_weights(l, ffn1_norm, ffn1_w_gate, ffn1_w_up, ffn1_w_down, mix_norm, w_in, q_norm, k_norm,
                   pool_w, pool_scale, conv_w, conv_b, conv_ln_g, conv_ln_b, conv_pw,
                   gate_norm, spatial_w, spatial_b, w_out, ffn2_norm, ffn2_w_gate, ffn2_w_up, ffn2_w_down):
    row = lambda v: v[l].reshape(1, -1)
    head_id = jnp.arange(GROUP) // HEAD_DIM
    seg = jnp.where(head_id[:, None] == head_id[None, :], 1.0 / HEAD_DIM, 0.0).astype(BF16)
    n_win = len(POOL_WINDOWS)
    pg = GROUP // n_win
    pool_bd = jnp.zeros((GROUP, GROUP), F32)
    for gi in range(n_win):
        pool_bd = pool_bd.at[gi * pg:(gi + 1) * pg, gi * pg:(gi + 1) * pg].set(pool_w[l, gi])
    sb_lanes = jnp.repeat(spatial_b[l].T, HEAD_DIM, axis=1)
    return dict(
        layer=l,
        ffn1=(row(ffn1_norm), ffn1_w_gate, ffn1_w_up, ffn1_w_down),
        ffn2=(row(ffn2_norm), ffn2_w_gate, ffn2_w_up, ffn2_w_down),
        proj_in=(row(mix_norm), w_in[l].astype(BF16), jnp.tile(q_norm[l], N_HEADS).reshape(1, -1),
                 jnp.tile(k_norm[l], N_HEADS).reshape(1, -1), row(gate_norm), seg),
        w_out=w_out[l].astype(BF16),
        mix=(pool_bd.astype(BF16), row(pool_scale), conv_w[l], row(conv_b), row(conv_ln_g), row(conv_ln_b),
             conv_pw[l].astype(BF16)),
        spatial_w=spatial_w[l],
        spatial_b_lanes=sb_lanes,
    )


def _token_stage_in(x, w, kv_stack=None):
    n = x.shape[0]
    x1 = _ffn(x, (), None, *w["ffn1"], layer=w["layer"], tm=_row_tile(n, FFN_ROWS),
              tf=_ffn_tile(w["ffn1"][1].shape[2]))
    tm = _row_tile(n if kv_stack is None else kv_stack[3], PROJ_ROWS)
    return x1, _proj_in(x1, *w["proj_in"], tm=tm, kv_stack=kv_stack)


def _token_stage_out(x1, att, pool, conv, gmlp, w):
    n = x1.shape[0]
    return _ffn(x1, (att, pool, conv, gmlp), w["w_out"], *w["ffn2"], layer=w["layer"], tm=_row_tile(n, FFN_ROWS),
                tf=_ffn_tile(w["ffn2"][1].shape[2]))


def _prompt_layer(x, w, depth, kv_prev):
    b, t, d = x.shape
    x1, (q, kt, vt, p, a, u, vg) = _token_stage_in(x.reshape(b * t, d), w, (w["layer"], depth, b, t, kv_prev))
    r3 = lambda arr: arr.reshape(b, t, GROUP)
    q, p, a, u, vg = map(r3, (q, p, a, u, vg))
    att = _attn_prompt(q, kt, vt, w["layer"])
    pool, conv, gmlp = _mix_prompt(p, a, u, vg, *w["mix"], w["spatial_w"], w["spatial_b_lanes"])
    flat = lambda arr: arr.reshape(b * t, GROUP)
    y = _token_stage_out(x1, flat(att), flat(pool), flat(conv), flat(gmlp), w)
    return y.reshape(b, t, d), (kt, vt), p[:, -POOL_HIST:], a[:, -CONV_HIST:]


def _sample_layer(x, cache_k, cache_v, page_table, page_base, hist_pool, hist_conv, w):
    db, t, d = x.shape
    x1, (q, k, v, p, a, u, vg) = _token_stage_in(x.reshape(db * t, d), w)
    r3 = lambda arr: arr.reshape(db, t, GROUP)
    q, k, v, p, a, u, vg = map(r3, (q, k, v, p, a, u, vg))
    att = _attn_sample(q, k, v, cache_k, cache_v, page_table, page_base)
    tm = lambda arr: jnp.swapaxes(arr, 0, 1)
    sw_lanes = jnp.repeat(jnp.transpose(w["spatial_w"][:, :t, :t], (1, 2, 0)), HEAD_DIM, axis=2)
    hp_t, p_t, hc_t, a_t = tm(hist_pool), tm(p), tm(hist_conv), tm(a)
    pool, conv, gmlp = _mix_sample(hp_t, p_t, hc_t, a_t, tm(u), tm(vg),
                                   *w["mix"], sw_lanes, w["spatial_b_lanes"][:t])
    flat = lambda arr: tm(arr).reshape(db * t, GROUP)
    y = _token_stage_out(x1, att.reshape(db * t, GROUP), flat(pool), flat(conv), flat(gmlp), w)
    pool_state = tm(jnp.concatenate([hp_t, p_t], axis=0)[-POOL_HIST:])
    conv_state = tm(jnp.concatenate([hc_t, a_t], axis=0)[-CONV_HIST:])
    return y.reshape(db, t, d), k, v, pool_state, conv_state, vg


def kernel(x_prompt, x_sample, cache_k, cache_v, state_pool, state_conv, page_table, ffn1_norm, ffn1_w_gate, ffn1_w_up, ffn1_w_down, mix_norm, w_in, q_norm, k_norm, pool_w, pool_scale, conv_w, conv_b, conv_ln_g, conv_ln_b, conv_pw, gate_norm, spatial_w, spatial_b, w_out, ffn2_norm, ffn2_w_gate, ffn2_w_up, ffn2_w_down):
    depth, n_pool = cache_k.shape[0], cache_k.shape[1]
    b, t = x_prompt.shape[0], x_prompt.shape[1]
    db, ts = x_sample.shape[0], x_sample.shape[1]
    ck = jnp.transpose(cache_k, (0, 1, 3, 4, 2)).reshape(depth * n_pool, GROUP, PAGE)
    cv = jnp.transpose(cache_v, (0, 1, 3, 4, 2)).reshape(depth * n_pool, GROUP, PAGE)
    hp, hs = x_prompt, x_sample
    outs = [[] for _ in range(7)]
    kv_prompt = None
    for l in range(depth):
        w = _layer_weights(l, ffn1_norm, ffn1_w_gate, ffn1_w_up, ffn1_w_down, mix_norm, w_in, q_norm, k_norm,
                           pool_w, pool_scale, conv_w, conv_b, conv_ln_g, conv_ln_b, conv_pw,
                           gate_norm, spatial_w, spatial_b, w_out, ffn2_norm, ffn2_w_gate, ffn2_w_up, ffn2_w_down)
        hp, kv_prompt, pp, cp = _prompt_layer(hp, w, depth, kv_prompt)
        hs, ks, vs, ps, cs, gs = _sample_layer(hs, ck, cv, page_table, l * n_pool, state_pool[l], state_conv[l], w)
        heads = lambda arr: arr.reshape(db, ts, N_HEADS, HEAD_DIM)
        for lst, val in zip(outs, (heads(ks), heads(vs), pp, ps, cp, cs, gs)):
            lst.append(val)
    k_prompt, v_prompt = (jnp.transpose(s.reshape(depth, b, N_HEADS, HEAD_DIM, t), (0, 1, 4, 2, 3)) for s in kv_prompt)
    return (hp, hs, k_prompt, v_prompt) + tuple(jnp.stack(lst) for lst in outs)
```

```python
import functools

import jax
import jax.numpy as jnp
from jax import lax
from jax.experimental import pallas as pl
from jax.experimental.pallas import tpu as pltpu

F32 = jnp.float32
BF16 = jnp.bfloat16
NEG_INF = float("-inf")

GROUP = 256
HEAD_DIM = 64
N_HEADS = GROUP // HEAD_DIM
MOBA_BLOCK = 256
MOBA_TOP_K = 3
PAGE = 128
POOL_WINDOWS = (2, 4, 8, 16)
POOL_HIST = 15
CONV_WIDTH = 31
CONV_HIST = 30
CHUNK = 128
FFN_ROWS = 512
FFN1_ROWS = 2 * FFN_ROWS
PROJ_ROWS = 1024
SAMPLE_SEQS = 4
RMS_EPS = 1e-6
LN_EPS = 1e-5
VMEM_LIMIT = 56 * 1024 * 1024


def _cparams(sem):
    return pltpu.CompilerParams(dimension_semantics=sem, vmem_limit_bytes=VMEM_LIMIT)


def _rms(x, g):
    ms = jnp.mean(x * x, axis=-1, keepdims=True)
    return x * lax.rsqrt(ms + RMS_EPS) * g


def _dot(a, b):
    return jnp.dot(a, b, preferred_element_type=F32)


def _dot_nt(a, b):
    return lax.dot_general(a, b, (((1,), (1,)), ((), ())), preferred_element_type=F32)


def _split(a):
    hi = a.astype(BF16)
    lo = (a - hi.astype(F32)).astype(BF16)
    return hi, lo


def _dot3_nt(a, b):
    ah, al = _split(a)
    bh, bl = _split(b)
    return _dot_nt(ah, bh) + _dot_nt(ah, bl) + _dot_nt(al, bh)


def _lane_group(shape, width):
    return lax.broadcasted_iota(jnp.int32, shape, len(shape) - 1) // width


def _ffn_kernel(*refs, n_f, tf, n_mix, layer):
    x_ref = refs[0]
    mix_refs = refs[1:1 + n_mix]
    rest = refs[1 + n_mix:]
    if n_mix:
        wo_ref, rest = rest[0], rest[1:]
    g_ref, wg_hbm, wu_hbm, wd_hbm, o_ref, wg_b, wu_b, wd_b, gu_stage, d_stage, sems, acc_scr = rest
    first = pl.program_id(0) == 0

    def chunk_copies(c):
        slot = c % 2
        cols = pl.ds(c * tf, tf)
        return (pltpu.make_async_copy(wg_hbm.at[layer, :, cols], gu_stage.at[slot, 0], sems.at[slot, 0]),
                pltpu.make_async_copy(wu_hbm.at[layer, :, cols], gu_stage.at[slot, 1], sems.at[slot, 1]),
                pltpu.make_async_copy(wd_hbm.at[layer, cols, :], d_stage.at[slot], sems.at[slot, 2]))

    @pl.when(first)
    def _():
        for cp in chunk_copies(0):
            cp.start()

    x = x_ref[...]
    for idx, m_ref in enumerate(mix_refs):
        x = x + _dot(m_ref[...].astype(BF16), wo_ref[idx * GROUP:(idx + 1) * GROUP, :])
    h = _rms(x, g_ref[...]).astype(BF16)

    def land_chunk(c):
        cols = slice(c * tf, (c + 1) * tf)
        if c + 1 < n_f:
            for cp in chunk_copies(c + 1):
                cp.start()
        for cp in chunk_copies(c):
            cp.wait()
        wg_b[:, cols] = gu_stage[c % 2, 0].astype(BF16)
        wu_b[:, cols] = gu_stage[c % 2, 1].astype(BF16)
        wd_b[cols, :] = d_stage[c % 2].astype(BF16)

    def swiglu(fetch):
        for c in range(n_f):
            cols = slice(c * tf, (c + 1) * tf)
            fetch(c)
            gate = _dot(h, wg_b[:, cols])
            up = _dot(h, wu_b[:, cols])
            act = (gate * jax.nn.sigmoid(gate) * up).astype(BF16)
            part = _dot(act, wd_b[cols, :])
            if c == 0:
                acc_scr[...] = part
            else:
                acc_scr[...] += part
        o_ref[...] = x + 0.5 * acc_scr[...]

    @pl.when(first)
    def _():
        swiglu(land_chunk)

    @pl.when(jnp.logical_not(first))
    def _():
        swiglu(lambda c: None)


def _ffn(x, mixes, w_out, norm, wg, wu, wd, layer, tm, tf):
    n, d = x.shape
    f = wg.shape[2]
    n_mix = len(mixes)
    row = lambda i: (i, 0)
    hbm = pl.BlockSpec(memory_space=pl.ANY)
    in_specs = [pl.BlockSpec((tm, d), row)] + [pl.BlockSpec((tm, GROUP), row)] * n_mix
    args = [x, *mixes]
    if n_mix:
        in_specs.append(pl.BlockSpec(w_out.shape, lambda i: (0, 0)))
        args.append(w_out)
    in_specs += [pl.BlockSpec((1, d), lambda i: (0, 0)), hbm, hbm, hbm]
    args += [norm, wg, wu, wd]
    return pl.pallas_call(
        functools.partial(_ffn_kernel, n_f=f // tf, tf=tf, n_mix=n_mix, layer=layer),
        grid=(n // tm,),
        in_specs=in_specs,
        out_specs=pl.BlockSpec((tm, d), row),
        out_shape=jax.ShapeDtypeStruct((n, d), F32),
        scratch_shapes=[pltpu.VMEM((d, f), BF16), pltpu.VMEM((d, f), BF16), pltpu.VMEM((f, d), BF16),
                        pltpu.VMEM((2, 2, d, tf), F32), pltpu.VMEM((2, tf, d), F32),
                        pltpu.SemaphoreType.DMA((2, 3)), pltpu.VMEM((tm, d), F32)],
        compiler_params=_cparams(("arbitrary",)),
        name="ffn_out" if n_mix else "ffn",
    )(*args)


def _proj_in_kernel(*refs, kv_transposed, n_alias):
    x_ref, g_ref, w_ref, qn_ref, kn_ref, gn_ref, seg_ref = refs[:7]
    q_o, k_o, v_o, p_o, a_o, u_o, vg_o = refs[7 + n_alias:]
    G = GROUP
    h = _rms(x_ref[...], g_ref[...]).astype(BF16)
    z = _dot(h, w_ref[...])
    seg = seg_ref[...]

    def head_rms(t, gain):
        sq = t * t
        hi, lo = _split(sq)
        ms = _dot(hi, seg) + _dot(lo, seg)
        return t * lax.rsqrt(ms + RMS_EPS) * gain

    q_o[...] = head_rms(z[:, 0:G], qn_ref[...])
    k = head_rms(z[:, G:2 * G], kn_ref[...])
    v = z[:, 2 * G:3 * G]
    if not kv_transposed:
        k_o[...] = k
        v_o[...] = v
    elif len(k_o.shape) == 2:
        k_o[...] = k.T
        v_o[...] = v.T
    else:
        k_o[0] = k.T
        v_o[0] = v.T
        for slot in range(1, k_o.shape[0]):
            k_o[slot] = jnp.zeros(k_o.shape[1:], F32)
            v_o[slot] = jnp.zeros(v_o.shape[1:], F32)
    p_o[...] = z[:, 3 * G:4 * G]
    a_o[...] = z[:, 4 * G:5 * G] * jax.nn.sigmoid(z[:, 5 * G:6 * G])
    d = jax.nn.gelu(z[:, 6 * G:8 * G])
    u_o[...] = d[:, 0:G]
    vg_o[...] = _rms(d[:, G:2 * G], gn_ref[...])


def _proj_in(x, norm, w_in, qn, kn, gn, seg, tm, kv_stack=None):
    n, d = x.shape
    cols = w_in.shape[1]
    row = lambda i: (i, 0)
    fixed = lambda i: (0, 0)
    out = jax.ShapeDtypeStruct((n, GROUP), F32)
    in_specs = [pl.BlockSpec((tm, d), row),
                pl.BlockSpec((1, d), fixed),
                pl.BlockSpec((d, cols), fixed),
                pl.BlockSpec((1, GROUP), fixed),
                pl.BlockSpec((1, GROUP), fixed),
                pl.BlockSpec((1, GROUP), fixed),
                pl.BlockSpec((GROUP, GROUP), fixed)]
    args = [x, norm, w_in, qn, kn, gn, seg]
    out_specs = [pl.BlockSpec((tm, GROUP), row)] * 7
    out_shape = [out] * 7
    aliases = {}
    if kv_stack is not None:
        layer, depth, b, t, prev = kv_stack
        per_b = t // tm
        if prev is None:
            assert layer == 0
            kv_spec = pl.BlockSpec((depth, None, GROUP, tm), lambda i: (0, i // per_b, 0, i % per_b))
        else:
            kv_spec = pl.BlockSpec((None, None, GROUP, tm), lambda i: (layer, i // per_b, 0, i % per_b))
        out_specs[1:3] = [kv_spec, kv_spec]
        out_shape[1:3] = [jax.ShapeDtypeStruct((depth, b, GROUP, t), F32)] * 2
        if prev is not None:
            in_specs += [pl.BlockSpec(memory_space=pl.ANY)] * 2
            aliases = {len(args): 1, len(args) + 1: 2}
            args += list(prev)
    return pl.pallas_call(
        functools.partial(_proj_in_kernel, kv_transposed=kv_stack is not None, n_alias=len(aliases)),
        grid=(n // tm,),
        in_specs=in_specs,
        out_specs=out_specs,
        out_shape=out_shape,
        input_output_aliases=aliases,
        compiler_params=_cparams(("parallel",)),
        name="proj_in",
    )(*args)


GATE_ROWS = 8
VT_ROWS = HEAD_DIM + 16


def _select_bias_rows(gate, n_valid):
    rows = lax.broadcasted_iota(jnp.int32, gate.shape, 0)
    rows_f = rows.astype(F32)
    cur = jnp.where(rows < n_valid, gate, NEG_INF)
    bias = jnp.where(rows == n_valid, 0.0, NEG_INF)
    for _ in range(MOBA_TOP_K):
        top = jnp.max(cur, axis=0, keepdims=True)
        first = jnp.min(jnp.where(cur == top, rows_f, float(gate.shape[0])), axis=0, keepdims=True)
        pick = (rows_f == first) & (top > NEG_INF)
        bias = jnp.where(pick, 0.0, bias)
        cur = jnp.where(pick, NEG_INF, cur)
    return bias


def _attn_prompt_kernel(q_ref, k_ref, v_ref, o_ref, kb_scr, vt_scr, kbd_scr, bias_scr, qk_scr, alibi_scr,
                        *, n_blk):
    B = MOBA_BLOCK
    lane_head = _lane_group((1, GROUP), HEAD_DIM)
    slopes = [2.0 ** (-8.0 * (h + 1) / N_HEADS) for h in range(N_HEADS)]

    neg_rel = (lax.broadcasted_iota(jnp.int32, (B, B), 0)
               - lax.broadcasted_iota(jnp.int32, (B, B), 1)).astype(F32)
    for h in range(N_HEADS):
        alibi_scr[h] = slopes[h] * neg_rel
        alibi_scr[N_HEADS + h] = jnp.where(neg_rel <= 0.0, slopes[h] * neg_rel, NEG_INF)
    kbd_scr[...] = jnp.zeros_like(kbd_scr)
    ones = jnp.ones((VT_ROWS - HEAD_DIM, B), BF16)
    for n in range(n_blk):
        kn = k_ref[:, n * B:(n + 1) * B].T
        kb_scr[n] = kn.astype(BF16)
        mean = jnp.mean(kn, axis=0, keepdims=True)
        vt = v_ref[:, n * B:(n + 1) * B]
        for h in range(N_HEADS):
            r = h * GATE_ROWS + n
            kbd_scr[r:r + 1, :] = jnp.where(lane_head == h, mean, 0.0)
            vt_scr[n, h, 0:HEAD_DIM, :] = vt[h * HEAD_DIM:(h + 1) * HEAD_DIM, :].astype(BF16)
            vt_scr[n, h, HEAD_DIM:VT_ROWS, :] = ones

    def query_block(i, _):
        rows = pl.ds(pl.multiple_of(i * B, B), B)
        o_ref[0, rows, :] = _attend_block(i, q_ref[0, rows, :], slopes, kb_scr, vt_scr, kbd_scr, bias_scr,
                                          qk_scr, alibi_scr)
        return 0

    lax.fori_loop(0, n_blk, query_block, 0)


def _attend_block(i, q, slopes, kb_scr, vt_scr, kbd_scr, bias_scr, qk_scr, alibi_scr):
    B = MOBA_BLOCK
    gate = _dot3_nt(kbd_scr[...], q)
    for h in range(N_HEADS):
        rows = slice(h * GATE_ROWS, (h + 1) * GATE_ROWS)
        bias_scr[rows, :] = _select_bias_rows(gate[rows, :], i)

    q_scaled = q * (HEAD_DIM ** -0.5)
    in_head = _lane_group((B, GROUP), HEAD_DIM)
    qh = [jnp.where(in_head == h, q_scaled, 0.0).astype(BF16) for h in range(N_HEADS)]

    def scores(blk):
        kn = kb_scr[blk]
        return tuple(_dot_nt(kn, qh[h]) for h in range(N_HEADS))

    def absorb(blk, ms, accs):
        causal = jnp.where(blk == i, N_HEADS, 0)
        dist = ((i - blk) * B).astype(F32)
        ss, new_ms, ps = [], [], []
        for h in range(N_HEADS):
            bias = bias_scr[pl.ds(h * GATE_ROWS + blk, 1), :] - slopes[h] * dist
            ss.append(qk_scr[h] + alibi_scr[h + causal] + bias)
            new_ms.append(jnp.maximum(ms[h], jnp.max(ss[h], axis=0, keepdims=True)))
        for h in range(N_HEADS):
            ps.append(jnp.exp(ss[h] - new_ms[h]).astype(BF16))
        new_accs = [jnp.exp(ms[h] - new_ms[h]) * accs[h] + _dot(vt_scr[blk, h], ps[h])
                    for h in range(N_HEADS)]
        return tuple(new_ms), tuple(new_accs)

    def stash(qk):
        for h in range(N_HEADS):
            qk_scr[h] = qk[h]

    def body(j, carry):
        qk_next = scores(j)
        carry = absorb(jnp.where(j == 0, i, j - 1), *carry)
        stash(qk_next)
        return carry

    stash(scores(i))
    init = ((jnp.full((1, B), NEG_INF, F32),) * N_HEADS, (jnp.zeros((VT_ROWS, B), F32),) * N_HEADS)
    ms, accs = lax.fori_loop(0, i, body, init)
    ms, accs = absorb(jnp.where(i == 0, i, i - 1), ms, accs)
    out_t = jnp.concatenate([a[0:HEAD_DIM, :] / a[HEAD_DIM:HEAD_DIM + 1, :] for a in accs], axis=0)
    return out_t.T


def _attn_prompt(q, kt, vt, layer):
    b, t, g = q.shape
    n_blk = t // MOBA_BLOCK
    assert t % MOBA_BLOCK == 0 and n_blk <= GATE_ROWS
    kv_spec = pl.BlockSpec((None, None, g, t), lambda bi: (layer, bi, 0, 0))
    seq_spec = pl.BlockSpec((1, t, g), lambda bi: (bi, 0, 0))
    return pl.pallas_call(
        functools.partial(_attn_prompt_kernel, n_blk=n_blk),
        grid=(b,),
        in_specs=[seq_spec, kv_spec, kv_spec],
        out_specs=seq_spec,
        out_shape=jax.ShapeDtypeStruct((b, t, g), F32),
        scratch_shapes=[pltpu.VMEM((n_blk, MOBA_BLOCK, g), BF16),
                        pltpu.VMEM((n_blk, N_HEADS, VT_ROWS, MOBA_BLOCK), BF16),
                        pltpu.VMEM((N_HEADS * GATE_ROWS, g), F32),
                        pltpu.VMEM((N_HEADS * GATE_ROWS, MOBA_BLOCK), F32),
                        pltpu.VMEM((N_HEADS, MOBA_BLOCK, MOBA_BLOCK), F32),
                        pltpu.VMEM((2 * N_HEADS, MOBA_BLOCK, MOBA_BLOCK), F32)],
        compiler_params=_cparams(("parallel",)),
        name="attn_prompt",
    )(q, kt, vt)


def _attn_sample_kernel(pt_ref, q_ref, kn_ref, vn_ref, ck_hbm, cv_hbm, o_ref, k_buf, v_buf, sems,
                        *, n_pages, t_new, page_base, n_seq):
    b = pl.program_id(0)

    def page_copies(step, slot):
        out = []
        for s in range(n_seq):
            for pg in range(n_pages):
                page = page_base + pt_ref[step * n_seq + s, pg]
                dst = s * n_pages + pg
                out.append(pltpu.make_async_copy(ck_hbm.at[page], k_buf.at[slot, dst], sems.at[0, slot]))
                out.append(pltpu.make_async_copy(cv_hbm.at[page], v_buf.at[slot, dst], sems.at[1, slot]))
        return out

    @pl.when(b == 0)
    def _():
        for cp in page_copies(0, 0):
            cp.start()

    @pl.when(b + 1 < pl.num_programs(0))
    def _():
        for cp in page_copies(b + 1, (b + 1) % 2):
            cp.start()

    slot = b % 2
    for cp in page_copies(b, slot):
        cp.wait()
    seqs = []
    for s in range(n_seq):
        k_pages = [k_buf.at[slot, s * n_pages + pg] for pg in range(n_pages)]
        v_pages = [v_buf.at[slot, s * n_pages + pg] for pg in range(n_pages)]
        seqs.append((q_ref[s], kn_ref[s], vn_ref[s], k_pages, v_pages))
    scored = [_sample_scores(q, k_new, k_pages, t_new) for q, k_new, _, k_pages, _ in seqs]
    probs = [_sample_softmax(*sc, t_new) for sc in scored]
    for s, ((_, _, v_new, _, v_pages), (ps, l)) in enumerate(zip(seqs, probs)):
        o_ref[s] = _sample_values(ps, l, v_new, v_pages, t_new)


def _sample_scores(q_in, k_in, k_pages, t_new):
    T = t_new
    R = N_HEADS * T
    q = q_in * HEAD_DIM ** -0.5
    q_rows = jnp.concatenate([q] * N_HEADS, axis=0)
    row_head = lax.broadcasted_iota(jnp.int32, (R, GROUP), 0) // T
    qbd = jnp.where(_lane_group((R, GROUP), HEAD_DIM) == row_head, q_rows, 0.0)
    q_hi, q_lo = _split(qbd)
    raw = []
    for kp in k_pages:
        k_hi, k_lo = _split(kp[...])
        raw.append(_dot(q_hi, k_hi) + _dot(q_hi, k_lo) + _dot(q_lo, k_hi))
    pad = jnp.zeros((PAGE - T, GROUP), F32)
    k_new = jnp.concatenate([k_in, pad], axis=0).astype(BF16)
    return raw, _dot_nt(q_hi, k_new)


def _sample_softmax(raw, raw_new, t_new):
    T = t_new
    R = N_HEADS * T
    n_pages = len(raw)
    per_blk = MOBA_BLOCK // PAGE
    n_blk = n_pages // per_blk
    past = n_pages * PAGE
    gates = [jnp.sum(sum(raw[n * per_blk:(n + 1) * per_blk]), axis=-1, keepdims=True) for n in range(n_blk)]
    biases = []
    for n in range(n_blk):
        rank = jnp.zeros((R, 1), F32)
        for mth in range(n_blk):
            if mth != n:
                beats = (gates[mth] >= gates[n]) if mth < n else (gates[mth] > gates[n])
                rank = rank + jnp.where(beats, 1.0, 0.0)
        biases.append(jnp.where(rank < MOBA_TOP_K, 0.0, NEG_INF))

    r1 = lax.broadcasted_iota(jnp.int32, (R, 1), 0)
    slope = jnp.zeros((R, 1), F32)
    for h in range(N_HEADS):
        slope = jnp.where(r1 // T == h, 2.0 ** (-8.0 * (h + 1) / N_HEADS), slope)
    t_row = r1 % T
    pos_q = (past + t_row).astype(F32)
    col = lax.broadcasted_iota(jnp.int32, (R, PAGE), 1)
    scores = []
    for pg in range(n_pages):
        key_pos = (col + pg * PAGE).astype(F32)
        scores.append(raw[pg] - slope * (pos_q - key_pos) + biases[pg // per_blk])
    s_new = raw_new - slope * (t_row - col).astype(F32)
    scores.append(jnp.where(col <= t_row, s_new, NEG_INF))

    m = functools.reduce(jnp.maximum, [jnp.max(s, axis=-1, keepdims=True) for s in scores])
    ps = [jnp.exp(s - m) for s in scores]
    l = functools.reduce(lambda a, b: a + b, [jnp.sum(p, axis=-1, keepdims=True) for p in ps])
    return ps, l


def _sample_values(ps, l, v_in, v_pages, t_new):
    T = t_new
    pad = jnp.zeros((PAGE - T, GROUP), F32)
    v_new = jnp.concatenate([v_in, pad], axis=0).astype(BF16)
    acc = _dot(ps[-1].astype(BF16), v_new)
    for p, vp in zip(ps[:-1], v_pages):
        acc = acc + _dot_nt(p.astype(BF16), vp[...].astype(BF16))
    acc = acc / l
    lane_head = _lane_group((T, GROUP), HEAD_DIM)
    out = jnp.zeros((T, GROUP), F32)
    for h in range(N_HEADS):
        out = jnp.where(lane_head == h, acc[h * T:(h + 1) * T, :], out)
    return out


def _attn_sample(q, k_new, v_new, cache_k, cache_v, page_table, page_base):
    db, t, g = q.shape
    n_pages = page_table.shape[1]
    n_seq = SAMPLE_SEQS if db % SAMPLE_SEQS == 0 else 1
    tok = pl.BlockSpec((n_seq, t, g), lambda b, pt: (b, 0, 0))
    hbm = pl.BlockSpec(memory_space=pl.ANY)
    page_buf = pltpu.VMEM((2, n_seq * n_pages, g, PAGE), F32)
    return pl.pallas_call(
        functools.partial(_attn_sample_kernel, n_pages=n_pages, t_new=t, page_base=page_base, n_seq=n_seq),
        grid_spec=pltpu.PrefetchScalarGridSpec(
            num_scalar_prefetch=1,
            grid=(db // n_seq,),
            in_specs=[tok, tok, tok, hbm, hbm],
            out_specs=tok,
            scratch_shapes=[page_buf, page_buf, pltpu.SemaphoreType.DMA((2, 2))],
        ),
        out_shape=jax.ShapeDtypeStruct((db, t, g), F32),
        compiler_params=_cparams(("arbitrary",)),
        name="attn_sample",
    )(page_table, q, k_new, v_new, cache_k, cache_v)


def _layernorm_silu(y, g, b):
    mu = jnp.mean(y, axis=-1, keepdims=True)
    yc = y - mu
    var = jnp.mean(yc * yc, axis=-1, keepdims=True)
    yn = yc * lax.rsqrt(var + LN_EPS) * g + b
    return yn * jax.nn.sigmoid(yn)


def _pool_window_select(sums, shape):
    grp = _lane_group(shape, GROUP // len(POOL_WINDOWS))
    sel = sums[POOL_WINDOWS[-1]]
    for gi in range(len(POOL_WINDOWS) - 2, -1, -1):
        sel = jnp.where(grp == gi, sums[POOL_WINDOWS[gi]], sel)
    win = jnp.left_shift(jnp.int32(POOL_WINDOWS[0]), grp)
    return sel, win


MIX_ROWS = 512
HALO = 32


def _mix_prompt_kernel(pp_ref, p_ref, ap_ref, a_ref, u_ref, vg_ref, poolw_ref, pscale_ref,
                       convw_ref, convb_ref, lng_ref, lnb_ref, convpw_ref, sw_ref, sb_ref,
                       pool_o, conv_o, gmlp_o, p_scr, a_scr):
    c = pl.program_id(1)
    C = CHUNK
    has_prev = c > 0
    p_scr[0:HALO, :] = jnp.where(has_prev, pp_ref[0], 0.0)
    p_scr[HALO:HALO + MIX_ROWS, :] = p_ref[0]
    a_scr[0:HALO, :] = jnp.where(has_prev, ap_ref[0], 0.0)
    a_scr[HALO:HALO + MIX_ROWS, :] = a_ref[0]
    tri = lax.broadcasted_iota(jnp.int32, (C, C), 0) >= lax.broadcasted_iota(jnp.int32, (C, C), 1)
    w_causal = [jnp.where(tri, sw_ref[g], 0.0).astype(BF16) for g in range(N_HEADS)]
    grp = _lane_group((C, GROUP), HEAD_DIM)

    for sub in range(MIX_ROWS // C):
        s0 = HALO + sub * C
        rows = slice(sub * C, (sub + 1) * C)

        w2 = p_scr[s0 - 24:s0 + C, :] + p_scr[s0 - 25:s0 + C - 1, :]
        w4 = w2[8:C + 24] + w2[6:C + 22]
        w8 = w4[8:C + 16] + w4[4:C + 12]
        w16 = w8[8:C + 8] + w8[0:C]
        sums = {2: w2[24:C + 24], 4: w4[16:C + 16], 8: w8[8:C + 8], 16: w16}
        sel, win = _pool_window_select(sums, (C, GROUP))
        t_glob = c * MIX_ROWS + sub * C + lax.broadcasted_iota(jnp.int32, (C, GROUP), 0)
        count = jnp.minimum(t_glob + 1, win).astype(F32)
        m = sel / count - p_scr[s0:s0 + C, :]
        pool_o[0, rows, :] = _dot(m.astype(BF16), poolw_ref[...]) * pscale_ref[...]

        base = s0 - CONV_HIST
        y = jnp.zeros((C, GROUP), F32)
        for r in range(8):
            taps = [j for j in range(CONV_WIDTH) if (base + j) % 8 == r]
            n_rows = C + (8 if r else 0)
            z = jnp.zeros((n_rows, GROUP), F32)
            for j in taps:
                start = base + j - r
                z = z + a_scr[start:start + n_rows, :] * convw_ref[j:j + 1, :]
            y = y + z[r:r + C]
        y = _layernorm_silu(y + convb_ref[...], lng_ref[...], lnb_ref[...])
        conv_o[0, rows, :] = _dot(y.astype(BF16), convpw_ref[...])

        vgb = vg_ref[0, rows, :].astype(BF16)
        s = sb_ref[...]
        for g in range(N_HEADS):
            s = s + jnp.where(grp == g, _dot(w_causal[g], vgb), 0.0)
        gmlp_o[0, rows, :] = u_ref[0, rows, :] * s


def _mix_prompt(p, a, u, vg, poolw, pscale, convw, convb, lng, lnb, convpw, sw, sb):
    b, t, g = p.shape
    assert POOL_WINDOWS == (2, 4, 8, 16) and t % MIX_ROWS == 0 and MIX_ROWS % CHUNK == 0 and HALO >= CONV_HIST
    per = MIX_ROWS // HALO
    cur = pl.BlockSpec((1, MIX_ROWS, g), lambda bi, c: (bi, c, 0))
    prev = pl.BlockSpec((1, HALO, g), lambda bi, c: (bi, jnp.maximum(c * per - 1, 0), 0))

    def full(arr):
        nd = arr.ndim
        return pl.BlockSpec(arr.shape, lambda bi, c: (0,) * nd)

    out = jax.ShapeDtypeStruct((b, t, g), F32)
    weights = (poolw, pscale, convw, convb, lng, lnb, convpw, sw, sb)
    scratch = pltpu.VMEM((HALO + MIX_ROWS, g), F32)
    return pl.pallas_call(
        _mix_prompt_kernel,
        grid=(b, t // MIX_ROWS),
        in_specs=[prev, cur, prev, cur, cur, cur] + [full(w) for w in weights],
        out_specs=[cur] * 3,
        out_shape=[out] * 3,
        scratch_shapes=[scratch, scratch],
        compiler_params=_cparams(("parallel", "arbitrary")),
        name="mix_prompt",
    )(p, p, a, a, u, vg, *weights)


def _mix_sample_kernel(hp_ref, p_ref, hc_ref, a_ref, u_ref, vg_ref, poolw_ref, pscale_ref,
                       convw_ref, convb_ref, lng_ref, lnb_ref, convpw_ref, sw_ref, sb_ref,
                       pool_o, conv_o, gmlp_o, *, t_new):
    T = t_new
    nb = p_ref.shape[1]
    shape = (nb, GROUP)

    def pool_row(k):
        return hp_ref[k] if k < POOL_HIST else p_ref[k - POOL_HIST]

    def conv_row(k):
        return hc_ref[k] if k < CONV_HIST else a_ref[k - CONV_HIST]

    for t in range(T):
        acc = jnp.zeros(shape, F32)
        sums = {}
        for i in range(POOL_WINDOWS[-1]):
            k = POOL_HIST + t - i
            if k >= 0:
                acc = acc + pool_row(k)
            if i + 1 in POOL_WINDOWS:
                sums[i + 1] = acc
        sel, win = _pool_window_select(sums, shape)
        count = jnp.minimum(POOL_HIST + t + 1, win).astype(F32)
        m = sel / count - p_ref[t]
        pool_o[t] = _dot(m.astype(BF16), poolw_ref[...]) * pscale_ref[...]

        y = jnp.zeros(shape, F32)
        for j in range(CONV_WIDTH):
            y = y + conv_row(t + j) * convw_ref[j:j + 1, :]
        y = _layernorm_silu(y + convb_ref[...], lng_ref[...], lnb_ref[...])
        conv_o[t] = _dot(y.astype(BF16), convpw_ref[...])

        s = jnp.broadcast_to(sb_ref[t:t + 1, :], shape)
        for j in range(t + 1):
            s = s + sw_ref[t, j:j + 1, :] * vg_ref[j]
        gmlp_o[t] = u_ref[t] * s


def _mix_sample(hp, p, hc, a, u, vg, poolw, pscale, convw, convb, lng, lnb, convpw, sw, sb):
    t, db, g = p.shape

    def full(arr):
        nd = arr.ndim
        return pl.BlockSpec(arr.shape, lambda i: (0,) * nd)

    args = (hp, p, hc, a, u, vg, poolw, pscale, convw, convb, lng, lnb, convpw, sw, sb)
    out = jax.ShapeDtypeStruct((t, db, g), F32)
    return pl.pallas_call(
        functools.partial(_mix_sample_kernel, t_new=t),
        grid=(1,),
        in_specs=[full(x) for x in args],
        out_specs=[full(out)] * 3,
        out_shape=[out] * 3,
        compiler_params=_cparams(("arbitrary",)),
        name="mix_sample",
    )(*args)


def _row_tile(n, cap):
    t = min(n, cap)
    while n % t:
        t //= 2
    return t


def _ffn_tile(f):
    for tf in (512, 256, 128):
        if f % tf == 0:
            return tf
    return f


def _layer_weights(l, ffn1_norm, ffn1_w_gate, ffn1_w_up, ffn1_w_down, mix_norm, w_in, q_norm, k_norm,
                   pool_w, pool_scale, conv_w, conv_b, conv_ln_g, conv_ln_b, conv_pw,
                   gate_norm, spatial_w, spatial_b, w_out, ffn2_norm, ffn2_w_gate, ffn2_w_up, ffn2_w_down):
    row = lambda v: v[l].reshape(1, -1)
    head_id = jnp.arange(GROUP) // HEAD_DIM
    seg = jnp.where(head_id[:, None] == head_id[None, :], 1.0 / HEAD_DIM, 0.0).astype(BF16)
    n_win = len(POOL_WINDOWS)
    pg = GROUP // n_win
    pool_bd = jnp.zeros((GROUP, GROUP), F32)
    for gi in range(n_win):
        pool_bd = pool_bd.at[gi * pg:(gi + 1) * pg, gi * pg:(gi + 1) * pg].set(pool_w[l, gi])
    sb_lanes = jnp.repeat(spatial_b[l].T, HEAD_DIM, axis=1)
    return dict(
        layer=l,
        ffn1=(row(ffn1_norm), ffn1_w_gate, ffn1_w_up, ffn1_w_down),
        ffn2=(row(ffn2_norm), ffn2_w_gate, ffn2_w_up, ffn2_w_down),
        proj_in=(row(mix_norm), w_in[l].astype(BF16), jnp.tile(q_norm[l], N_HEADS).reshape(1, -1),
                 jnp.tile(k_norm[l], N_HEADS).reshape(1, -1), row(gate_norm), seg),
        w_out=w_out[l].astype(BF16),
        mix=(pool_bd.astype(BF16), row(pool_scale), conv_w[l], row(conv_b), row(conv_ln_g), row(conv_ln_b),
             conv_pw[l].astype(BF16)),
        spatial_w=spatial_w[l],
        spatial_b_lanes=sb_lanes,
    )


def _token_stage_in(x, w, kv_stack=None):
    n = x.shape[0]
    x1 = _ffn(x, (), None, *w["ffn1"], layer=w["layer"], tm=_row_tile(n, FFN1_ROWS),
              tf=_ffn_tile(w["ffn1"][1].shape[2]))
    tm = _row_tile(n if kv_stack is None else kv_stack[3], PROJ_ROWS)
    return x1, _proj_in(x1, *w["proj_in"], tm=tm, kv_stack=kv_stack)


def _token_stage_out(x1, att, pool, conv, gmlp, w):
    n = x1.shape[0]
    return _ffn(x1, (att, pool, conv, gmlp), w["w_out"], *w["ffn2"], layer=w["layer"], tm=_row_tile(n, FFN_ROWS),
                tf=_ffn_tile(w["ffn2"][1].shape[2]))


def _prompt_layer(x, w, depth, kv_prev):
    b, t, d = x.shape
    x1, (q, kt, vt, p, a, u, vg) = _token_stage_in(x.reshape(b * t, d), w, (w["layer"], depth, b, t, kv_prev))
    r3 = lambda arr: arr.reshape(b, t, GROUP)
    q, p, a, u, vg = map(r3, (q, p, a, u, vg))
    att = _attn_prompt(q, kt, vt, w["layer"])
    pool, conv, gmlp = _mix_prompt(p, a, u, vg, *w["mix"], w["spatial_w"], w["spatial_b_lanes"])
    flat = lambda arr: arr.reshape(b * t, GROUP)
    y = _token_stage_out(x1, flat(att), flat(pool), flat(conv), flat(gmlp), w)
    return y.reshape(b, t, d), (kt, vt), p[:, -POOL_HIST:], a[:, -CONV_HIST:]


def _sample_layer(x, cache_k, cache_v, page_table, page_base, hist_pool, hist_conv, w):
    db, t, d = x.shape
    x1, (q, k, v, p, a, u, vg) = _token_stage_in(x.reshape(db * t, d), w)
    r3 = lambda arr: arr.reshape(db, t, GROUP)
    q, k, v, p, a, u, vg = map(r3, (q, k, v, p, a, u, vg))
    att = _attn_sample(q, k, v, cache_k, cache_v, page_table, page_base)
    tm = lambda arr: jnp.swapaxes(arr, 0, 1)
    sw_lanes = jnp.repeat(jnp.transpose(w["spatial_w"][:, :t, :t], (1, 2, 0)), HEAD_DIM, axis=2)
    hp_t, p_t, hc_t, a_t = tm(hist_pool), tm(p), tm(hist_conv), tm(a)
    pool, conv, gmlp = _mix_sample(hp_t, p_t, hc_t, a_t, tm(u), tm(vg),
                                   *w["mix"], sw_lanes, w["spatial_b_lanes"][:t])
    flat = lambda arr: tm(arr).reshape(db * t, GROUP)
    y = _token_stage_out(x1, att.reshape(db * t, GROUP), flat(pool), flat(conv), flat(gmlp), w)
    pool_state = tm(jnp.concatenate([hp_t, p_t], axis=0)[-POOL_HIST:])
    conv_state = tm(jnp.concatenate([hc_t, a_t], axis=0)[-CONV_HIST:])
    return y.reshape(db, t, d), k, v, pool_state, conv_state, vg


def kernel(x_prompt, x_sample, cache_k, cache_v, state_pool, state_conv, page_table, ffn1_norm, ffn1_w_gate, ffn1_w_up, ffn1_w_down, mix_norm, w_in, q_norm, k_norm, pool_w, pool_scale, conv_w, conv_b, conv_ln_g, conv_ln_b, conv_pw, gate_norm, spatial_w, spatial_b, w_out, ffn2_norm, ffn2_w_gate, ffn2_w_up, ffn2_w_down):
    depth, n_pool = cache_k.shape[0], cache_k.shape[1]
    b, t = x_prompt.shape[0], x_prompt.shape[1]
    db, ts = x_sample.shape[0], x_sample.shape[1]
    ck = jnp.transpose(cache_k, (0, 1, 3, 4, 2)).reshape(depth * n_pool, GROUP, PAGE)
    cv = jnp.transpose(cache_v, (0, 1, 3, 4, 2)).reshape(depth * n_pool, GROUP, PAGE)
    hp, hs = x_prompt, x_sample
    outs = [[] for _ in range(7)]
    kv_prompt = None
    for l in range(depth):
        w = _layer_weights(l, ffn1_norm, ffn1_w_gate, ffn1_w_up, ffn1_w_down, mix_norm, w_in, q_norm, k_norm,
                           pool_w, pool_scale, conv_w, conv_b, conv_ln_g, conv_ln_b, conv_pw,
                           gate_norm, spatial_w, spatial_b, w_out, ffn2_norm, ffn2_w_gate, ffn2_w_up, ffn2_w_down)
        hp, kv_prompt, pp, cp = _prompt_layer(hp, w, depth, kv_prompt)
        hs, ks, vs, ps, cs, gs = _sample_layer(hs, ck, cv, page_table, l * n_pool, state_pool[l], state_conv[l], w)
        heads = lambda arr: arr.reshape(db, ts, N_HEADS, HEAD_DIM)
        for lst, val in zip(outs, (heads(ks), heads(vs), pp, ps, cp, cs, gs)):
            lst.append(val)
    k_prompt, v_prompt = (jnp.transpose(s.reshape(depth, b, N_HEADS, HEAD_DIM, t), (0, 1, 4, 2, 3)) for s in kv_prompt)
    return (hp, hs, k_prompt, v_prompt) + tuple(jnp.stack(lst) for lst in outs)
```

```python
import functools

import jax
import jax.numpy as jnp
from jax import lax
from jax.experimental import pallas as pl
from jax.experimental.pallas import tpu as pltpu

F32 = jnp.float32
BF16 = jnp.bfloat16
NEG_INF = float("-inf")

GROUP = 256
HEAD_DIM = 64
N_HEADS = GROUP // HEAD_DIM
MOBA_BLOCK = 256
MOBA_TOP_K = 3
PAGE = 128
POOL_WINDOWS = (2, 4, 8, 16)
POOL_HIST = 15
CONV_WIDTH = 31
CONV_HIST = 30
CHUNK = 128
FFN_ROWS = 512
PROJ_ROWS = 1024
SAMPLE_SEQS = 4
RMS_EPS = 1e-6
LN_EPS = 1e-5
VMEM_LIMIT = 56 * 1024 * 1024


def _cparams(sem):
    return pltpu.CompilerParams(dimension_semantics=sem, vmem_limit_bytes=VMEM_LIMIT)


def _rms(x, g):
    ms = jnp.mean(x * x, axis=-1, keepdims=True)
    return x * lax.rsqrt(ms + RMS_EPS) * g


def _dot(a, b):
    return jnp.dot(a, b, preferred_element_type=F32)


def _dot_nt(a, b):
    return lax.dot_general(a, b, (((1,), (1,)), ((), ())), preferred_element_type=F32)


def _split(a):
    hi = a.astype(BF16)
    lo = (a - hi.astype(F32)).astype(BF16)
    return hi, lo


def _dot3_nt(a, b):
    ah, al = _split(a)
    bh, bl = _split(b)
    return _dot_nt(ah, bh) + _dot_nt(ah, bl) + _dot_nt(al, bh)


def _lane_group(shape, width):
    return lax.broadcasted_iota(jnp.int32, shape, len(shape) - 1) // width


def _ffn_kernel(*refs, n_f, tf, n_mix, layer):
    x_ref = refs[0]
    mix_refs = refs[1:1 + n_mix]
    rest = refs[1 + n_mix:]
    if n_mix:
        wo_ref, rest = rest[0], rest[1:]
    g_ref, wg_hbm, wu_hbm, wd_hbm, o_ref, wg_b, wu_b, wd_b, gu_stage, d_stage, sems, acc_scr = rest
    first = pl.program_id(0) == 0

    def chunk_copies(c):
        slot = c % 2
        cols = pl.ds(c * tf, tf)
        return (pltpu.make_async_copy(wg_hbm.at[layer, :, cols], gu_stage.at[slot, 0], sems.at[slot, 0]),
                pltpu.make_async_copy(wu_hbm.at[layer, :, cols], gu_stage.at[slot, 1], sems.at[slot, 1]),
                pltpu.make_async_copy(wd_hbm.at[layer, cols, :], d_stage.at[slot], sems.at[slot, 2]))

    @pl.when(first)
    def _():
        for cp in chunk_copies(0):
            cp.start()

    x = x_ref[...]
    for idx, m_ref in enumerate(mix_refs):
        x = x + _dot(m_ref[...].astype(BF16), wo_ref[idx * GROUP:(idx + 1) * GROUP, :])
    h = _rms(x, g_ref[...]).astype(BF16)

    def land_chunk(c):
        cols = slice(c * tf, (c + 1) * tf)
        if c + 1 < n_f:
            for cp in chunk_copies(c + 1):
                cp.start()
        for cp in chunk_copies(c):
            cp.wait()
        wg_b[:, cols] = gu_stage[c % 2, 0].astype(BF16)
        wu_b[:, cols] = gu_stage[c % 2, 1].astype(BF16)
        wd_b[cols, :] = d_stage[c % 2].astype(BF16)

    def swiglu(fetch):
        for c in range(n_f):
            cols = slice(c * tf, (c + 1) * tf)
            fetch(c)
            gate = _dot(h, wg_b[:, cols])
            up = _dot(h, wu_b[:, cols])
            act = (gate * jax.nn.sigmoid(gate) * up).astype(BF16)
            part = _dot(act, wd_b[cols, :])
            if c == 0:
                acc_scr[...] = part
            else:
                acc_scr[...] += part
        o_ref[...] = x + 0.5 * acc_scr[...]

    @pl.when(first)
    def _():
        swiglu(land_chunk)

    @pl.when(jnp.logical_not(first))
    def _():
        swiglu(lambda c: None)


def _ffn(x, mixes, w_out, norm, wg, wu, wd, layer, tm, tf):
    n, d = x.shape
    f = wg.shape[2]
    n_mix = len(mixes)
    row = lambda i: (i, 0)
    hbm = pl.BlockSpec(memory_space=pl.ANY)
    in_specs = [pl.BlockSpec((tm, d), row)] + [pl.BlockSpec((tm, GROUP), row)] * n_mix
    args = [x, *mixes]
    if n_mix:
        in_specs.append(pl.BlockSpec(w_out.shape, lambda i: (0, 0)))
        args.append(w_out)
    in_specs += [pl.BlockSpec((1, d), lambda i: (0, 0)), hbm, hbm, hbm]
    args += [norm, wg, wu, wd]
    return pl.pallas_call(
        functools.partial(_ffn_kernel, n_f=f // tf, tf=tf, n_mix=n_mix, layer=layer),
        grid=(n // tm,),
        in_specs=in_specs,
        out_specs=pl.BlockSpec((tm, d), row),
        out_shape=jax.ShapeDtypeStruct((n, d), F32),
        scratch_shapes=[pltpu.VMEM((d, f), BF16), pltpu.VMEM((d, f), BF16), pltpu.VMEM((f, d), BF16),
                        pltpu.VMEM((2, 2, d, tf), F32), pltpu.VMEM((2, tf, d), F32),
                        pltpu.SemaphoreType.DMA((2, 3)), pltpu.VMEM((tm, d), F32)],
        compiler_params=_cparams(("arbitrary",)),
        name="ffn_out" if n_mix else "ffn",
    )(*args)


def _proj_in_kernel(*refs, kv_transposed, n_alias):
    x_ref, g_ref, w_ref, qn_ref, kn_ref, gn_ref, seg_ref = refs[:7]
    q_o, k_o, v_o, p_o, a_o, u_o, vg_o = refs[7 + n_alias:]
    G = GROUP
    h = _rms(x_ref[...], g_ref[...]).astype(BF16)
    z = _dot(h, w_ref[...])
    seg = seg_ref[...]

    def head_rms(t, gain):
        sq = t * t
        hi, lo = _split(sq)
        ms = _dot(hi, seg) + _dot(lo, seg)
        return t * lax.rsqrt(ms + RMS_EPS) * gain

    q_o[...] = head_rms(z[:, 0:G], qn_ref[...])
    k = head_rms(z[:, G:2 * G], kn_ref[...])
    v = z[:, 2 * G:3 * G]
    if not kv_transposed:
        k_o[...] = k
        v_o[...] = v
    elif len(k_o.shape) == 2:
        k_o[...] = k.T
        v_o[...] = v.T
    else:
        k_o[0] = k.T
        v_o[0] = v.T
        for slot in range(1, k_o.shape[0]):
            k_o[slot] = jnp.zeros(k_o.shape[1:], F32)
            v_o[slot] = jnp.zeros(v_o.shape[1:], F32)
    p_o[...] = z[:, 3 * G:4 * G]
    a_o[...] = z[:, 4 * G:5 * G] * jax.nn.sigmoid(z[:, 5 * G:6 * G])
    d = jax.nn.gelu(z[:, 6 * G:8 * G])
    u_o[...] = d[:, 0:G]
    vg_o[...] = _rms(d[:, G:2 * G], gn_ref[...])


def _proj_in(x, norm, w_in, qn, kn, gn, seg, tm, kv_stack=None):
    n, d = x.shape
    cols = w_in.shape[1]
    row = lambda i: (i, 0)
    fixed = lambda i: (0, 0)
    out = jax.ShapeDtypeStruct((n, GROUP), F32)
    in_specs = [pl.BlockSpec((tm, d), row),
                pl.BlockSpec((1, d), fixed),
                pl.BlockSpec((d, cols), fixed),
                pl.BlockSpec((1, GROUP), fixed),
                pl.BlockSpec((1, GROUP), fixed),
                pl.BlockSpec((1, GROUP), fixed),
                pl.BlockSpec((GROUP, GROUP), fixed)]
    args = [x, norm, w_in, qn, kn, gn, seg]
    out_specs = [pl.BlockSpec((tm, GROUP), row)] * 7
    out_shape = [out] * 7
    aliases = {}
    if kv_stack is not None:
        layer, depth, b, t, prev = kv_stack
        per_b = t // tm
        if prev is None:
            assert layer == 0
            kv_spec = pl.BlockSpec((depth, None, GROUP, tm), lambda i: (0, i // per_b, 0, i % per_b))
        else:
            kv_spec = pl.BlockSpec((None, None, GROUP, tm), lambda i: (layer, i // per_b, 0, i % per_b))
        out_specs[1:3] = [kv_spec, kv_spec]
        out_shape[1:3] = [jax.ShapeDtypeStruct((depth, b, GROUP, t), F32)] * 2
        if prev is not None:
            in_specs += [pl.BlockSpec(memory_space=pl.ANY)] * 2
            aliases = {len(args): 1, len(args) + 1: 2}
            args += list(prev)
    return pl.pallas_call(
        functools.partial(_proj_in_kernel, kv_transposed=kv_stack is not None, n_alias=len(aliases)),
        grid=(n // tm,),
        in_specs=in_specs,
        out_specs=out_specs,
        out_shape=out_shape,
        input_output_aliases=aliases,
        compiler_params=_cparams(("parallel",)),
        name="proj_in",
    )(*args)


GATE_ROWS = 8
VT_ROWS = HEAD_DIM + 16


def _select_bias_rows(gate, n_valid):
    rows = lax.broadcasted_iota(jnp.int32, gate.shape, 0)
    rows_f = rows.astype(F32)
    cur = jnp.where(rows < n_valid, gate, NEG_INF)
    bias = jnp.where(rows == n_valid, 0.0, NEG_INF)
    for _ in range(MOBA_TOP_K):
        top = jnp.max(cur, axis=0, keepdims=True)
        first = jnp.min(jnp.where(cur == top, rows_f, float(gate.shape[0])), axis=0, keepdims=True)
        pick = (rows_f == first) & (top > NEG_INF)
        bias = jnp.where(pick, 0.0, bias)
        cur = jnp.where(pick, NEG_INF, cur)
    return bias


def _attn_prompt_kernel(q_ref, k_ref, v_ref, o_ref, kb_scr, vt_scr, kbd_scr, bias_scr, qk_scr, alibi_scr,
                        *, n_blk):
    B = MOBA_BLOCK
    lane_head = _lane_group((1, GROUP), HEAD_DIM)
    slopes = [2.0 ** (-8.0 * (h + 1) / N_HEADS) for h in range(N_HEADS)]

    neg_rel = (lax.broadcasted_iota(jnp.int32, (B, B), 0)
               - lax.broadcasted_iota(jnp.int32, (B, B), 1)).astype(F32)
    for h in range(N_HEADS):
        alibi_scr[h] = slopes[h] * neg_rel
        alibi_scr[N_HEADS + h] = jnp.where(neg_rel <= 0.0, slopes[h] * neg_rel, NEG_INF)
    kbd_scr[...] = jnp.zeros_like(kbd_scr)
    ones = jnp.ones((VT_ROWS - HEAD_DIM, B), BF16)
    for n in range(n_blk):
        kn = k_ref[:, n * B:(n + 1) * B].T
        kb_scr[n] = kn.astype(BF16)
        mean = jnp.mean(kn, axis=0, keepdims=True)
        vt = v_ref[:, n * B:(n + 1) * B]
        for h in range(N_HEADS):
            r = h * GATE_ROWS + n
            kbd_scr[r:r + 1, :] = jnp.where(lane_head == h, mean, 0.0)
            vt_scr[n, h, 0:HEAD_DIM, :] = vt[h * HEAD_DIM:(h + 1) * HEAD_DIM, :].astype(BF16)
            vt_scr[n, h, HEAD_DIM:VT_ROWS, :] = ones

    for i in range(n_blk):
        rows = slice(i * B, (i + 1) * B)
        o_ref[0, rows, :] = _attend_block(i, q_ref[0, rows, :], slopes, kb_scr, vt_scr, kbd_scr, bias_scr,
                                          qk_scr, alibi_scr)


def _attend_block(i, q, slopes, kb_scr, vt_scr, kbd_scr, bias_scr, qk_scr, alibi_scr):
    B = MOBA_BLOCK
    gate = _dot3_nt(kbd_scr[...], q)
    for h in range(N_HEADS):
        rows = slice(h * GATE_ROWS, (h + 1) * GATE_ROWS)
        bias_scr[rows, :] = _select_bias_rows(gate[rows, :], i)

    q_scaled = q * (HEAD_DIM ** -0.5)
    in_head = _lane_group((B, GROUP), HEAD_DIM)
    qh = [jnp.where(in_head == h, q_scaled, 0.0).astype(BF16) for h in range(N_HEADS)]

    def scores(blk):
        kn = kb_scr[blk]
        return tuple(_dot_nt(kn, qh[h]) for h in range(N_HEADS))

    def absorb(blk, ms, accs):
        causal = jnp.where(blk == i, N_HEADS, 0)
        dist = ((i - blk) * B).astype(F32)
        ss, new_ms, ps = [], [], []
        for h in range(N_HEADS):
            bias = bias_scr[pl.ds(h * GATE_ROWS + blk, 1), :] - slopes[h] * dist
            ss.append(qk_scr[h] + alibi_scr[h + causal] + bias)
            new_ms.append(jnp.maximum(ms[h], jnp.max(ss[h], axis=0, keepdims=True)))
        for h in range(N_HEADS):
            ps.append(jnp.exp(ss[h] - new_ms[h]).astype(BF16))
        new_accs = [jnp.exp(ms[h] - new_ms[h]) * accs[h] + _dot(vt_scr[blk, h], ps[h])
                    for h in range(N_HEADS)]
        return tuple(new_ms), tuple(new_accs)

    def stash(qk):
        for h in range(N_HEADS):
            qk_scr[h] = qk[h]

    def body(j, carry):
        qk_next = scores(j)
        carry = absorb(jnp.where(j == 0, i, j - 1), *carry)
        stash(qk_next)
        return carry

    stash(scores(i))
    init = ((jnp.full((1, B), NEG_INF, F32),) * N_HEADS, (jnp.zeros((VT_ROWS, B), F32),) * N_HEADS)
    ms, accs = lax.fori_loop(0, i, body, init)
    ms, accs = absorb(jnp.where(i == 0, i, i - 1), ms, accs)
    out_t = jnp.concatenate([a[0:HEAD_DIM, :] / a[HEAD_DIM:HEAD_DIM + 1, :] for a in accs], axis=0)
    return out_t.T


def _attn_prompt(q, kt, vt, layer):
    b, t, g = q.shape
    n_blk = t // MOBA_BLOCK
    assert t % MOBA_BLOCK == 0 and n_blk <= GATE_ROWS
    kv_spec = pl.BlockSpec((None, None, g, t), lambda bi: (layer, bi, 0, 0))
    seq_spec = pl.BlockSpec((1, t, g), lambda bi: (bi, 0, 0))
    return pl.pallas_call(
        functools.partial(_attn_prompt_kernel, n_blk=n_blk),
        grid=(b,),
        in_specs=[seq_spec, kv_spec, kv_spec],
        out_specs=seq_spec,
        out_shape=jax.ShapeDtypeStruct((b, t, g), F32),
        scratch_shapes=[pltpu.VMEM((n_blk, MOBA_BLOCK, g), BF16),
                        pltpu.VMEM((n_blk, N_HEADS, VT_ROWS, MOBA_BLOCK), BF16),
                        pltpu.VMEM((N_HEADS * GATE_ROWS, g), F32),
                        pltpu.VMEM((N_HEADS * GATE_ROWS, MOBA_BLOCK), F32),
                        pltpu.VMEM((N_HEADS, MOBA_BLOCK, MOBA_BLOCK), F32),
                        pltpu.VMEM((2 * N_HEADS, MOBA_BLOCK, MOBA_BLOCK), F32)],
        compiler_params=_cparams(("parallel",)),
        name="attn_prompt",
    )(q, kt, vt)


def _attn_sample_kernel(pt_ref, q_ref, kn_ref, vn_ref, ck_hbm, cv_hbm, o_ref, k_buf, v_buf, sems,
                        *, n_pages, t_new, page_base, n_seq):
    b = pl.program_id(0)

    def page_copies(step, slot):
        out = []
        for s in range(n_seq):
            for pg in range(n_pages):
                page = page_base + pt_ref[step * n_seq + s, pg]
                dst = s * n_pages + pg
                out.append(pltpu.make_async_copy(ck_hbm.at[page], k_buf.at[slot, dst], sems.at[0, slot]))
                out.append(pltpu.make_async_copy(cv_hbm.at[page], v_buf.at[slot, dst], sems.at[1, slot]))
        return out

    @pl.when(b == 0)
    def _():
        for cp in page_copies(0, 0):
            cp.start()

    @pl.when(b + 1 < pl.num_programs(0))
    def _():
        for cp in page_copies(b + 1, (b + 1) % 2):
            cp.start()

    slot = b % 2
    for cp in page_copies(b, slot):
        cp.wait()
    seqs = []
    for s in range(n_seq):
        k_pages = [k_buf.at[slot, s * n_pages + pg] for pg in range(n_pages)]
        v_pages = [v_buf.at[slot, s * n_pages + pg] for pg in range(n_pages)]
        seqs.append((q_ref[s], kn_ref[s], vn_ref[s], k_pages, v_pages))
    scored = [_sample_scores(q, k_new, k_pages, t_new) for q, k_new, _, k_pages, _ in seqs]
    probs = [_sample_softmax(*sc, t_new) for sc in scored]
    for s, ((_, _, v_new, _, v_pages), (ps, l)) in enumerate(zip(seqs, probs)):
        o_ref[s] = _sample_values(ps, l, v_new, v_pages, t_new)


def _sample_scores(q_in, k_in, k_pages, t_new):
    T = t_new
    R = N_HEADS * T
    q = q_in * HEAD_DIM ** -0.5
    q_rows = jnp.concatenate([q] * N_HEADS, axis=0)
    row_head = lax.broadcasted_iota(jnp.int32, (R, GROUP), 0) // T
    qbd = jnp.where(_lane_group((R, GROUP), HEAD_DIM) == row_head, q_rows, 0.0)
    q_hi, q_lo = _split(qbd)
    raw = []
    for kp in k_pages:
        k_hi, k_lo = _split(kp[...])
        raw.append(_dot(q_hi, k_hi) + _dot(q_hi, k_lo) + _dot(q_lo, k_hi))
    pad = jnp.zeros((PAGE - T, GROUP), F32)
    k_new = jnp.concatenate([k_in, pad], axis=0).astype(BF16)
    return raw, _dot_nt(q_hi, k_new)


def _sample_softmax(raw, raw_new, t_new):
    T = t_new
    R = N_HEADS * T
    n_pages = len(raw)
    per_blk = MOBA_BLOCK // PAGE
    n_blk = n_pages // per_blk
    past = n_pages * PAGE
    gates = [jnp.sum(sum(raw[n * per_blk:(n + 1) * per_blk]), axis=-1, keepdims=True) for n in range(n_blk)]
    biases = []
    for n in range(n_blk):
        rank = jnp.zeros((R, 1), F32)
        for mth in range(n_blk):
            if mth != n:
                beats = (gates[mth] >= gates[n]) if mth < n else (gates[mth] > gates[n])
                rank = rank + jnp.where(beats, 1.0, 0.0)
        biases.append(jnp.where(rank < MOBA_TOP_K, 0.0, NEG_INF))

    r1 = lax.broadcasted_iota(jnp.int32, (R, 1), 0)
    slope = jnp.zeros((R, 1), F32)
    for h in range(N_HEADS):
        slope = jnp.where(r1 // T == h, 2.0 ** (-8.0 * (h + 1) / N_HEADS), slope)
    t_row = r1 % T
    pos_q = (past + t_row).astype(F32)
    col = lax.broadcasted_iota(jnp.int32, (R, PAGE), 1)
    scores = []
    for pg in range(n_pages):
        key_pos = (col + pg * PAGE).astype(F32)
        scores.append(raw[pg] - slope * (pos_q - key_pos) + biases[pg // per_blk])
    s_new = raw_new - slope * (t_row - col).astype(F32)
    scores.append(jnp.where(col <= t_row, s_new, NEG_INF))

    m = functools.reduce(jnp.maximum, [jnp.max(s, axis=-1, keepdims=True) for s in scores])
    ps = [jnp.exp(s - m) for s in scores]
    l = functools.reduce(lambda a, b: a + b, [jnp.sum(p, axis=-1, keepdims=True) for p in ps])
    return ps, l


def _sample_values(ps, l, v_in, v_pages, t_new):
    T = t_new
    pad = jnp.zeros((PAGE - T, GROUP), F32)
    v_new = jnp.concatenate([v_in, pad], axis=0).astype(BF16)
    acc = _dot(ps[-1].astype(BF16), v_new)
    for p, vp in zip(ps[:-1], v_pages):
        acc = acc + _dot_nt(p.astype(BF16), vp[...].astype(BF16))
    acc = acc / l
    lane_head = _lane_group((T, GROUP), HEAD_DIM)
    out = jnp.zeros((T, GROUP), F32)
    for h in range(N_HEADS):
        out = jnp.where(lane_head == h, acc[h * T:(h + 1) * T, :], out)
    return out


def _attn_sample(q, k_new, v_new, cache_k, cache_v, page_table, page_base):
    db, t, g = q.shape
    n_pages = page_table.shape[1]
    n_seq = SAMPLE_SEQS if db % SAMPLE_SEQS == 0 else 1
    tok = pl.BlockSpec((n_seq, t, g), lambda b, pt: (b, 0, 0))
    hbm = pl.BlockSpec(memory_space=pl.ANY)
    page_buf = pltpu.VMEM((2, n_seq * n_pages, g, PAGE), F32)
    return pl.pallas_call(
        functools.partial(_attn_sample_kernel, n_pages=n_pages, t_new=t, page_base=page_base, n_seq=n_seq),
        grid_spec=pltpu.PrefetchScalarGridSpec(
            num_scalar_prefetch=1,
            grid=(db // n_seq,),
            in_specs=[tok, tok, tok, hbm, hbm],
            out_specs=tok,
            scratch_shapes=[page_buf, page_buf, pltpu.SemaphoreType.DMA((2, 2))],
        ),
        out_shape=jax.ShapeDtypeStruct((db, t, g), F32),
        compiler_params=_cparams(("arbitrary",)),
        name="attn_sample",
    )(page_table, q, k_new, v_new, cache_k, cache_v)


def _layernorm_silu(y, g, b):
    mu = jnp.mean(y, axis=-1, keepdims=True)
    yc = y - mu
    var = jnp.mean(yc * yc, axis=-1, keepdims=True)
    yn = yc * lax.rsqrt(var + LN_EPS) * g + b
    return yn * jax.nn.sigmoid(yn)


def _pool_window_select(sums, shape):
    grp = _lane_group(shape, GROUP // len(POOL_WINDOWS))
    sel = sums[POOL_WINDOWS[-1]]
    for gi in range(len(POOL_WINDOWS) - 2, -1, -1):
        sel = jnp.where(grp == gi, sums[POOL_WINDOWS[gi]], sel)
    win = jnp.left_shift(jnp.int32(POOL_WINDOWS[0]), grp)
    return sel, win


MIX_ROWS = 512
HALO = 32


def _mix_prompt_kernel(pp_ref, p_ref, ap_ref, a_ref, u_ref, vg_ref, poolw_ref, pscale_ref,
                       convw_ref, convb_ref, lng_ref, lnb_ref, convpw_ref, sw_ref, sb_ref,
                       pool_o, conv_o, gmlp_o, p_scr, a_scr):
    c = pl.program_id(1)
    C = CHUNK
    has_prev = c > 0
    p_scr[0:HALO, :] = jnp.where(has_prev, pp_ref[0], 0.0)
    p_scr[HALO:HALO + MIX_ROWS, :] = p_ref[0]
    a_scr[0:HALO, :] = jnp.where(has_prev, ap_ref[0], 0.0)
    a_scr[HALO:HALO + MIX_ROWS, :] = a_ref[0]
    tri = lax.broadcasted_iota(jnp.int32, (C, C), 0) >= lax.broadcasted_iota(jnp.int32, (C, C), 1)
    w_causal = [jnp.where(tri, sw_ref[g], 0.0).astype(BF16) for g in range(N_HEADS)]
    grp = _lane_group((C, GROUP), HEAD_DIM)

    for sub in range(MIX_ROWS // C):
        s0 = HALO + sub * C
        rows = slice(sub * C, (sub + 1) * C)

        w2 = p_scr[s0 - 24:s0 + C, :] + p_scr[s0 - 25:s0 + C - 1, :]
        w4 = w2[8:C + 24] + w2[6:C + 22]
        w8 = w4[8:C + 16] + w4[4:C + 12]
        w16 = w8[8:C + 8] + w8[0:C]
        sums = {2: w2[24:C + 24], 4: w4[16:C + 16], 8: w8[8:C + 8], 16: w16}
        sel, win = _pool_window_select(sums, (C, GROUP))
        t_glob = c * MIX_ROWS + sub * C + lax.broadcasted_iota(jnp.int32, (C, GROUP), 0)
        count = jnp.minimum(t_glob + 1, win).astype(F32)
        m = sel / count - p_scr[s0:s0 + C, :]
        pool_o[0, rows, :] = _dot(m.astype(BF16), poolw_ref[...]) * pscale_ref[...]

        base = s0 - CONV_HIST
        y = jnp.zeros((C, GROUP), F32)
        for r in range(8):
            taps = [j for j in range(CONV_WIDTH) if (base + j) % 8 == r]
            n_rows = C + (8 if r else 0)
            z = jnp.zeros((n_rows, GROUP), F32)
            for j in taps:
                start = base + j - r
                z = z + a_scr[start:start + n_rows, :] * convw_ref[j:j + 1, :]
            y = y + z[r:r + C]
        y = _layernorm_silu(y + convb_ref[...], lng_ref[...], lnb_ref[...])
        conv_o[0, rows, :] = _dot(y.astype(BF16), convpw_ref[...])

        vgb = vg_ref[0, rows, :].astype(BF16)
        s = sb_ref[...]
        for g in range(N_HEADS):
            s = s + jnp.where(grp == g, _dot(w_causal[g], vgb), 0.0)
        gmlp_o[0, rows, :] = u_ref[0, rows, :] * s


def _mix_prompt(p, a, u, vg, poolw, pscale, convw, convb, lng, lnb, convpw, sw, sb):
    b, t, g = p.shape
    assert POOL_WINDOWS == (2, 4, 8, 16) and t % MIX_ROWS == 0 and MIX_ROWS % CHUNK == 0 and HALO >= CONV_HIST
    per = MIX_ROWS // HALO
    cur = pl.BlockSpec((1, MIX_ROWS, g), lambda bi, c: (bi, c, 0))
    prev = pl.BlockSpec((1, HALO, g), lambda bi, c: (bi, jnp.maximum(c * per - 1, 0), 0))

    def full(arr):
        nd = arr.ndim
        return pl.BlockSpec(arr.shape, lambda bi, c: (0,) * nd)

    out = jax.ShapeDtypeStruct((b, t, g), F32)
    weights = (poolw, pscale, convw, convb, lng, lnb, convpw, sw, sb)
    scratch = pltpu.VMEM((HALO + MIX_ROWS, g), F32)
    return pl.pallas_call(
        _mix_prompt_kernel,
        grid=(b, t // MIX_ROWS),
        in_specs=[prev, cur, prev, cur, cur, cur] + [full(w) for w in weights],
        out_specs=[cur] * 3,
        out_shape=[out] * 3,
        scratch_shapes=[scratch, scratch],
        compiler_params=_cparams(("parallel", "arbitrary")),
        name="mix_prompt",
    )(p, p, a, a, u, vg, *weights)


def _mix_sample_kernel(hp_ref, p_ref, hc_ref, a_ref, u_ref, vg_ref, poolw_ref, pscale_ref,
                       convw_ref, convb_ref, lng_ref, lnb_ref, convpw_ref, sw_ref, sb_ref,
                       pool_o, conv_o, gmlp_o, *, t_new):
    T = t_new
    nb = p_ref.shape[1]
    shape = (nb, GROUP)

    def pool_row(k):
        return hp_ref[k] if k < POOL_HIST else p_ref[k - POOL_HIST]

    def conv_row(k):
        return hc_ref[k] if k < CONV_HIST else a_ref[k - CONV_HIST]

    for t in range(T):
        acc = jnp.zeros(shape, F32)
        sums = {}
        for i in range(POOL_WINDOWS[-1]):
            k = POOL_HIST + t - i
            if k >= 0:
                acc = acc + pool_row(k)
            if i + 1 in POOL_WINDOWS:
                sums[i + 1] = acc
        sel, win = _pool_window_select(sums, shape)
        count = jnp.minimum(POOL_HIST + t + 1, win).astype(F32)
        m = sel / count - p_ref[t]
        pool_o[t] = _dot(m.astype(BF16), poolw_ref[...]) * pscale_ref[...]

        y = jnp.zeros(shape, F32)
        for j in range(CONV_WIDTH):
            y = y + conv_row(t + j) * convw_ref[j:j + 1, :]
        y = _layernorm_silu(y + convb_ref[...], lng_ref[...], lnb_ref[...])
        conv_o[t] = _dot(y.astype(BF16), convpw_ref[...])

        s = jnp.broadcast_to(sb_ref[t:t + 1, :], shape)
        for j in range(t + 1):
            s = s + sw_ref[t, j:j + 1, :] * vg_ref[j]
        gmlp_o[t] = u_ref[t] * s


def _mix_sample(hp, p, hc, a, u, vg, poolw, pscale, convw, convb, lng, lnb, convpw, sw, sb):
    t, db, g = p.shape

    def full(arr):
        nd = arr.ndim
        return pl.BlockSpec(arr.shape, lambda i: (0,) * nd)

    args = (hp, p, hc, a, u, vg, poolw, pscale, convw, convb, lng, lnb, convpw, sw, sb)
    out = jax.ShapeDtypeStruct((t, db, g), F32)
    return pl.pallas_call(
        functools.partial(_mix_sample_kernel, t_new=t),
        grid=(1,),
        in_specs=[full(x) for x in args],
        out_specs=[full(out)] * 3,
        out_shape=[out] * 3,
        compiler_params=_cparams(("arbitrary",)),
        name="mix_sample",
    )(*args)


def _row_tile(n, cap):
    t = min(n, cap)
    while n % t:
        t //= 2
    return t


def _ffn_tile(f):
    for tf in (512, 256, 128):
        if f % tf == 0:
            return tf
    return f


def _layer_weights(l, ffn1_norm, ffn1_w_gate, ffn1_w_up, ffn1_w_down, mix_norm, w_in, q_norm, k_norm,
                   pool_w, pool_scale, conv_w, conv_b, conv_ln_g, conv_ln_b, conv_pw,
                   gate_norm, spatial_w, spatial_b, w_out, ffn2_norm, ffn2_w_gate, ffn2_w_up, ffn2_w_down):
    row = lambda v: v[l].reshape(1, -1)
    head_id = jnp.arange(GROUP) // HEAD_DIM
    seg = jnp.where(head_id[:, None] == head_id[None, :], 1.0 / HEAD_DIM, 0.0).astype(BF16)
    n_win = len(POOL_WINDOWS)
    pg = GROUP // n_win
    pool_bd = jnp.zeros((GROUP, GROUP), F32)
    for gi in range(n_win):
        pool_bd = pool_bd.at[gi * pg:(gi + 1) * pg, gi * pg:(gi + 1) * pg].set(pool_w[l, gi])
    sb_lanes = jnp.repeat(spatial_b[l].T, HEAD_DIM, axis=1)
    return dict(
        layer=l,
        ffn1=(row(ffn1_norm), ffn1_w_gate, ffn1_w_up, ffn1_w_down),
        ffn2=(row(ffn2_norm), ffn2_w_gate, ffn2_w_up, ffn2_w_down),
        proj_in=(row(mix_norm), w_in[l].astype(BF16), jnp.tile(q_norm[l], N_HEADS).reshape(1, -1),
                 jnp.tile(k_norm[l], N_HEADS).reshape(1, -1), row(gate_norm), seg),
        w_out=w_out[l].astype(BF16),
        mix=(pool_bd.astype(BF16), row(pool_scale), conv_w[l], row(conv_b), row(conv_ln_g), row(conv_ln_b),
             conv_pw[l].astype(BF16)),
        spatial_w=spatial_w[l],
        spatial_b_lanes=sb_lanes,
    )


def _token_stage_in(x, w, kv_stack=None):
    n = x.shape[0]
    x1 = _ffn(x, (), None, *w["ffn1"], layer=w["layer"], tm=_row_tile(n, FFN_ROWS),
              tf=_ffn_tile(w["ffn1"][1].shape[2]))
    tm = _row_tile(n if kv_stack is None else kv_stack[3], PROJ_ROWS)
    return x1, _proj_in(x1, *w["proj_in"], tm=tm, kv_stack=kv_stack)


def _token_stage_out(x1, att, pool, conv, gmlp, w):
    n = x1.shape[0]
    return _ffn(x1, (att, pool, conv, gmlp), w["w_out"], *w["ffn2"], layer=w["layer"], tm=_row_tile(n, FFN_ROWS),
                tf=_ffn_tile(w["ffn2"][1].shape[2]))


def _prompt_layer(x, w, depth, kv_prev):
    b, t, d = x.shape
    x1, (q, kt, vt, p, a, u, vg) = _token_stage_in(x.reshape(b * t, d), w, (w["layer"], depth, b, t, kv_prev))
    r3 = lambda arr: arr.reshape(b, t, GROUP)
    q, p, a, u, vg = map(r3, (q, p, a, u, vg))
    att = _attn_prompt(q, kt, vt, w["layer"])
    pool, conv, gmlp = _mix_prompt(p, a, u, vg, *w["mix"], w["spatial_w"], w["spatial_b_lanes"])
    flat = lambda arr: arr.reshape(b * t, GROUP)
    y = _token_stage_out(x1, flat(att), flat(pool), flat(conv), flat(gmlp), w)
    return y.reshape(b, t, d), (kt, vt), p[:, -POOL_HIST:], a[:, -CONV_HIST:]


def _sample_layer(x, cache_k, cache_v, page_table, page_base, hist_pool, hist_conv, w):
    db, t, d = x.shape
    x1, (q, k, v, p, a, u, vg) = _token_stage_in(x.reshape(db * t, d), w)
    r3 = lambda arr: arr.reshape(db, t, GROUP)
    q, k, v, p, a, u, vg = map(r3, (q, k, v, p, a, u, vg))
    att = _attn_sample(q, k, v, cache_k, cache_v, page_table, page_base)
    tm = lambda arr: jnp.swapaxes(arr, 0, 1)
    sw_lanes = jnp.repeat(jnp.transpose(w["spatial_w"][:, :t, :t], (1, 2, 0)), HEAD_DIM, axis=2)
    hp_t, p_t, hc_t, a_t = tm(hist_pool), tm(p), tm(hist_conv), tm(a)
    pool, conv, gmlp = _mix_sample(hp_t, p_t, hc_t, a_t, tm(u), tm(vg),
                                   *w["mix"], sw_lanes, w["spatial_b_lanes"][:t])
    flat = lambda arr: tm(arr).reshape(db * t, GROUP)
    y = _token_stage_out(x1, att.reshape(db * t, GROUP), flat(pool), flat(conv), flat(gmlp), w)
    pool_state = tm(jnp.concatenate([hp_t, p_t], axis=0)[-POOL_HIST:])
    conv_state = tm(jnp.concatenate([hc_t, a_t], axis=0)[-CONV_HIST:])
    return y.reshape(db, t, d), k, v, pool_state, conv_state, vg


def kernel(x_prompt, x_sample, cache_k, cache_v, state_pool, state_conv, page_table, ffn1_norm, ffn1_w_gate, ffn1_w_up, ffn1_w_down, mix_norm, w_in, q_norm, k_norm, pool_w, pool_scale, conv_w, conv_b, conv_ln_g, conv_ln_b, conv_pw, gate_norm, spatial_w, spatial_b, w_out, ffn2_norm, ffn2_w_gate, ffn2_w_up, ffn2_w_down):
    depth, n_pool = cache_k.shape[0], cache_k.shape[1]
    b, t = x_prompt.shape[0], x_prompt.shape[1]
    db, ts = x_sample.shape[0], x_sample.shape[1]
    ck = jnp.transpose(cache_k, (0, 1, 3, 4, 2)).reshape(depth * n_pool, GROUP, PAGE)
    cv = jnp.transpose(cache_v, (0, 1, 3, 4, 2)).reshape(depth * n_pool, GROUP, PAGE)
    hp, hs = x_prompt, x_sample
    outs = [[] for _ in range(7)]
    kv_prompt = None
    for l in range(depth):
        w = _layer_weights(l, ffn1_norm, ffn1_w_gate, ffn1_w_up, ffn1_w_down, mix_norm, w_in, q_norm, k_norm,
                           pool_w, pool_scale, conv_w, conv_b, conv_ln_g, conv_ln_b, conv_pw,
                           gate_norm, spatial_w, spatial_b, w_out, ffn2_norm, ffn2_w_gate, ffn2_w_up, ffn2_w_down)
        hp, kv_prompt, pp, cp = _prompt_layer(hp, w, depth, kv_prompt)
        hs, ks, vs, ps, cs, gs = _sample_layer(hs, ck, cv, page_table, l * n_pool, state_pool[l], state_conv[l], w)
        heads = lambda arr: arr.reshape(db, ts, N_HEADS, HEAD_DIM)
        for lst, val in zip(outs, (heads(ks), heads(vs), pp, ps, cp, cs, gs)):
            lst.append(val)
    k_prompt, v_prompt = (jnp.transpose(s.reshape(depth, b, N_HEADS, HEAD_DIM, t), (0, 1, 4, 2, 3)) for s in kv_prompt)
    return (hp, hs, k_prompt, v_prompt) + tuple(jnp.stack(lst) for lst in outs)
```

```python
import functools

import jax
import jax.numpy as jnp
from jax import lax
from jax.experimental import pallas as pl
from jax.experimental.pallas import tpu as pltpu

F32 = jnp.float32
BF16 = jnp.bfloat16
NEG_INF = float("-inf")

GROUP = 256
HEAD_DIM = 64
N_HEADS = GROUP // HEAD_DIM
MOBA_BLOCK = 256
MOBA_TOP_K = 3
PAGE = 128
POOL_WINDOWS = (2, 4, 8, 16)
POOL_HIST = 15
CONV_WIDTH = 31
CONV_HIST = 30
CHUNK = 128
FFN_ROWS = 512
PROJ_ROWS = 1024
SAMPLE_SEQS = 4
RMS_EPS = 1e-6
LN_EPS = 1e-5
VMEM_LIMIT = 56 * 1024 * 1024


def _cparams(sem):
    return pltpu.CompilerParams(dimension_semantics=sem, vmem_limit_bytes=VMEM_LIMIT)


def _rms(x, g):
    ms = jnp.mean(x * x, axis=-1, keepdims=True)
    return x * lax.rsqrt(ms + RMS_EPS) * g


def _dot(a, b):
    return jnp.dot(a, b, preferred_element_type=F32)


def _dot_nt(a, b):
    return lax.dot_general(a, b, (((1,), (1,)), ((), ())), preferred_element_type=F32)


def _split(a):
    hi = a.astype(BF16)
    lo = (a - hi.astype(F32)).astype(BF16)
    return hi, lo


def _dot3_nt(a, b):
    ah, al = _split(a)
    bh, bl = _split(b)
    return _dot_nt(ah, bh) + _dot_nt(ah, bl) + _dot_nt(al, bh)


def _lane_group(shape, width):
    return lax.broadcasted_iota(jnp.int32, shape, len(shape) - 1) // width


def _ffn_kernel(*refs, n_f, tf, n_mix, layer):
    x_ref = refs[0]
    mix_refs = refs[1:1 + n_mix]
    rest = refs[1 + n_mix:]
    if n_mix:
        wo_ref, rest = rest[0], rest[1:]
    g_ref, wg_hbm, wu_hbm, wd_hbm, o_ref, wg_b, wu_b, wd_b, gu_stage, d_stage, sems, acc_scr = rest
    first = pl.program_id(0) == 0

    def chunk_copies(c):
        slot = c % 2
        cols = pl.ds(c * tf, tf)
        return (pltpu.make_async_copy(wg_hbm.at[layer, :, cols], gu_stage.at[slot, 0], sems.at[slot, 0]),
                pltpu.make_async_copy(wu_hbm.at[layer, :, cols], gu_stage.at[slot, 1], sems.at[slot, 1]),
                pltpu.make_async_copy(wd_hbm.at[layer, cols, :], d_stage.at[slot], sems.at[slot, 2]))

    @pl.when(first)
    def _():
        for cp in chunk_copies(0):
            cp.start()

    x = x_ref[...]
    for idx, m_ref in enumerate(mix_refs):
        x = x + _dot(m_ref[...].astype(BF16), wo_ref[idx * GROUP:(idx + 1) * GROUP, :])
    h = _rms(x, g_ref[...]).astype(BF16)

    def land_chunk(c):
        cols = slice(c * tf, (c + 1) * tf)
        if c + 1 < n_f:
            for cp in chunk_copies(c + 1):
                cp.start()
        for cp in chunk_copies(c):
            cp.wait()
        wg_b[:, cols] = gu_stage[c % 2, 0].astype(BF16)
        wu_b[:, cols] = gu_stage[c % 2, 1].astype(BF16)
        wd_b[cols, :] = d_stage[c % 2].astype(BF16)

    def swiglu(fetch):
        for c in range(n_f):
            cols = slice(c * tf, (c + 1) * tf)
            fetch(c)
            gate = _dot(h, wg_b[:, cols])
            up = _dot(h, wu_b[:, cols])
            act = (gate * jax.nn.sigmoid(gate) * up).astype(BF16)
            part = _dot(act, wd_b[cols, :])
            if c == 0:
                acc_scr[...] = part
            else:
                acc_scr[...] += part
        o_ref[...] = x + 0.5 * acc_scr[...]

    @pl.when(first)
    def _():
        swiglu(land_chunk)

    @pl.when(jnp.logical_not(first))
    def _():
        swiglu(lambda c: None)


def _ffn(x, mixes, w_out, norm, wg, wu, wd, layer, tm, tf):
    n, d = x.shape
    f = wg.shape[2]
    n_mix = len(mixes)
    row = lambda i: (i, 0)
    hbm = pl.BlockSpec(memory_space=pl.ANY)
    in_specs = [pl.BlockSpec((tm, d), row)] + [pl.BlockSpec((tm, GROUP), row)] * n_mix
    args = [x, *mixes]
    if n_mix:
        in_specs.append(pl.BlockSpec(w_out.shape, lambda i: (0, 0)))
        args.append(w_out)
    in_specs += [pl.BlockSpec((1, d), lambda i: (0, 0)), hbm, hbm, hbm]
    args += [norm, wg, wu, wd]
    return pl.pallas_call(
        functools.partial(_ffn_kernel, n_f=f // tf, tf=tf, n_mix=n_mix, layer=layer),
        grid=(n // tm,),
        in_specs=in_specs,
        out_specs=pl.BlockSpec((tm, d), row),
        out_shape=jax.ShapeDtypeStruct((n, d), F32),
        scratch_shapes=[pltpu.VMEM((d, f), BF16), pltpu.VMEM((d, f), BF16), pltpu.VMEM((f, d), BF16),
                        pltpu.VMEM((2, 2, d, tf), F32), pltpu.VMEM((2, tf, d), F32),
                        pltpu.SemaphoreType.DMA((2, 3)), pltpu.VMEM((tm, d), F32)],
        compiler_params=_cparams(("arbitrary",)),
        name="ffn_out" if n_mix else "ffn",
    )(*args)


def _proj_in_kernel(*refs, kv_transposed, n_alias):
    x_ref, g_ref, w_ref, qn_ref, kn_ref, gn_ref, seg_ref = refs[:7]
    q_o, k_o, v_o, p_o, a_o, u_o, vg_o = refs[7 + n_alias:]
    G = GROUP
    h = _rms(x_ref[...], g_ref[...]).astype(BF16)
    z = _dot(h, w_ref[...])
    seg = seg_ref[...]

    def head_rms(t, gain):
        sq = t * t
        hi, lo = _split(sq)
        ms = _dot(hi, seg) + _dot(lo, seg)
        return t * lax.rsqrt(ms + RMS_EPS) * gain

    q_o[...] = head_rms(z[:, 0:G], qn_ref[...])
    k = head_rms(z[:, G:2 * G], kn_ref[...])
    v = z[:, 2 * G:3 * G]
    if not kv_transposed:
        k_o[...] = k
        v_o[...] = v
    elif len(k_o.shape) == 2:
        k_o[...] = k.T
        v_o[...] = v.T
    else:
        k_o[0] = k.T
        v_o[0] = v.T
        for slot in range(1, k_o.shape[0]):
            k_o[slot] = jnp.zeros(k_o.shape[1:], F32)
            v_o[slot] = jnp.zeros(v_o.shape[1:], F32)
    p_o[...] = z[:, 3 * G:4 * G]
    a_o[...] = z[:, 4 * G:5 * G] * jax.nn.sigmoid(z[:, 5 * G:6 * G])
    d = jax.nn.gelu(z[:, 6 * G:8 * G])
    u_o[...] = d[:, 0:G]
    vg_o[...] = _rms(d[:, G:2 * G], gn_ref[...])


def _proj_in(x, norm, w_in, qn, kn, gn, seg, tm, kv_stack=None):
    n, d = x.shape
    cols = w_in.shape[1]
    row = lambda i: (i, 0)
    fixed = lambda i: (0, 0)
    out = jax.ShapeDtypeStruct((n, GROUP), F32)
    in_specs = [pl.BlockSpec((tm, d), row),
                pl.BlockSpec((1, d), fixed),
                pl.BlockSpec((d, cols), fixed),
                pl.BlockSpec((1, GROUP), fixed),
                pl.BlockSpec((1, GROUP), fixed),
                pl.BlockSpec((1, GROUP), fixed),
                pl.BlockSpec((GROUP, GROUP), fixed)]
    args = [x, norm, w_in, qn, kn, gn, seg]
    out_specs = [pl.BlockSpec((tm, GROUP), row)] * 7
    out_shape = [out] * 7
    aliases = {}
    if kv_stack is not None:
        layer, depth, b, t, prev = kv_stack
        per_b = t // tm
        if prev is None:
            assert layer == 0
            kv_spec = pl.BlockSpec((depth, None, GROUP, tm), lambda i: (0, i // per_b, 0, i % per_b))
        else:
            kv_spec = pl.BlockSpec((None, None, GROUP, tm), lambda i: (layer, i // per_b, 0, i % per_b))
        out_specs[1:3] = [kv_spec, kv_spec]
        out_shape[1:3] = [jax.ShapeDtypeStruct((depth, b, GROUP, t), F32)] * 2
        if prev is not None:
            in_specs += [pl.BlockSpec(memory_space=pl.ANY)] * 2
            aliases = {len(args): 1, len(args) + 1: 2}
            args += list(prev)
    return pl.pallas_call(
        functools.partial(_proj_in_kernel, kv_transposed=kv_stack is not None, n_alias=len(aliases)),
        grid=(n // tm,),
        in_specs=in_specs,
        out_specs=out_specs,
        out_shape=out_shape,
        input_output_aliases=aliases,
        compiler_params=_cparams(("parallel",)),
        name="proj_in",
    )(*args)


GATE_ROWS = 8
VT_ROWS = HEAD_DIM + 16


def _select_bias_rows(gate, n_valid):
    rows = lax.broadcasted_iota(jnp.int32, gate.shape, 0)
    rows_f = rows.astype(F32)
    cur = jnp.where(rows < n_valid, gate, NEG_INF)
    bias = jnp.where(rows == n_valid, 0.0, NEG_INF)
    for _ in range(MOBA_TOP_K):
        top = jnp.max(cur, axis=0, keepdims=True)
        first = jnp.min(jnp.where(cur == top, rows_f, float(gate.shape[0])), axis=0, keepdims=True)
        pick = (rows_f == first) & (top > NEG_INF)
        bias = jnp.where(pick, 0.0, bias)
        cur = jnp.where(pick, NEG_INF, cur)
    return bias


def _attn_prompt_kernel(q_ref, k_ref, v_ref, o_ref, kb_scr, vt_scr, kbd_scr, bias_scr, qk_scr, alibi_scr,
                        *, n_blk):
    B = MOBA_BLOCK
    lane_head = _lane_group((1, GROUP), HEAD_DIM)
    slopes = [2.0 ** (-8.0 * (h + 1) / N_HEADS) for h in range(N_HEADS)]

    neg_rel = (lax.broadcasted_iota(jnp.int32, (B, B), 0)
               - lax.broadcasted_iota(jnp.int32, (B, B), 1)).astype(F32)
    for h in range(N_HEADS):
        alibi_scr[h] = slopes[h] * neg_rel
        alibi_scr[N_HEADS + h] = jnp.where(neg_rel <= 0.0, slopes[h] * neg_rel, NEG_INF)
    kbd_scr[...] = jnp.zeros_like(kbd_scr)
    ones = jnp.ones((VT_ROWS - HEAD_DIM, B), BF16)
    for n in range(n_blk):
        kn = k_ref[:, n * B:(n + 1) * B].T
        kb_scr[n] = kn.astype(BF16)
        mean = jnp.mean(kn, axis=0, keepdims=True)
        vt = v_ref[:, n * B:(n + 1) * B]
        for h in range(N_HEADS):
            r = h * GATE_ROWS + n
            kbd_scr[r:r + 1, :] = jnp.where(lane_head == h, mean, 0.0)
            vt_scr[n, h, 0:HEAD_DIM, :] = vt[h * HEAD_DIM:(h + 1) * HEAD_DIM, :].astype(BF16)
            vt_scr[n, h, HEAD_DIM:VT_ROWS, :] = ones

    for i in range(n_blk):
        rows = slice(i * B, (i + 1) * B)
        o_ref[0, rows, :] = _attend_block(i, q_ref[0, rows, :], slopes, kb_scr, vt_scr, kbd_scr, bias_scr,
                                          qk_scr, alibi_scr)


def _attend_block(i, q, slopes, kb_scr, vt_scr, kbd_scr, bias_scr, qk_scr, alibi_scr):
    B = MOBA_BLOCK
    gate = _dot3_nt(kbd_scr[...], q)
    for h in range(N_HEADS):
        rows = slice(h * GATE_ROWS, (h + 1) * GATE_ROWS)
        bias_scr[rows, :] = _select_bias_rows(gate[rows, :], i)

    q_scaled = q * (HEAD_DIM ** -0.5)
    in_head = _lane_group((B, GROUP), HEAD_DIM)
    qh = [jnp.where(in_head == h, q_scaled, 0.0).astype(BF16) for h in range(N_HEADS)]

    def scores(blk):
        kn = kb_scr[blk]
        return tuple(_dot_nt(kn, qh[h]) for h in range(N_HEADS))

    def absorb(blk, ms, accs):
        causal = jnp.where(blk == i, N_HEADS, 0)
        dist = ((i - blk) * B).astype(F32)
        ss, new_ms, ps = [], [], []
        for h in range(N_HEADS):
            bias = bias_scr[pl.ds(h * GATE_ROWS + blk, 1), :] - slopes[h] * dist
            ss.append(qk_scr[h] + alibi_scr[h + causal] + bias)
            new_ms.append(jnp.maximum(ms[h], jnp.max(ss[h], axis=0, keepdims=True)))
        for h in range(N_HEADS):
            ps.append(jnp.exp(ss[h] - new_ms[h]).astype(BF16))
        new_accs = [jnp.exp(ms[h] - new_ms[h]) * accs[h] + _dot(vt_scr[blk, h], ps[h])
                    for h in range(N_HEADS)]
        return tuple(new_ms), tuple(new_accs)

    def stash(qk):
        for h in range(N_HEADS):
            qk_scr[h] = qk[h]

    def body(j, carry):
        qk_next = scores(j)
        carry = absorb(jnp.where(j == 0, i, j - 1), *carry)
        stash(qk_next)
        return carry

    stash(scores(i))
    init = ((jnp.full((1, B), NEG_INF, F32),) * N_HEADS, (jnp.zeros((VT_ROWS, B), F32),) * N_HEADS)
    ms, accs = init
    for j in range(i):
        ms, accs = body(j, (ms, accs))
    ms, accs = absorb(jnp.where(i == 0, i, i - 1), ms, accs)
    out_t = jnp.concatenate([a[0:HEAD_DIM, :] / a[HEAD_DIM:HEAD_DIM + 1, :] for a in accs], axis=0)
    return out_t.T


def _attn_prompt(q, kt, vt, layer):
    b, t, g = q.shape
    n_blk = t // MOBA_BLOCK
    assert t % MOBA_BLOCK == 0 and n_blk <= GATE_ROWS
    kv_spec = pl.BlockSpec((None, None, g, t), lambda bi: (layer, bi, 0, 0))
    seq_spec = pl.BlockSpec((1, t, g), lambda bi: (bi, 0, 0))
    return pl.pallas_call(
        functools.partial(_attn_prompt_kernel, n_blk=n_blk),
        grid=(b,),
        in_specs=[seq_spec, kv_spec, kv_spec],
        out_specs=seq_spec,
        out_shape=jax.ShapeDtypeStruct((b, t, g), F32),
        scratch_shapes=[pltpu.VMEM((n_blk, MOBA_BLOCK, g), BF16),
                        pltpu.VMEM((n_blk, N_HEADS, VT_ROWS, MOBA_BLOCK), BF16),
                        pltpu.VMEM((N_HEADS * GATE_ROWS, g), F32),
                        pltpu.VMEM((N_HEADS * GATE_ROWS, MOBA_BLOCK), F32),
                        pltpu.VMEM((N_HEADS, MOBA_BLOCK, MOBA_BLOCK), F32),
                        pltpu.VMEM((2 * N_HEADS, MOBA_BLOCK, MOBA_BLOCK), F32)],
        compiler_params=_cparams(("parallel",)),
        name="attn_prompt",
    )(q, kt, vt)


def _attn_sample_kernel(pt_ref, q_ref, kn_ref, vn_ref, ck_hbm, cv_hbm, o_ref, k_buf, v_buf, sems,
                        *, n_pages, t_new, page_base, n_seq):
    b = pl.program_id(0)

    def page_copies(step, slot):
        out = []
        for s in range(n_seq):
            for pg in range(n_pages):
                page = page_base + pt_ref[step * n_seq + s, pg]
                dst = s * n_pages + pg
                out.append(pltpu.make_async_copy(ck_hbm.at[page], k_buf.at[slot, dst], sems.at[0, slot]))
                out.append(pltpu.make_async_copy(cv_hbm.at[page], v_buf.at[slot, dst], sems.at[1, slot]))
        return out

    @pl.when(b == 0)
    def _():
        for cp in page_copies(0, 0):
            cp.start()

    @pl.when(b + 1 < pl.num_programs(0))
    def _():
        for cp in page_copies(b + 1, (b + 1) % 2):
            cp.start()

    slot = b % 2
    for cp in page_copies(b, slot):
        cp.wait()
    seqs = []
    for s in range(n_seq):
        k_pages = [k_buf.at[slot, s * n_pages + pg] for pg in range(n_pages)]
        v_pages = [v_buf.at[slot, s * n_pages + pg] for pg in range(n_pages)]
        seqs.append((q_ref[s], kn_ref[s], vn_ref[s], k_pages, v_pages))
    scored = [_sample_scores(q, k_new, k_pages, t_new) for q, k_new, _, k_pages, _ in seqs]
    probs = [_sample_softmax(*sc, t_new) for sc in scored]
    for s, ((_, _, v_new, _, v_pages), (ps, l)) in enumerate(zip(seqs, probs)):
        o_ref[s] = _sample_values(ps, l, v_new, v_pages, t_new)


def _sample_scores(q_in, k_in, k_pages, t_new):
    T = t_new
    R = N_HEADS * T
    q = q_in * HEAD_DIM ** -0.5
    q_rows = jnp.concatenate([q] * N_HEADS, axis=0)
    row_head = lax.broadcasted_iota(jnp.int32, (R, GROUP), 0) // T
    qbd = jnp.where(_lane_group((R, GROUP), HEAD_DIM) == row_head, q_rows, 0.0)
    q_hi, q_lo = _split(qbd)
    raw = []
    for kp in k_pages:
        k_hi, k_lo = _split(kp[...])
        raw.append(_dot(q_hi, k_hi) + _dot(q_hi, k_lo) + _dot(q_lo, k_hi))
    pad = jnp.zeros((PAGE - T, GROUP), F32)
    k_new = jnp.concatenate([k_in, pad], axis=0).astype(BF16)
    return raw, _dot_nt(q_hi, k_new)


def _sample_softmax(raw, raw_new, t_new):
    T = t_new
    R = N_HEADS * T
    n_pages = len(raw)
    per_blk = MOBA_BLOCK // PAGE
    n_blk = n_pages // per_blk
    past = n_pages * PAGE
    gates = [jnp.sum(sum(raw[n * per_blk:(n + 1) * per_blk]), axis=-1, keepdims=True) for n in range(n_blk)]
    biases = []
    for n in range(n_blk):
        rank = jnp.zeros((R, 1), F32)
        for mth in range(n_blk):
            if mth != n:
                beats = (gates[mth] >= gates[n]) if mth < n else (gates[mth] > gates[n])
                rank = rank + jnp.where(beats, 1.0, 0.0)
        biases.append(jnp.where(rank < MOBA_TOP_K, 0.0, NEG_INF))

    r1 = lax.broadcasted_iota(jnp.int32, (R, 1), 0)
    slope = jnp.zeros((R, 1), F32)
    for h in range(N_HEADS):
        slope = jnp.where(r1 // T == h, 2.0 ** (-8.0 * (h + 1) / N_HEADS), slope)
    t_row = r1 % T
    pos_q = (past + t_row).astype(F32)
    col = lax.broadcasted_iota(jnp.int32, (R, PAGE), 1)
    scores = []
    for pg in range(n_pages):
        key_pos = (col + pg * PAGE).astype(F32)
        scores.append(raw[pg] - slope * (pos_q - key_pos) + biases[pg // per_blk])
    s_new = raw_new - slope * (t_row - col).astype(F32)
    scores.append(jnp.where(col <= t_row, s_new, NEG_INF))

    m = functools.reduce(jnp.maximum, [jnp.max(s, axis=-1, keepdims=True) for s in scores])
    ps = [jnp.exp(s - m) for s in scores]
    l = functools.reduce(lambda a, b: a + b, [jnp.sum(p, axis=-1, keepdims=True) for p in ps])
    return ps, l


def _sample_values(ps, l, v_in, v_pages, t_new):
    T = t_new
    pad = jnp.zeros((PAGE - T, GROUP), F32)
    v_new = jnp.concatenate([v_in, pad], axis=0).astype(BF16)
    acc = _dot(ps[-1].astype(BF16), v_new)
    for p, vp in zip(ps[:-1], v_pages):
        acc = acc + _dot_nt(p.astype(BF16), vp[...].astype(BF16))
    acc = acc / l
    lane_head = _lane_group((T, GROUP), HEAD_DIM)
    out = jnp.zeros((T, GROUP), F32)
    for h in range(N_HEADS):
        out = jnp.where(lane_head == h, acc[h * T:(h + 1) * T, :], out)
    return out


def _attn_sample(q, k_new, v_new, cache_k, cache_v, page_table, page_base):
    db, t, g = q.shape
    n_pages = page_table.shape[1]
    n_seq = SAMPLE_SEQS if db % SAMPLE_SEQS == 0 else 1
    tok = pl.BlockSpec((n_seq, t, g), lambda b, pt: (b, 0, 0))
    hbm = pl.BlockSpec(memory_space=pl.ANY)
    page_buf = pltpu.VMEM((2, n_seq * n_pages, g, PAGE), F32)
    return pl.pallas_call(
        functools.partial(_attn_sample_kernel, n_pages=n_pages, t_new=t, page_base=page_base, n_seq=n_seq),
        grid_spec=pltpu.PrefetchScalarGridSpec(
            num_scalar_prefetch=1,
            grid=(db // n_seq,),
            in_specs=[tok, tok, tok, hbm, hbm],
            out_specs=tok,
            scratch_shapes=[page_buf, page_buf, pltpu.SemaphoreType.DMA((2, 2))],
        ),
        out_shape=jax.ShapeDtypeStruct((db, t, g), F32),
        compiler_params=_cparams(("arbitrary",)),
        name="attn_sample",
    )(page_table, q, k_new, v_new, cache_k, cache_v)


def _layernorm_silu(y, g, b):
    mu = jnp.mean(y, axis=-1, keepdims=True)
    yc = y - mu
    var = jnp.mean(yc * yc, axis=-1, keepdims=True)
    yn = yc * lax.rsqrt(var + LN_EPS) * g + b
    return yn * jax.nn.sigmoid(yn)


def _pool_window_select(sums, shape):
    grp = _lane_group(shape, GROUP // len(POOL_WINDOWS))
    sel = sums[POOL_WINDOWS[-1]]
    for gi in range(len(POOL_WINDOWS) - 2, -1, -1):
        sel = jnp.where(grp == gi, sums[POOL_WINDOWS[gi]], sel)
    win = jnp.left_shift(jnp.int32(POOL_WINDOWS[0]), grp)
    return sel, win


MIX_ROWS = 512
HALO = 32


def _mix_prompt_kernel(pp_ref, p_ref, ap_ref, a_ref, u_ref, vg_ref, poolw_ref, pscale_ref,
                       convw_ref, convb_ref, lng_ref, lnb_ref, convpw_ref, sw_ref, sb_ref,
                       pool_o, conv_o, gmlp_o, p_scr, a_scr):
    c = pl.program_id(1)
    C = CHUNK
    has_prev = c > 0
    p_scr[0:HALO, :] = jnp.where(has_prev, pp_ref[0], 0.0)
    p_scr[HALO:HALO + MIX_ROWS, :] = p_ref[0]
    a_scr[0:HALO, :] = jnp.where(has_prev, ap_ref[0], 0.0)
    a_scr[HALO:HALO + MIX_ROWS, :] = a_ref[0]
    tri = lax.broadcasted_iota(jnp.int32, (C, C), 0) >= lax.broadcasted_iota(jnp.int32, (C, C), 1)
    w_causal = [jnp.where(tri, sw_ref[g], 0.0).astype(BF16) for g in range(N_HEADS)]
    grp = _lane_group((C, GROUP), HEAD_DIM)

    for sub in range(MIX_ROWS // C):
        s0 = HALO + sub * C
        rows = slice(sub * C, (sub + 1) * C)

        w2 = p_scr[s0 - 24:s0 + C, :] + p_scr[s0 - 25:s0 + C - 1, :]
        w4 = w2[8:C + 24] + w2[6:C + 22]
        w8 = w4[8:C + 16] + w4[4:C + 12]
        w16 = w8[8:C + 8] + w8[0:C]
        sums = {2: w2[24:C + 24], 4: w4[16:C + 16], 8: w8[8:C + 8], 16: w16}
        sel, win = _pool_window_select(sums, (C, GROUP))
        t_glob = c * MIX_ROWS + sub * C + lax.broadcasted_iota(jnp.int32, (C, GROUP), 0)
        count = jnp.minimum(t_glob + 1, win).astype(F32)
        m = sel / count - p_scr[s0:s0 + C, :]
        pool_o[0, rows, :] = _dot(m.astype(BF16), poolw_ref[...]) * pscale_ref[...]

        base = s0 - CONV_HIST
        y = jnp.zeros((C, GROUP), F32)
        for r in range(8):
            taps = [j for j in range(CONV_WIDTH) if (base + j) % 8 == r]
            n_rows = C + (8 if r else 0)
            z = jnp.zeros((n_rows, GROUP), F32)
            for j in taps:
                start = base + j - r
                z = z + a_scr[start:start + n_rows, :] * convw_ref[j:j + 1, :]
            y = y + z[r:r + C]
        y = _layernorm_silu(y + convb_ref[...], lng_ref[...], lnb_ref[...])
        conv_o[0, rows, :] = _dot(y.astype(BF16), convpw_ref[...])

        vgb = vg_ref[0, rows, :].astype(BF16)
        s = sb_ref[...]
        for g in range(N_HEADS):
            s = s + jnp.where(grp == g, _dot(w_causal[g], vgb), 0.0)
        gmlp_o[0, rows, :] = u_ref[0, rows, :] * s


def _mix_prompt(p, a, u, vg, poolw, pscale, convw, convb, lng, lnb, convpw, sw, sb):
    b, t, g = p.shape
    assert POOL_WINDOWS == (2, 4, 8, 16) and t % MIX_ROWS == 0 and MIX_ROWS % CHUNK == 0 and HALO >= CONV_HIST
    per = MIX_ROWS // HALO
    cur = pl.BlockSpec((1, MIX_ROWS, g), lambda bi, c: (bi, c, 0))
    prev = pl.BlockSpec((1, HALO, g), lambda bi, c: (bi, jnp.maximum(c * per - 1, 0), 0))

    def full(arr):
        nd = arr.ndim
        return pl.BlockSpec(arr.shape, lambda bi, c: (0,) * nd)

    out = jax.ShapeDtypeStruct((b, t, g), F32)
    weights = (poolw, pscale, convw, convb, lng, lnb, convpw, sw, sb)
    scratch = pltpu.VMEM((HALO + MIX_ROWS, g), F32)
    return pl.pallas_call(
        _mix_prompt_kernel,
        grid=(b, t // MIX_ROWS),
        in_specs=[prev, cur, prev, cur, cur, cur] + [full(w) for w in weights],
        out_specs=[cur] * 3,
        out_shape=[out] * 3,
        scratch_shapes=[scratch, scratch],
        compiler_params=_cparams(("parallel", "arbitrary")),
        name="mix_prompt",
    )(p, p, a, a, u, vg, *weights)


def _mix_sample_kernel(hp_ref, p_ref, hc_ref, a_ref, u_ref, vg_ref, poolw_ref, pscale_ref,
                       convw_ref, convb_ref, lng_ref, lnb_ref, convpw_ref, sw_ref, sb_ref,
                       pool_o, conv_o, gmlp_o, *, t_new):
    T = t_new
    nb = p_ref.shape[1]
    shape = (nb, GROUP)

    def pool_row(k):
        return hp_ref[k] if k < POOL_HIST else p_ref[k - POOL_HIST]

    def conv_row(k):
        return hc_ref[k] if k < CONV_HIST else a_ref[k - CONV_HIST]

    for t in range(T):
        acc = jnp.zeros(shape, F32)
        sums = {}
        for i in range(POOL_WINDOWS[-1]):
            k = POOL_HIST + t - i
            if k >= 0:
                acc = acc + pool_row(k)
            if i + 1 in POOL_WINDOWS:
                sums[i + 1] = acc
        sel, win = _pool_window_select(sums, shape)
        count = jnp.minimum(POOL_HIST + t + 1, win).astype(F32)
        m = sel / count - p_ref[t]
        pool_o[t] = _dot(m.astype(BF16), poolw_ref[...]) * pscale_ref[...]

        y = jnp.zeros(shape, F32)
        for j in range(CONV_WIDTH):
            y = y + conv_row(t + j) * convw_ref[j:j + 1, :]
        y = _layernorm_silu(y + convb_ref[...], lng_ref[...], lnb_ref[...])
        conv_o[t] = _dot(y.astype(BF16), convpw_ref[...])

        s = jnp.broadcast_to(sb_ref[t:t + 1, :], shape)
        for j in range(t + 1):
            s = s + sw_ref[t, j:j + 1, :] * vg_ref[j]
        gmlp_o[t] = u_ref[t] * s


def _mix_sample(hp, p, hc, a, u, vg, poolw, pscale, convw, convb, lng, lnb, convpw, sw, sb):
    t, db, g = p.shape

    def full(arr):
        nd = arr.ndim
        return pl.BlockSpec(arr.shape, lambda i: (0,) * nd)

    args = (hp, p, hc, a, u, vg, poolw, pscale, convw, convb, lng, lnb, convpw, sw, sb)
    out = jax.ShapeDtypeStruct((t, db, g), F32)
    return pl.pallas_call(
        functools.partial(_mix_sample_kernel, t_new=t),
        grid=(1,),
        in_specs=[full(x) for x in args],
        out_specs=[full(out)] * 3,
        out_shape=[out] * 3,
        compiler_params=_cparams(("arbitrary",)),
        name="mix_sample",
    )(*args)


def _row_tile(n, cap):
    t = min(n, cap)
    while n % t:
        t //= 2
    return t


def _ffn_tile(f):
    for tf in (512, 256, 128):
        if f % tf == 0:
            return tf
    return f


def _layer_weights(l, ffn1_norm, ffn1_w_gate, ffn1_w_up, ffn1_w_down, mix_norm, w_in, q_norm, k_norm,
                   pool_w, pool_scale, conv_w, conv_b, conv_ln_g, conv_ln_b, conv_pw,
                   gate_norm, spatial_w, spatial_b, w_out, ffn2_norm, ffn2_w_gate, ffn2_w_up, ffn2_w_down):
    row = lambda v: v[l].reshape(1, -1)
    head_id = jnp.arange(GROUP) // HEAD_DIM
    seg = jnp.where(head_id[:, None] == head_id[None, :], 1.0 / HEAD_DIM, 0.0).astype(BF16)
    n_win = len(POOL_WINDOWS)
    pg = GROUP // n_win
    pool_bd = jnp.zeros((GROUP, GROUP), F32)
    for gi in range(n_win):
        pool_bd = pool_bd.at[gi * pg:(gi + 1) * pg, gi * pg:(gi + 1) * pg].set(pool_w[l, gi])
    sb_lanes = jnp.repeat(spatial_b[l].T, HEAD_DIM, axis=1)
    return dict(
        layer=l,
        ffn1=(row(ffn1_norm), ffn1_w_gate, ffn1_w_up, ffn1_w_down),
        ffn2=(row(ffn2_norm), ffn2_w_gate, ffn2_w_up, ffn2_w_down),
        proj_in=(row(mix_norm), w_in[l].astype(BF16), jnp.tile(q_norm[l], N_HEADS).reshape(1, -1),
                 jnp.tile(k_norm[l], N_HEADS).reshape(1, -1), row(gate_norm), seg),
        w_out=w_out[l].astype(BF16),
        mix=(pool_bd.astype(BF16), row(pool_scale), conv_w[l], row(conv_b), row(conv_ln_g), row(conv_ln_b),
             conv_pw[l].astype(BF16)),
        spatial_w=spatial_w[l],
        spatial_b_lanes=sb_lanes,
    )


def _token_stage_in(x, w, kv_stack=None):
    n = x.shape[0]
    x1 = _ffn(x, (), None, *w["ffn1"], layer=w["layer"], tm=_row_tile(n, FFN_ROWS),
              tf=_ffn_tile(w["ffn1"][1].shape[2]))
    tm = _row_tile(n if kv_stack is None else kv_stack[3], PROJ_ROWS)
    return x1, _proj_in(x1, *w["proj_in"], tm=tm, kv_stack=kv_stack)


def _token_stage_out(x1, att, pool, conv, gmlp, w):
    n = x1.shape[0]
    return _ffn(x1, (att, pool, conv, gmlp), w["w_out"], *w["ffn2"], layer=w["layer"], tm=_row_tile(n, FFN_ROWS),
                tf=_ffn_tile(w["ffn2"][1].shape[2]))


def _prompt_layer(x, w, depth, kv_prev):
    b, t, d = x.shape
    x1, (q, kt, vt, p, a, u, vg) = _token_stage_in(x.reshape(b * t, d), w, (w["layer"], depth, b, t, kv_prev))
    r3 = lambda arr: arr.reshape(b, t, GROUP)
    q, p, a, u, vg = map(r3, (q, p, a, u, vg))
    att = _attn_prompt(q, kt, vt, w["layer"])
    pool, conv, gmlp = _mix_prompt(p, a, u, vg, *w["mix"], w["spatial_w"], w["spatial_b_lanes"])
    flat = lambda arr: arr.reshape(b * t, GROUP)
    y = _token_stage_out(x1, flat(att), flat(pool), flat(conv), flat(gmlp), w)
    return y.reshape(b, t, d), (kt, vt), p[:, -POOL_HIST:], a[:, -CONV_HIST:]


def _sample_layer(x, cache_k, cache_v, page_table, page_base, hist_pool, hist_conv, w):
    db, t, d = x.shape
    x1, (q, k, v, p, a, u, vg) = _token_stage_in(x.reshape(db * t, d), w)
    r3 = lambda arr: arr.reshape(db, t, GROUP)
    q, k, v, p, a, u, vg = map(r3, (q, k, v, p, a, u, vg))
    att = _attn_sample(q, k, v, cache_k, cache_v, page_table, page_base)
    tm = lambda arr: jnp.swapaxes(arr, 0, 1)
    sw_lanes = jnp.repeat(jnp.transpose(w["spatial_w"][:, :t, :t], (1, 2, 0)), HEAD_DIM, axis=2)
    hp_t, p_t, hc_t, a_t = tm(hist_pool), tm(p), tm(hist_conv), tm(a)
    pool, conv, gmlp = _mix_sample(hp_t, p_t, hc_t, a_t, tm(u), tm(vg),
                                   *w["mix"], sw_lanes, w["spatial_b_lanes"][:t])
    flat = lambda arr: tm(arr).reshape(db * t, GROUP)
    y = _token_stage_out(x1, att.reshape(db * t, GROUP), flat(pool), flat(conv), flat(gmlp), w)
    pool_state = tm(jnp.concatenate([hp_t, p_t], axis=0)[-POOL_HIST:])
    conv_state = tm(jnp.concatenate([hc_t, a_t], axis=0)[-CONV_HIST:])
    return y.reshape(db, t, d), k, v, pool_state, conv_state, vg


def kernel(x_prompt, x_sample, cache_k, cache_v, state_pool, state_conv, page_table, ffn1_norm, ffn1_w_gate, ffn1_w_up, ffn1_w_down, mix_norm, w_in, q_norm, k_norm, pool_w, pool_scale, conv_w, conv_b, conv_ln_g, conv_ln_b, conv_pw, gate_norm, spatial_w, spatial_b, w_out, ffn2_norm, ffn2_w_gate, ffn2_w_up, ffn2_w_down):
    depth, n_pool = cache_k.shape[0], cache_k.shape[1]
    b, t = x_prompt.shape[0], x_prompt.shape[1]
    db, ts = x_sample.shape[0], x_sample.shape[1]
    ck = jnp.transpose(cache_k, (0, 1, 3, 4, 2)).reshape(depth * n_pool, GROUP, PAGE)
    cv = jnp.transpose(cache_v, (0, 1, 3, 4, 2)).reshape(depth * n_pool, GROUP, PAGE)
    hp, hs = x_prompt, x_sample
    outs = [[] for _ in range(7)]
    kv_prompt = None
    for l in range(depth):
        w = _layer_weights(l, ffn1_norm, ffn1_w_gate, ffn1_w_up, ffn1_w_down, mix_norm, w_in, q_norm, k_norm,
                           pool_w, pool_scale, conv_w, conv_b, conv_ln_g, conv_ln_b, conv_pw,
                           gate_norm, spatial_w, spatial_b, w_out, ffn2_norm, ffn2_w_gate, ffn2_w_up, ffn2_w_down)
        hp, kv_prompt, pp, cp = _prompt_layer(hp, w, depth, kv_prompt)
        hs, ks, vs, ps, cs, gs = _sample_layer(hs, ck, cv, page_table, l * n_pool, state_pool[l], state_conv[l], w)
        heads = lambda arr: arr.reshape(db, ts, N_HEADS, HEAD_DIM)
        for lst, val in zip(outs, (heads(ks), heads(vs), pp, ps, cp, cs, gs)):
            lst.append(val)
    k_prompt, v_prompt = (jnp.transpose(s.reshape(depth, b, N_HEADS, HEAD_DIM, t), (0, 1, 4, 2, 3)) for s in kv_prompt)
    return (hp, hs, k_prompt, v_prompt) + tuple(jnp.stack(lst) for lst in outs)
```
